```python
import math
import jax
import jax.numpy as jnp
from jax import lax
import numpy as np

D_MODEL = 2048
BATCH = 4
SEQ = 4096
DEPTH = 1

GRID_W = 64
CTX_LEN = 256
D_SSM = 2048
SSM_HEAD_DIM = 64
SSM_HEADS = D_SSM // SSM_HEAD_DIM
SSM_GROUPS = 4
SSM_HPG = SSM_HEADS // SSM_GROUPS
SSM_STATE = 128
SSM_CONV = 3
SSD_CHUNK = 128
D_XBC = D_SSM + 2 * SSM_GROUPS * SSM_STATE
D_SSM_IN = D_SSM + D_XBC + 2 * SSM_HEADS
D_SC = 2048
SC_CONV = 3
D_MIX = D_SSM + D_SC
D_IN_PROJ = D_SSM_IN + 3 * D_SC
N_EXPERTS = 16
EC_CAPACITY_FACTOR = 2
EXPERT_FF = 1024
N_MOD = 6
EPS = 1e-6

kernel_name = 'hybrid_ssd_shortconv_ec_dit_layer'


def rmsnorm(x, w):
    xf = x.astype(jnp.float32)
    xf = xf * lax.rsqrt(jnp.mean(xf * xf, axis=-1, keepdims=True) + EPS)
    return (xf * w.astype(jnp.float32)).astype(x.dtype)


def group_rmsnorm(y, w, groups):
    shp = y.shape
    yf = y.astype(jnp.float32).reshape(*shp[:-1], groups, shp[-1] // groups)
    yf = yf * lax.rsqrt(jnp.mean(yf * yf, axis=-1, keepdims=True) + EPS)
    return yf.reshape(shp) * w.astype(jnp.float32)


def modulate(h, shift, scale):
    return h * (1 + scale) + shift


def _dwconv(u, w, axis):
    k = w.shape[0]
    half = k // 2
    n = u.shape[axis]
    pad = [(0, 0)] * u.ndim
    pad[axis] = (half, half)
    up = jnp.pad(u, pad)
    return sum(lax.slice_in_dim(up, j, j + n, axis=axis) * w[j] for j in range(k))


def dwconv_ctx(u, w):
    return _dwconv(u, w, axis=1)


def dwconv_grid(u, w, rows):
    b, n, ch = u.shape
    g = u.reshape(b, rows, GRID_W, ch)
    return _dwconv(g, w, axis=2).reshape(b, n, ch)


def segsum(a):
    t = a.shape[-1]
    cs = jnp.cumsum(a, axis=-1)
    diff = cs[..., :, None] - cs[..., None, :]
    mask = jnp.tril(jnp.ones((t, t), dtype=bool))
    return jnp.where(mask, diff, -jnp.inf)


def ssd_chunked(xs, dt, a_neg, bm, cm, init):
    b, l, g, r, p = xs.shape
    n = bm.shape[-1]
    q = SSD_CHUNK
    nc = l // q
    xdt = (xs * dt[..., None]).reshape(b, nc, q, g, r, p)
    la = jnp.moveaxis((dt * a_neg).reshape(b, nc, q, g, r), 2, -1)
    la_cum = jnp.cumsum(la, axis=-1)
    bc = bm.reshape(b, nc, q, g, n)
    cc = cm.reshape(b, nc, q, g, n)
    decay_in = jnp.exp(segsum(la))
    cb = jnp.einsum('bcign,bcjgn->bcgij', cc, bc)
    y_diag = jnp.einsum('bcgij,bcgrij,bcjgrp->bcigrp', cb, decay_in, xdt)
    decay_to_end = jnp.exp(la_cum[..., -1:] - la_cum)
    chunk_states = jnp.einsum('bcjgn,bcgrj,bcjgrp->bcgrpn', bc, decay_to_end, xdt)
    states = jnp.concatenate([init[:, None], chunk_states], axis=1)
    chunk_decay = jnp.pad(la_cum[..., -1], ((0, 0), (1, 0), (0, 0), (0, 0)))
    decay_chunks = jnp.exp(segsum(jnp.moveaxis(chunk_decay, 1, -1)))
    states = jnp.einsum('bgrzc,bcgrpn->bzgrpn', decay_chunks, states)
    y_off = jnp.einsum('bcign,bcgrpn,bcgri->bcigrp', cc, states[:, :-1], jnp.exp(la_cum))
    y = (y_diag + y_off).reshape(b, l, g, r, p)
    return y, states[:, -1]


def scan_direction(xs, dt, a_neg, bm, cm, n_ctx, reverse):
    def orient(t):
        return jnp.flip(t, axis=1) if reverse else t
    b = xs.shape[0]
    state = jnp.zeros((b, SSM_GROUPS, SSM_HPG, SSM_HEAD_DIM, SSM_STATE), jnp.float32)
    parts = []
    for seg in (slice(0, n_ctx), slice(n_ctx, None)):
        y, state = ssd_chunked(orient(xs[:, seg]), orient(dt[:, seg]), a_neg,
                               orient(bm[:, seg]), orient(cm[:, seg]), state)
        parts.append(orient(y))
    return jnp.concatenate(parts, axis=1)


def ssd_branch(u, n_ctx, rows, conv_w, conv_b, a_log, dt_bias, d_skip, norm_w):
    dtype = u.dtype
    z = u[..., :D_SSM]
    xbc = u[..., D_SSM:D_SSM + D_XBC]
    dt_raw = u[..., D_SSM + D_XBC:]
    xbc = jnp.concatenate([dwconv_ctx(xbc[:, :n_ctx], conv_w),
                           dwconv_grid(xbc[:, n_ctx:], conv_w, rows)], axis=1)
    xbc = jax.nn.silu(xbc + conv_b).astype(jnp.float32)
    b, l, _ = xbc.shape
    gn = SSM_GROUPS * SSM_STATE
    xs = xbc[..., :D_SSM].reshape(b, l, SSM_GROUPS, SSM_HPG, SSM_HEAD_DIM)
    bm = xbc[..., D_SSM:D_SSM + gn].reshape(b, l, SSM_GROUPS, SSM_STATE)
    cm = xbc[..., D_SSM + gn:].reshape(b, l, SSM_GROUPS, SSM_STATE)
    dt = jax.nn.softplus(dt_raw.astype(jnp.float32).reshape(b, l, 2, SSM_GROUPS, SSM_HPG)
                         + dt_bias.astype(jnp.float32).reshape(2, SSM_GROUPS, SSM_HPG))
    a_neg = -jnp.exp(a_log.astype(jnp.float32)).reshape(2, SSM_GROUPS, SSM_HPG)
    y = d_skip.astype(jnp.float32).reshape(SSM_GROUPS, SSM_HPG, 1) * xs
    for direction, reverse in ((0, False), (1, True)):
        y = y + scan_direction(xs, dt[:, :, direction], a_neg[direction], bm, cm, n_ctx, reverse)
    y = y.reshape(b, l, D_SSM) * jax.nn.silu(z.astype(jnp.float32))
    return group_rmsnorm(y, norm_w, SSM_GROUPS).astype(dtype)


def short_conv_branch(u, n_ctx, rows, conv_w, norm_w):
    bg, cg, hv = jnp.split(u, 3, axis=-1)
    v = cg * hv
    v = jnp.concatenate([dwconv_ctx(v[:, :n_ctx], conv_w),
                         dwconv_grid(v[:, n_ctx:], conv_w, rows)], axis=1)
    return rmsnorm(bg * v, norm_w)


def expert_choice_ffn(h, w_router, w_gate, w_up, w_down):
    b, n, d = h.shape
    cap = EC_CAPACITY_FACTOR * n // N_EXPERTS
    aff = jax.nn.softmax(jnp.einsum('bnd,de->bne', h, w_router).astype(jnp.float32), axis=-1)
    gates, idx = lax.top_k(jnp.swapaxes(aff, 1, 2), cap)
    xe = jax.vmap(lambda hb, ib: hb[ib])(h, idx)
    hid = jax.nn.silu(jnp.einsum('becd,edf->becf', xe, w_gate)) * jnp.einsum('becd,edf->becf', xe, w_up)
    ye = jnp.einsum('becf,efd->becd', hid, w_down) * gates[..., None].astype(h.dtype)
    def combine(yb, ib):
        return jnp.zeros((n, d), h.dtype).at[ib.reshape(-1)].add(yb.reshape(-1, d))
    return jax.vmap(combine)(ye, idx)


def setup_inputs(seed: int = 0) -> dict:
    key = jax.random.key(seed)
    ks = jax.random.split(key, 24)
    f32 = jnp.float32
    L = DEPTH

    def nrm(k, shape, s):
        return s * jax.random.normal(k, shape, f32)

    def gain(k, shape):
        return 1.0 + 0.05 * jax.random.normal(k, shape, f32)

    dt0 = jnp.exp(jax.random.uniform(ks[11], (L, 2 * SSM_HEADS), f32, math.log(1e-3), math.log(1e-1)))
    return {
        'x': jax.random.normal(ks[0], (BATCH, SEQ, D_MODEL), f32),
        'c': jax.random.normal(ks[1], (BATCH, D_MODEL), f32),
        'ctx': jax.random.normal(ks[2], (BATCH, CTX_LEN, D_MODEL), f32),
        'c_ctx': jax.random.normal(ks[3], (D_MODEL,), f32),
        'w_mod': nrm(ks[4], (L, D_MODEL, N_MOD * D_MODEL), 0.5 * D_MODEL ** -0.5),
        'b_mod': nrm(ks[5], (L, N_MOD * D_MODEL), 0.02),
        'norm_mix_w': gain(ks[6], (L, D_MODEL)),
        'w_in': nrm(ks[7], (L, D_MODEL, D_IN_PROJ), D_MODEL ** -0.5),
        'ssm_conv_w': nrm(ks[8], (L, SSM_CONV, D_XBC), SSM_CONV ** -0.5),
        'ssm_conv_b': nrm(ks[9], (L, D_XBC), 0.02),
        'ssm_a_log': jnp.log(jax.random.uniform(ks[10], (L, 2 * SSM_HEADS), f32, 1.0, 16.0)),
        'ssm_dt_bias': dt0 + jnp.log(-jnp.expm1(-dt0)),
        'ssm_d': gain(ks[12], (L, SSM_HEADS)),
        'ssm_norm_w': gain(ks[13], (L, D_SSM)),
        'sc_conv_w': nrm(ks[14], (L, SC_CONV, D_SC), SC_CONV ** -0.5),
        'sc_norm_w': gain(ks[15], (L, D_SC)),
        'w_out': nrm(ks[16], (L, D_MIX, D_MODEL), D_MIX ** -0.5),
        'norm_ffn_w': gain(ks[17], (L, D_MODEL)),
        'w_router': nrm(ks[18], (L, D_MODEL, N_EXPERTS), D_MODEL ** -0.5),
        'w_gate': nrm(ks[19], (L, N_EXPERTS, D_MODEL, EXPERT_FF), D_MODEL ** -0.5),
        'w_up': nrm(ks[20], (L, N_EXPERTS, D_MODEL, EXPERT_FF), D_MODEL ** -0.5),
        'w_down': nrm(ks[21], (L, N_EXPERTS, EXPERT_FF, D_MODEL), EXPERT_FF ** -0.5),
        'final_norm_w': gain(ks[22], (D_MODEL,)),
    }


def reference(x, c, ctx, c_ctx, w_mod, b_mod, norm_mix_w, w_in, ssm_conv_w, ssm_conv_b, ssm_a_log,
              ssm_dt_bias, ssm_d, ssm_norm_w, sc_conv_w, sc_norm_w, w_out, norm_ffn_w, w_router,
              w_gate, w_up, w_down, final_norm_w):
    n_ctx = ctx.shape[1]
    rows = x.shape[1] // GRID_W
    h_lat, h_ctx = x, ctx
    for layer in range(DEPTH):
        mod_lat = jnp.einsum('bd,de->be', jax.nn.silu(c), w_mod[layer]) + b_mod[layer]
        mod_ctx = jax.nn.silu(c_ctx) @ w_mod[layer] + b_mod[layer]
        sh1_l, sc1_l, g1_l, sh2_l, sc2_l, g2_l = jnp.split(mod_lat[:, None, :], N_MOD, axis=-1)
        sh1_c, sc1_c, g1_c, sh2_c, sc2_c, g2_c = jnp.split(mod_ctx[None, None, :], N_MOD, axis=-1)

        a_ctx = modulate(rmsnorm(h_ctx, norm_mix_w[layer]), sh1_c, sc1_c)
        a_lat = modulate(rmsnorm(h_lat, norm_mix_w[layer]), sh1_l, sc1_l)
        u = jnp.einsum('bld,de->ble', jnp.concatenate([a_ctx, a_lat], axis=1), w_in[layer])
        y_ssm = ssd_branch(u[..., :D_SSM_IN], n_ctx, rows, ssm_conv_w[layer], ssm_conv_b[layer],
                           ssm_a_log[layer], ssm_dt_bias[layer], ssm_d[layer], ssm_norm_w[layer])
        y_sc = short_conv_branch(u[..., D_SSM_IN:], n_ctx, rows, sc_conv_w[layer], sc_norm_w[layer])
        mix = jnp.einsum('ble,ed->bld', jnp.concatenate([y_ssm, y_sc], axis=-1), w_out[layer])
        h_lat = h_lat + g1_l * mix[:, n_ctx:]

        if layer + 1 < DEPTH:
            h_ctx = h_ctx + g1_c * mix[:, :n_ctx]
            f_ctx = modulate(rmsnorm(h_ctx, norm_ffn_w[layer]), sh2_c, sc2_c)
            h_ctx = h_ctx + g2_c * expert_choice_ffn(f_ctx, w_router[layer], w_gate[layer],
                                                     w_up[layer], w_down[layer])

        f_lat = modulate(rmsnorm(h_lat, norm_ffn_w[layer]), sh2_l, sc2_l)
        h_lat = h_lat + g2_l * expert_choice_ffn(f_lat, w_router[layer], w_gate[layer],
                                                 w_up[layer], w_down[layer])
    return rmsnorm(h_lat, final_norm_w)
```

```python
import functools

import jax
import jax.numpy as jnp
from jax import lax
from jax.experimental import pallas as pl
from jax.experimental.pallas import tpu as pltpu

F32 = jnp.float32
BF16 = jnp.bfloat16
I32 = jnp.int32

D_MODEL = 2048
GRID_W = 64
D_SSM = 2048
SSM_HEAD_DIM = 64
SSM_HEADS = D_SSM // SSM_HEAD_DIM
SSM_GROUPS = 4
SSM_HPG = SSM_HEADS // SSM_GROUPS
SSM_STATE = 128
SSD_CHUNK = 128
GN = SSM_GROUPS * SSM_STATE
D_XBC = D_SSM + 2 * GN
D_ZX = D_SSM + D_XBC
N_DT = 2 * SSM_HEADS
D_SSM_IN = D_ZX + N_DT
D_SC = 2048
N_EXPERTS = 16
EC_CAPACITY_FACTOR = 2
EXPERT_FF = 1024
N_MOD = 6
EPS = 1e-6

LANES = 128
BLK = 2 * SSD_CHUNK
GROUP_W = SSM_HPG * SSM_HEAD_DIM
VMEM_LIMIT = 56 * 1024 * 1024


def _cparams(sem):
    return pltpu.CompilerParams(dimension_semantics=sem, vmem_limit_bytes=VMEM_LIMIT)


def _silu(v):
    return v * jax.nn.sigmoid(v)


def _softplus(v):
    return jnp.maximum(v, 0.0) + jnp.log1p(jnp.exp(-jnp.abs(v)))


def _mod_kernel(c_ref, w_ref, b_ref, o_ref):
    s = _silu(c_ref[...]).astype(BF16)
    o_ref[...] = jnp.dot(s, w_ref[...].astype(BF16), preferred_element_type=F32) + b_ref[...]


def _mod_call(cvec, w_mod, b_mod):
    rows, d = cvec.shape
    n = w_mod.shape[1]
    tn = 1024
    return pl.pallas_call(
        _mod_kernel,
        grid=(n // tn,),
        in_specs=[pl.BlockSpec((rows, d), lambda j: (0, 0)),
                  pl.BlockSpec((d, tn), lambda j: (0, j)),
                  pl.BlockSpec((1, tn), lambda j: (0, j))],
        out_specs=pl.BlockSpec((rows, tn), lambda j: (0, j)),
        out_shape=jax.ShapeDtypeStruct((rows, n), F32),
        compiler_params=_cparams(("arbitrary",)),
        name="mod",
    )(cvec, w_mod, b_mod)


def _norm_modulate(h, nw, shift, scale):
    ms = jnp.mean(h * h, axis=-1, keepdims=True)
    hn = h * lax.rsqrt(ms + EPS) * nw
    return hn * (1.0 + scale) + shift


def _conv3(v, cw, period):
    rows = v.shape[0]
    r = lax.broadcasted_iota(I32, (rows, 1), 0) % period
    prev = jnp.where(r == 0, 0.0, pltpu.roll(v, 1, 0))
    nxt = jnp.where(r == period - 1, 0.0, pltpu.roll(v, rows - 1, 0))
    return prev * cw[0:1] + v * cw[1:2] + nxt * cw[2:3]


def _inproj_zx_kernel(h_ref, sh_ref, sc_ref, nw_ref, w_ref, wdt_ref, wdtT_ref, cw_ref, cb_ref,
                      *rest, with_z, period, tm, n_z):
    if with_z:
        z_ref, xbc_ref, dt_ref, dtT_ref, a_ref = rest
    else:
        xbc_ref, dt_ref, dtT_ref, a_ref = rest
    n = pl.program_id(1)

    @pl.when(n == 0)
    def _():
        a = _norm_modulate(h_ref[...], nw_ref[...], sh_ref[0], sc_ref[0]).astype(BF16)
        a_ref[...] = a
        dt_ref[...] = jnp.dot(a, wdt_ref[...], preferred_element_type=F32)
        dtT_ref[...] = lax.dot_general(wdtT_ref[...], a, (((1,), (1,)), ((), ())),
                                       preferred_element_type=F32)

    if with_z:
        @pl.when(n < n_z)
        def _():
            for k in range(tm // BLK):
                rows = slice(k * BLK, (k + 1) * BLK)
                z_ref[rows, :] = jnp.dot(a_ref[rows, :], w_ref[...],
                                         preferred_element_type=F32).astype(BF16)

    @pl.when(n >= n_z)
    def _():
        for k in range(tm // BLK):
            rows = slice(k * BLK, (k + 1) * BLK)
            acc = jnp.dot(a_ref[rows, :], w_ref[...], preferred_element_type=F32)
            y = _conv3(acc, cw_ref[...], period) + cb_ref[...]
            xbc_ref[rows, :] = _silu(y).astype(BF16)


def _inproj_zx_call(h2d, shift, scale, norm_w, w_zx, w_dt, w_dtT, conv_w, conv_b, *,
                    with_z, period, tm, rows_per_mod):
    m, d = h2d.shape
    tn = 1024
    n_z = D_SSM // tn if with_z else 0
    n_x = D_XBC // tn
    n_off = 0 if with_z else D_SSM // tn
    per = rows_per_mod // tm
    kern = functools.partial(_inproj_zx_kernel, with_z=with_z, period=period, tm=tm, n_z=n_z)
    out_shape = [jax.ShapeDtypeStruct((m, D_XBC), BF16),
                 jax.ShapeDtypeStruct((m, LANES), F32),
                 jax.ShapeDtypeStruct((LANES, m), F32)]
    out_specs = [pl.BlockSpec((tm, tn), lambda i, n: (i, jnp.maximum(n - n_z, 0))),
                 pl.BlockSpec((tm, LANES), lambda i, n: (i, 0)),
                 pl.BlockSpec((LANES, tm), lambda i, n: (0, i))]
    if with_z:
        out_shape = [jax.ShapeDtypeStruct((m, D_SSM), BF16)] + out_shape
        out_specs = [pl.BlockSpec((tm, tn), lambda i, n: (i, jnp.minimum(n, n_z - 1)))] + out_specs
    return pl.pallas_call(
        kern,
        grid=(m // tm, n_z + n_x),
        in_specs=[pl.BlockSpec((tm, d), lambda i, n: (i, 0)),
                  pl.BlockSpec((1, 1, d), lambda i, n: (i // per, 0, 0)),
                  pl.BlockSpec((1, 1, d), lambda i, n: (i // per, 0, 0)),
                  pl.BlockSpec((1, d), lambda i, n: (0, 0)),
                  pl.BlockSpec((d, tn), lambda i, n: (0, n + n_off)),
                  pl.BlockSpec((d, LANES), lambda i, n: (0, 0)),
                  pl.BlockSpec((LANES, d), lambda i, n: (0, 0)),
                  pl.BlockSpec((3, tn), lambda i, n: (0, jnp.maximum(n - n_z, 0))),
                  pl.BlockSpec((1, tn), lambda i, n: (0, jnp.maximum(n - n_z, 0)))],
        out_specs=out_specs,
        out_shape=out_shape,
        scratch_shapes=[pltpu.VMEM((tm, d), BF16)],
        compiler_params=_cparams(("arbitrary", "arbitrary")),
        name="inproj_zx" if with_z else "inproj_ctx",
    )(h2d, shift, scale, norm_w, w_zx, w_dt, w_dtT, conv_w, conv_b)


def _inproj_sc_kernel(h_ref, sh_ref, sc_ref, nw_ref, w_ref, cw_ref, o_ref, a_ref, *, tm, tc):
    n = pl.program_id(1)

    @pl.when(n == 0)
    def _():
        a_ref[...] = _norm_modulate(h_ref[...], nw_ref[...], sh_ref[0], sc_ref[0]).astype(BF16)

    for k in range(tm // BLK):
        rows = slice(k * BLK, (k + 1) * BLK)
        acc = jnp.dot(a_ref[rows, :], w_ref[...], preferred_element_type=F32)
        bg, cg, hv = acc[:, :tc], acc[:, tc:2 * tc], acc[:, 2 * tc:]
        o_ref[rows, :] = (bg * _conv3(cg * hv, cw_ref[...], GRID_W)).astype(BF16)


def _inproj_sc_call(h2d, shift, scale, norm_w, w_sc, conv_w, *, tm, tc, rows_per_mod):
    m, d = h2d.shape
    per = rows_per_mod // tm
    kern = functools.partial(_inproj_sc_kernel, tm=tm, tc=tc)
    return pl.pallas_call(
        kern,
        grid=(m // tm, D_SC // tc),
        in_specs=[pl.BlockSpec((tm, d), lambda i, n: (i, 0)),
                  pl.BlockSpec((1, 1, d), lambda i, n: (i // per, 0, 0)),
                  pl.BlockSpec((1, 1, d), lambda i, n: (i // per, 0, 0)),
                  pl.BlockSpec((1, d), lambda i, n: (0, 0)),
                  pl.BlockSpec((d, 3 * tc), lambda i, n: (0, n)),
                  pl.BlockSpec((3, tc), lambda i, n: (0, n))],
        out_specs=pl.BlockSpec((tm, tc), lambda i, n: (i, n)),
        out_shape=jax.ShapeDtypeStruct((m, D_SC), BF16),
        scratch_shapes=[pltpu.VMEM((tm, d), BF16)],
        compiler_params=_cparams(("arbitrary", "arbitrary")),
        name="inproj_sc",
    )(h2d, shift, scale, norm_w, w_sc, conv_w)


def _ssd_kernel(*refs, reverse, mode, nblk):
    xbc_ref, dt_ref, dtT_ref, alr_ref, alc_ref, br_ref, bc_ref, e_ref, init_ref = refs[:9]
    rest = refs[9:]
    if mode == "state":
        out_ref, s_ref = rest
    elif mode == "partial":
        out_ref, s_ref = rest
    else:
        z_ref, yo_ref, dsk_ref, nw_ref, out_ref, s_ref = rest
    j = pl.program_id(1)

    @pl.when(j == 0)
    def _():
        s_ref[...] = init_ref[0]

    ii = lax.broadcasted_iota(I32, (SSD_CHUNK, SSD_CHUNK), 0)
    jj = lax.broadcasted_iota(I32, (SSD_CHUNK, SSD_CHUNK), 1)
    tri = jnp.where(jj <= ii, 1.0, 0.0).astype(F32)
    triT = jnp.where(ii <= jj, 1.0, 0.0).astype(F32)
    mask = (jj >= ii) if reverse else (ii >= jj)
    lane4 = lax.broadcasted_iota(I32, (SSD_CHUNK, 4 * SSM_HEAD_DIM), 1) // SSM_HEAD_DIM
    aneg_r = -jnp.exp(alr_ref[...])
    aneg_c = -jnp.exp(alc_ref[...])
    lane0 = SSM_HEADS if reverse else 0
    hi = lax.Precision.HIGHEST

    def chunk(c):
        rows = slice(c * SSD_CHUNK, (c + 1) * SSD_CHUNK)
        dt = _softplus(dt_ref[rows, :] + br_ref[...])
        dtT = _softplus(dtT_ref[:, rows] + bc_ref[...])
        la = dt * aneg_r
        laT = dtT * aneg_c
        cs = jnp.dot(tri, la, precision=hi, preferred_element_type=F32)
        csT = jnp.dot(laT, triT, precision=hi, preferred_element_type=F32)
        total = cs[SSD_CHUNK - 1:SSD_CHUNK, :]
        if reverse:
            ecs = cs - la
            col, rowm = -ecs, csT - laT
            yscale = jnp.exp(total - ecs)
            w = dt * jnp.exp(ecs)
        else:
            col, rowm = cs, -csT
            yscale = jnp.exp(cs)
            w = dt * jnp.exp(total - cs)
        t1 = total.astype(BF16).astype(F32)
        r1 = total - t1
        t2 = r1.astype(BF16).astype(F32)
        t3 = r1 - t2
        r16 = lax.broadcasted_iota(I32, (16, LANES), 0)
        tail = jnp.where(r16 == 0, t1, jnp.where(r16 == 1, t2, jnp.where(r16 == 2, t3, 0.0)))
        stack = jnp.concatenate([yscale, w, tail], axis=0).astype(BF16)
        ex = jnp.dot(stack, e_ref[...], preferred_element_type=F32)
        yscale_x = ex[0:SSD_CHUNK]
        w_x = ex[SSD_CHUNK:2 * SSD_CHUNK]
        tb = 2 * SSD_CHUNK
        sdec_x = jnp.exp(ex[tb:tb + 1] + ex[tb + 1:tb + 2] + ex[tb + 2:tb + 3])

        for g in range(SSM_GROUPS):
            gsl = slice(g * GROUP_W, (g + 1) * GROUP_W)
            xg = xbc_ref[rows, gsl]
            bg = xbc_ref[rows, D_SSM + g * SSM_STATE:D_SSM + (g + 1) * SSM_STATE]
            s_old = s_ref[g]
            if mode != "state":
                cg = xbc_ref[rows, D_SSM + GN + g * SSM_STATE:D_SSM + GN + (g + 1) * SSM_STATE]
                cb = lax.dot_general(cg, bg, (((1,), (1,)), ((), ())), preferred_element_type=F32)
                y_off = jnp.dot(cg, s_old.astype(BF16), preferred_element_type=F32)
                halves = []
                for q in range(2):
                    xq = xg[:, q * 256:(q + 1) * 256]
                    acc = jnp.zeros((SSD_CHUNK, 256), F32)
                    for r4 in range(4):
                        lane = lane0 + g * SSM_HPG + q * 4 + r4
                        seg = col[:, lane:lane + 1] + rowm[lane:lane + 1, :]
                        dec = jnp.exp(jnp.where(mask, seg, -jnp.inf))
                        mh = (cb * dec * dtT[lane:lane + 1, :]).astype(BF16)
                        xm = jnp.where(lane4 == r4, xq, jnp.zeros_like(xq))
                        acc = acc + jnp.dot(mh, xm, preferred_element_type=F32)
                    halves.append(acc)
                y = jnp.concatenate(halves, axis=1) + yscale_x[:, gsl] * y_off
                if mode == "partial":
                    out_ref[rows, gsl] = y.astype(BF16)
                else:
                    y = y + yo_ref[rows, gsl].astype(F32) + dsk_ref[:, gsl] * xg.astype(F32)
                    y = y * _silu(z_ref[rows, gsl].astype(F32))
                    ms = jnp.mean(y * y, axis=-1, keepdims=True)
                    out_ref[rows, gsl] = (y * lax.rsqrt(ms + EPS) * nw_ref[:, gsl]).astype(BF16)
            xw = (xg.astype(F32) * w_x[:, gsl]).astype(BF16)
            upd = lax.dot_general(bg, xw, (((0,), (0,)), ((), ())), preferred_element_type=F32)
            s_ref[g] = sdec_x[:, gsl] * s_old + upd

    for c in ((1, 0) if reverse else (0, 1)):
        chunk(c)

    if mode == "state":
        @pl.when(j == nblk - 1)
        def _():
            out_ref[0] = s_ref[...]


def _ssd_call(xbc, dt, dtT, prm, init, *, batch, nblk, reverse, mode, extra=()):
    m = xbc.shape[0]

    def blk(b, j):
        return b * nblk + (nblk - 1 - j if reverse else j)

    state_spec = pl.BlockSpec((1, SSM_GROUPS, SSM_STATE, GROUP_W), lambda b, j: (b, 0, 0, 0))
    const2 = lambda b, j: (0, 0)
    in_specs = [pl.BlockSpec((BLK, D_XBC), lambda b, j: (blk(b, j), 0)),
                pl.BlockSpec((BLK, LANES), lambda b, j: (blk(b, j), 0)),
                pl.BlockSpec((LANES, BLK), lambda b, j: (0, blk(b, j))),
                pl.BlockSpec((1, LANES), const2), pl.BlockSpec((LANES, 1), const2),
                pl.BlockSpec((1, LANES), const2), pl.BlockSpec((LANES, 1), const2),
                pl.BlockSpec((LANES, D_SSM), const2),
                state_spec]
    args = [xbc, dt, dtT, prm["alog_row"], prm["alog_col"], prm["bias_row"], prm["bias_col"],
            prm["expand_rev" if reverse else "expand_fwd"], init]
    if mode == "state":
        out_specs = state_spec
        out_shape = jax.ShapeDtypeStruct((batch, SSM_GROUPS, SSM_STATE, GROUP_W), F32)
    else:
        out_specs = pl.BlockSpec((BLK, D_SSM), lambda b, j: (blk(b, j), 0))
        out_shape = jax.ShapeDtypeStruct((m, D_SSM), BF16)
    if mode == "final":
        z, y_other = extra
        in_specs += [pl.BlockSpec((BLK, D_SSM), lambda b, j: (blk(b, j), 0)),
                     pl.BlockSpec((BLK, D_SSM), lambda b, j: (blk(b, j), 0)),
                     pl.BlockSpec((1, D_SSM), const2), pl.BlockSpec((1, D_SSM), const2)]
        args += [z, y_other, prm["dskip"], prm["norm_w"]]
    kern = functools.partial(_ssd_kernel, reverse=reverse, mode=mode, nblk=nblk)
    return pl.pallas_call(
        kern,
        grid=(batch, nblk),
        in_specs=in_specs,
        out_specs=out_specs,
        out_shape=out_shape,
        scratch_shapes=[pltpu.VMEM((SSM_GROUPS, SSM_STATE, GROUP_W), F32)],
        compiler_params=_cparams(("arbitrary", "arbitrary")),
        name=f"ssd_{mode}_{'rev' if reverse else 'fwd'}",
    )(*args)


def _outproj_kernel(ys_ref, yc_ref, scw_ref, w_ref, x_ref, g1_ref, nw_ref, sh_ref, sc_ref, wr_ref,
                    h_ref, f_ref, aff_ref):
    yc = yc_ref[...].astype(F32)
    ms = jnp.mean(yc * yc, axis=-1, keepdims=True)
    ycn = (yc * lax.rsqrt(ms + EPS) * scw_ref[...]).astype(BF16)
    mix = jnp.dot(ys_ref[...], w_ref[0:D_SSM, :], preferred_element_type=F32)
    mix = mix + jnp.dot(ycn, w_ref[D_SSM:, :], preferred_element_type=F32)
    h = x_ref[...] + g1_ref[0] * mix
    h_ref[...] = h
    f = _norm_modulate(h, nw_ref[...], sh_ref[0], sc_ref[0])
    f_ref[...] = f
    logits = lax.dot_general(wr_ref[...], f.astype(BF16), (((1,), (1,)), ((), ())),
                             preferred_element_type=F32)
    mx = jnp.max(logits, axis=0, keepdims=True)
    ex = jnp.exp(logits - mx)
    aff_ref[0] = ex / jnp.sum(ex, axis=0, keepdims=True)


def _outproj_call(y_ssm, y_sc, sc_norm_w, w_out, x2d, g1, norm_w, shift, scale, w_rT, *, batch, tm):
    m, d = x2d.shape
    n_tok = m // batch
    per = n_tok // tm
    vec = lambda i: (i // per, 0, 0)
    const2 = lambda i: (0, 0)
    return pl.pallas_call(
        _outproj_kernel,
        grid=(m // tm,),
        in_specs=[pl.BlockSpec((tm, D_SSM), lambda i: (i, 0)),
                  pl.BlockSpec((tm, D_SC), lambda i: (i, 0)),
                  pl.BlockSpec((1, D_SC), const2),
                  pl.BlockSpec((D_SSM + D_SC, d), const2),
                  pl.BlockSpec((tm, d), lambda i: (i, 0)),
                  pl.BlockSpec((1, 1, d), vec),
                  pl.BlockSpec((1, d), const2),
                  pl.BlockSpec((1, 1, d), vec),
                  pl.BlockSpec((1, 1, d), vec),
                  pl.BlockSpec((N_EXPERTS, d), const2)],
        out_specs=[pl.BlockSpec((tm, d), lambda i: (i, 0)),
                   pl.BlockSpec((tm, d), lambda i: (i, 0)),
                   pl.BlockSpec((1, N_EXPERTS, tm), lambda i: (i // per, 0, i % per))],
        out_shape=[jax.ShapeDtypeStruct((m, d), F32),
                   jax.ShapeDtypeStruct((m, d), F32),
                   jax.ShapeDtypeStruct((batch, N_EXPERTS, n_tok), F32)],
        compiler_params=_cparams(("arbitrary",)),
        name="outproj",
    )(y_ssm, y_sc, sc_norm_w, w_out, x2d, g1, norm_w, shift, scale, w_rT)


def _prefix_lanes(m01):
    rows, n = m01.shape
    ki = lax.broadcasted_iota(I32, (LANES, LANES), 0)
    ci = lax.broadcasted_iota(I32, (LANES, LANES), 1)
    triu = jnp.where(ki <= ci, 1.0, 0.0).astype(BF16)
    off = jnp.zeros((rows, 1), F32)
    parts = []
    for k in range(n // LANES):
        p = jnp.dot(m01[:, k * LANES:(k + 1) * LANES].astype(BF16), triu,
                    preferred_element_type=F32)
        parts.append(p + off)
        off = off + p[:, LANES - 1:LANES]
    return jnp.concatenate(parts, axis=1)


def _route_kernel(aff_ref, idx_ref, gcol_ref, *, cap):
    a = aff_ref[0]
    n_e, n = a.shape
    thr = jnp.zeros((n_e, 1), I32)
    for bit in range(30, -1, -1):
        cand = thr | (1 << bit)
        cand_f = lax.bitcast_convert_type(cand, F32)
        cnt = jnp.sum(jnp.where(a >= cand_f, 1.0, 0.0), axis=1, keepdims=True)
        thr = jnp.where(cnt >= cap, cand, thr)
    thr_f = lax.bitcast_convert_type(thr, F32)
    gt = a > thr_f
    eq = a == thr_f
    need = cap - jnp.sum(jnp.where(gt, 1.0, 0.0), axis=1, keepdims=True)
    eqf = jnp.where(eq, 1.0, 0.0)
    ties_before = _prefix_lanes(eqf) - eqf
    sel = jnp.logical_or(gt, jnp.logical_and(eq, ties_before < need))
    self_ = jnp.where(sel, 1.0, 0.0)
    slot = jnp.where(sel, _prefix_lanes(self_) - 1.0, -1.0).astype(I32)

    tok = lax.broadcasted_iota(I32, (1, n), 1)
    t_hi = (tok // 64).astype(F32)
    t_lo = (tok % 64).astype(F32)
    a1 = a.astype(BF16).astype(F32)
    ra = a - a1
    a2 = ra.astype(BF16).astype(F32)
    a3 = ra - a2
    srow = lax.broadcasted_iota(I32, (cap, n), 0)
    r8 = lax.broadcasted_iota(I32, (8, n), 0)
    zpad = jnp.zeros((LANES - 8, n), F32)
    for e in range(n_e):
        onehot = jnp.where(slot[e:e + 1, :] == srow, 1.0, 0.0).astype(BF16)
        v8 = jnp.where(r8 == 0, t_hi, jnp.where(r8 == 1, t_lo, jnp.where(
            r8 == 2, a1[e:e + 1], jnp.where(r8 == 3, a2[e:e + 1], jnp.where(
                r8 == 4, a3[e:e + 1], 0.0)))))
        vals = jnp.concatenate([v8, zpad], axis=0).astype(BF16)
        o_row = lax.dot_general(vals, onehot, (((1,), (1,)), ((), ())),
                                preferred_element_type=F32)
        o_col = lax.dot_general(onehot, vals, (((1,), (1,)), ((), ())),
                                preferred_element_type=F32)
        idx_ref[0, e:e + 1, :] = (o_row[0:1] * 64.0 + o_row[1:2]).astype(I32)
        gcol_ref[0, e] = o_col


def _route_call(aff, *, cap):
    batch, n_e, n = aff.shape
    return pl.pallas_call(
        functools.partial(_route_kernel, cap=cap),
        grid=(batch,),
        in_specs=[pl.BlockSpec((1, n_e, n), lambda b: (b, 0, 0))],
        out_specs=[pl.BlockSpec((1, n_e, cap), lambda b: (b, 0, 0)),
                   pl.BlockSpec((1, n_e, cap, LANES), lambda b: (b, 0, 0, 0))],
        out_shape=[jax.ShapeDtypeStruct((batch, n_e, cap), I32),
                   jax.ShapeDtypeStruct((batch, n_e, cap, LANES), F32)],
        compiler_params=_cparams(("arbitrary",)),
        name="route",
    )(aff)


def _moe_kernel(idx_ref, f_hbm, gcol_ref, g2_ref, wg_ref, wu_ref, wd_ref, hin_hbm, hout_hbm,
                xbuf, hbuf, sem, *, cap, n_tok, sub):
    e = pl.program_id(0)
    b = pl.program_id(1)
    base = (b * N_EXPERTS + e) * cap

    def row_of(jrow):
        return b * n_tok + idx_ref[base + jrow]

    def x_copy(jrow):
        return pltpu.make_async_copy(f_hbm.at[pl.ds(row_of(jrow), 1)], xbuf.at[pl.ds(jrow, 1)],
                                     sem.at[0])

    def h_copy(jrow):
        return pltpu.make_async_copy(hin_hbm.at[pl.ds(row_of(jrow), 1)], hbuf.at[pl.ds(jrow, 1)],
                                     sem.at[1])

    def o_copy(jrow):
        return pltpu.make_async_copy(hbuf.at[pl.ds(jrow, 1)], hout_hbm.at[pl.ds(row_of(jrow), 1)],
                                     sem.at[2])

    @pl.loop(0, cap)
    def _(jrow):
        x_copy(jrow).start()
        h_copy(jrow).start()

    pltpu.make_async_copy(f_hbm.at[pl.ds(0, cap)], xbuf, sem.at[0]).wait()
    pltpu.make_async_copy(hin_hbm.at[pl.ds(0, cap)], hbuf, sem.at[1]).wait()

    for k in range(cap // sub):
        rows = slice(k * sub, (k + 1) * sub)
        xb = xbuf[rows, :].astype(BF16)
        hg = jnp.dot(xb, wg_ref[0], preferred_element_type=F32)
        hu = jnp.dot(xb, wu_ref[0], preferred_element_type=F32)
        hid = (_silu(hg) * hu).astype(BF16)
        ye = jnp.dot(hid, wd_ref[0], preferred_element_type=F32)
        gc = gcol_ref[0, 0, rows, :]
        gate = gc[:, 2:3] + gc[:, 3:4] + gc[:, 4:5]
        hbuf[rows, :] = hbuf[rows, :] + g2_ref[0] * (ye * gate)

    @pl.loop(0, cap)
    def _(jrow):
        o_copy(jrow).start()

    pltpu.make_async_copy(hbuf, hout_hbm.at[pl.ds(0, cap)], sem.at[2]).wait()


def _moe_call(idx_flat, f2d, gcol, g2, wg, wu, wd, h2d, *, batch, cap):
    m, d = h2d.shape
    n_tok = m // batch
    sub = min(cap, 128)
    kern = functools.partial(_moe_kernel, cap=cap, n_tok=n_tok, sub=sub)
    grid_spec = pltpu.PrefetchScalarGridSpec(
        num_scalar_prefetch=1,
        grid=(N_EXPERTS, batch),
        in_specs=[pl.BlockSpec(memory_space=pl.ANY),
                  pl.BlockSpec((1, 1, cap, LANES), lambda e, b, idx: (b, e, 0, 0)),
                  pl.BlockSpec((1, 1, d), lambda e, b, idx: (b, 0, 0)),
                  pl.BlockSpec((1, d, EXPERT_FF), lambda e, b, idx: (e, 0, 0)),
                  pl.BlockSpec((1, d, EXPERT_FF), lambda e, b, idx: (e, 0, 0)),
                  pl.BlockSpec((1, EXPERT_FF, d), lambda e, b, idx: (e, 0, 0)),
                  pl.BlockSpec(memory_space=pl.ANY)],
        out_specs=pl.BlockSpec(memory_space=pl.ANY),
        scratch_shapes=[pltpu.VMEM((cap, d), F32), pltpu.VMEM((cap, d), F32),
                        pltpu.SemaphoreType.DMA((3,))],
    )
    return pl.pallas_call(
        kern,
        grid_spec=grid_spec,
        out_shape=jax.ShapeDtypeStruct((m, d), F32),
        input_output_aliases={7: 0},
        compiler_params=_cparams(("arbitrary", "arbitrary")),
        name="moe",
    )(idx_flat, f2d, gcol, g2, wg, wu, wd, h2d)


def _final_kernel(h_ref, w_ref, o_ref):
    h = h_ref[...]
    ms = jnp.mean(h * h, axis=-1, keepdims=True)
    o_ref[...] = h * lax.rsqrt(ms + EPS) * w_ref[...]


def _final_call(h2d, w, *, tm):
    m, d = h2d.shape
    return pl.pallas_call(
        _final_kernel,
        grid=(m // tm,),
        in_specs=[pl.BlockSpec((tm, d), lambda i: (i, 0)), pl.BlockSpec((1, d), lambda i: (0, 0))],
        out_specs=pl.BlockSpec((tm, d), lambda i: (i, 0)),
        out_shape=jax.ShapeDtypeStruct((m, d), F32),
        compiler_params=_cparams(("arbitrary",)),
        name="final_norm",
    )(h2d, w)


def _pick_tile(n, pref):
    t = pref
    while n % t:
        t //= 2
    return t


def _ssd_params(a_log, dt_bias, d_skip, norm_w):
    padw = LANES - N_DT
    alog = jnp.pad(a_log.astype(F32), (0, padw))
    bias = jnp.pad(dt_bias.astype(F32), (0, padw))
    lane = jnp.arange(LANES)[:, None]
    head = jnp.arange(D_SSM)[None, :] // SSM_HEAD_DIM
    return {
        "alog_row": alog[None, :], "alog_col": alog[:, None],
        "bias_row": bias[None, :], "bias_col": bias[:, None],
        "expand_fwd": (lane == head).astype(BF16),
        "expand_rev": (lane == head + SSM_HEADS).astype(BF16),
        "dskip": jnp.repeat(d_skip.astype(F32), SSM_HEAD_DIM)[None, :],
        "norm_w": norm_w.astype(F32)[None, :],
    }


def kernel(x, c, ctx, c_ctx, w_mod, b_mod, norm_mix_w, w_in, ssm_conv_w, ssm_conv_b, ssm_a_log,
           ssm_dt_bias, ssm_d, ssm_norm_w, sc_conv_w, sc_norm_w, w_out, norm_ffn_w, w_router,
           w_gate, w_up, w_down, final_norm_w):
    batch, n_lat, d = x.shape
    n_ctx = ctx.shape[1]
    depth = w_mod.shape[0]
    assert depth == 1 and d == D_MODEL and n_ctx == BLK and n_lat % BLK == 0
    layer = 0
    cap = EC_CAPACITY_FACTOR * n_lat // N_EXPERTS

    pad_rows = -(batch + 1) % 8
    cvec = jnp.concatenate([c, c_ctx[None, :], jnp.zeros((pad_rows, d), F32)], axis=0)
    mod = _mod_call(cvec, w_mod[layer], b_mod[layer][None, :])
    sh1, sc1, g1, sh2, sc2, g2 = [mod[:batch, k * d:(k + 1) * d][:, None, :] for k in range(N_MOD)]
    sh1_c = mod[batch:batch + 1, 0:d][:, None, :]
    sc1_c = mod[batch:batch + 1, d:2 * d][:, None, :]

    w_in_l = w_in[layer]
    w_zx = w_in_l[:, :D_ZX].astype(BF16)
    w_dt = jnp.pad(w_in_l[:, D_ZX:D_SSM_IN], ((0, 0), (0, LANES - N_DT))).astype(BF16)
    w_dtT = w_dt.T
    tc = 512
    w_sc3 = w_in_l[:, D_SSM_IN:].reshape(d, 3, D_SC // tc, tc)
    w_sc = jnp.transpose(w_sc3, (0, 2, 1, 3)).reshape(d, 3 * D_SC).astype(BF16)
    w_out_b = w_out[layer].astype(BF16)
    w_rT = w_router[layer].T.astype(BF16)
    wg, wu, wd = w_gate[layer].astype(BF16), w_up[layer].astype(BF16), w_down[layer].astype(BF16)
    norm_mix = norm_mix_w[layer][None, :]
    conv_w, conv_b = ssm_conv_w[layer], ssm_conv_b[layer][None, :]
    prm = _ssd_params(ssm_a_log[layer], ssm_dt_bias[layer], ssm_d[layer], ssm_norm_w[layer])

    ctx2d = ctx.reshape(batch * n_ctx, d)
    xbc_c, dt_c, dtT_c = _inproj_zx_call(
        ctx2d, sh1_c, sc1_c, norm_mix, w_zx, w_dt, w_dtT, conv_w, conv_b,
        with_z=False, period=n_ctx, tm=_pick_tile(batch * n_ctx, 1024), rows_per_mod=batch * n_ctx)
    zero_state = jnp.zeros((batch, SSM_GROUPS, SSM_STATE, GROUP_W), F32)
    st_f = _ssd_call(xbc_c, dt_c, dtT_c, prm, zero_state, batch=batch, nblk=1, reverse=False,
                     mode="state")
    st_r = _ssd_call(xbc_c, dt_c, dtT_c, prm, zero_state, batch=batch, nblk=1, reverse=True,
                     mode="state")

    x2d = x.reshape(batch * n_lat, d)
    tm = _pick_tile(n_lat, 1024)
    z, xbc, dt, dtT = _inproj_zx_call(
        x2d, sh1, sc1, norm_mix, w_zx, w_dt, w_dtT, conv_w, conv_b,
        with_z=True, period=GRID_W, tm=tm, rows_per_mod=n_lat)
    y_sc = _inproj_sc_call(x2d, sh1, sc1, norm_mix, w_sc, sc_conv_w[layer], tm=tm, tc=tc,
                           rows_per_mod=n_lat)
    nblk = n_lat // BLK
    y_rev = _ssd_call(xbc, dt, dtT, prm, st_r, batch=batch, nblk=nblk, reverse=True, mode="partial")
    y_ssm = _ssd_call(xbc, dt, dtT, prm, st_f, batch=batch, nblk=nblk, reverse=False, mode="final",
                      extra=(z, y_rev))

    h1, f_lat, aff = _outproj_call(
        y_ssm, y_sc, sc_norm_w[layer][None, :], w_out_b, x2d, g1, norm_ffn_w[layer][None, :],
        sh2, sc2, w_rT, batch=batch, tm=_pick_tile(n_lat, 256))
    idx, gcol = _route_call(aff, cap=cap)
    h2 = _moe_call(idx.reshape(-1), f_lat, gcol, g2, wg, wu, wd, h1, batch=batch, cap=cap)
    out = _final_call(h2, final_norm_w[None, :], tm=_pick_tile(batch * n_lat, 512))
    return out.reshape(batch, n_lat, d)
```

```python
import functools

import jax
import jax.numpy as jnp
from jax import lax
from jax.experimental import pallas as pl
from jax.experimental.pallas import tpu as pltpu

F32 = jnp.float32
BF16 = jnp.bfloat16
I32 = jnp.int32

D_MODEL = 2048
GRID_W = 64
D_SSM = 2048
SSM_HEAD_DIM = 64
SSM_HEADS = D_SSM // SSM_HEAD_DIM
SSM_GROUPS = 4
SSM_HPG = SSM_HEADS // SSM_GROUPS
SSM_STATE = 128
SSD_CHUNK = 128
GN = SSM_GROUPS * SSM_STATE
D_XBC = D_SSM + 2 * GN
D_ZX = D_SSM + D_XBC
N_DT = 2 * SSM_HEADS
D_SSM_IN = D_ZX + N_DT
D_SC = 2048
N_EXPERTS = 16
EC_CAPACITY_FACTOR = 2
EXPERT_FF = 1024
N_MOD = 6
EPS = 1e-6

LANES = 128
BLK = 2 * SSD_CHUNK
GROUP_W = SSM_HPG * SSM_HEAD_DIM
VMEM_LIMIT = 56 * 1024 * 1024


def _cparams(sem):
    return pltpu.CompilerParams(dimension_semantics=sem, vmem_limit_bytes=VMEM_LIMIT)


def _silu(v):
    return v * jax.nn.sigmoid(v)


def _softplus(v):
    return jnp.maximum(v, 0.0) + jnp.log1p(jnp.exp(-jnp.abs(v)))


def _mod_kernel(c_ref, w_ref, b_ref, o_ref):
    s = _silu(c_ref[...]).astype(BF16)
    o_ref[...] = jnp.dot(s, w_ref[...].astype(BF16), preferred_element_type=F32) + b_ref[...]


def _mod_call(cvec, w_mod, b_mod):
    rows, d = cvec.shape
    n = w_mod.shape[1]
    tn = 1024
    return pl.pallas_call(
        _mod_kernel,
        grid=(n // tn,),
        in_specs=[pl.BlockSpec((rows, d), lambda j: (0, 0)),
                  pl.BlockSpec((d, tn), lambda j: (0, j)),
                  pl.BlockSpec((1, tn), lambda j: (0, j))],
        out_specs=pl.BlockSpec((rows, tn), lambda j: (0, j)),
        out_shape=jax.ShapeDtypeStruct((rows, n), F32),
        compiler_params=_cparams(("arbitrary",)),
        name="mod",
    )(cvec, w_mod, b_mod)


def _norm_modulate(h, nw, shift, scale):
    ms = jnp.mean(h * h, axis=-1, keepdims=True)
    hn = h * lax.rsqrt(ms + EPS) * nw
    return hn * (1.0 + scale) + shift


def _conv3(v, cw, period):
    rows = v.shape[0]
    r = lax.broadcasted_iota(I32, (rows, 1), 0) % period
    prev = jnp.where(r == 0, 0.0, pltpu.roll(v, 1, 0))
    nxt = jnp.where(r == period - 1, 0.0, pltpu.roll(v, rows - 1, 0))
    return prev * cw[0:1] + v * cw[1:2] + nxt * cw[2:3]


def _inproj_zx_kernel(h_ref, sh_ref, sc_ref, nw_ref, w_ref, wdt_ref, wdtT_ref, cw_ref, cb_ref,
                      *rest, with_z, period, tm, n_z):
    if with_z:
        z_ref, xbc_ref, dt_ref, dtT_ref, a_ref = rest
    else:
        xbc_ref, dt_ref, dtT_ref, a_ref = rest
    n = pl.program_id(1)

    @pl.when(n == 0)
    def _():
        a = _norm_modulate(h_ref[...], nw_ref[...], sh_ref[0], sc_ref[0]).astype(BF16)
        a_ref[...] = a
        dt_ref[...] = jnp.dot(a, wdt_ref[...], preferred_element_type=F32)
        dtT_ref[...] = lax.dot_general(wdtT_ref[...], a, (((1,), (1,)), ((), ())),
                                       preferred_element_type=F32)

    if with_z:
        @pl.when(n < n_z)
        def _():
            for k in range(tm // BLK):
                rows = slice(k * BLK, (k + 1) * BLK)
                z_ref[rows, :] = jnp.dot(a_ref[rows, :], w_ref[...],
                                         preferred_element_type=F32).astype(BF16)

    @pl.when(n >= n_z)
    def _():
        for k in range(tm // BLK):
            rows = slice(k * BLK, (k + 1) * BLK)
            acc = jnp.dot(a_ref[rows, :], w_ref[...], preferred_element_type=F32)
            y = _conv3(acc, cw_ref[...], period) + cb_ref[...]
            xbc_ref[rows, :] = _silu(y).astype(BF16)


def _inproj_zx_call(h2d, shift, scale, norm_w, w_zx, w_dt, w_dtT, conv_w, conv_b, *,
                    with_z, period, tm, rows_per_mod):
    m, d = h2d.shape
    tn = 1024
    n_z = D_SSM // tn if with_z else 0
    n_x = D_XBC // tn
    n_off = 0 if with_z else D_SSM // tn
    per = rows_per_mod // tm
    kern = functools.partial(_inproj_zx_kernel, with_z=with_z, period=period, tm=tm, n_z=n_z)
    out_shape = [jax.ShapeDtypeStruct((m, D_XBC), BF16),
                 jax.ShapeDtypeStruct((m, LANES), F32),
                 jax.ShapeDtypeStruct((LANES, m), F32)]
    out_specs = [pl.BlockSpec((tm, tn), lambda i, n: (i, jnp.maximum(n - n_z, 0))),
                 pl.BlockSpec((tm, LANES), lambda i, n: (i, 0)),
                 pl.BlockSpec((LANES, tm), lambda i, n: (0, i))]
    if with_z:
        out_shape = [jax.ShapeDtypeStruct((m, D_SSM), BF16)] + out_shape
        out_specs = [pl.BlockSpec((tm, tn), lambda i, n: (i, jnp.minimum(n, n_z - 1)))] + out_specs
    return pl.pallas_call(
        kern,
        grid=(m // tm, n_z + n_x),
        in_specs=[pl.BlockSpec((tm, d), lambda i, n: (i, 0)),
                  pl.BlockSpec((1, 1, d), lambda i, n: (i // per, 0, 0)),
                  pl.BlockSpec((1, 1, d), lambda i, n: (i // per, 0, 0)),
                  pl.BlockSpec((1, d), lambda i, n: (0, 0)),
                  pl.BlockSpec((d, tn), lambda i, n: (0, n + n_off)),
                  pl.BlockSpec((d, LANES), lambda i, n: (0, 0)),
                  pl.BlockSpec((LANES, d), lambda i, n: (0, 0)),
                  pl.BlockSpec((3, tn), lambda i, n: (0, jnp.maximum(n - n_z, 0))),
                  pl.BlockSpec((1, tn), lambda i, n: (0, jnp.maximum(n - n_z, 0)))],
        out_specs=out_specs,
        out_shape=out_shape,
        scratch_shapes=[pltpu.VMEM((tm, d), BF16)],
        compiler_params=_cparams(("arbitrary", "arbitrary")),
        name="inproj_zx" if with_z else "inproj_ctx",
    )(h2d, shift, scale, norm_w, w_zx, w_dt, w_dtT, conv_w, conv_b)


def _inproj_sc_kernel(h_ref, sh_ref, sc_ref, nw_ref, wb_ref, wc_ref, wv_ref, cw_ref, o_ref, a_ref,
                      *, tm):
    n = pl.program_id(1)

    @pl.when(n == 0)
    def _():
        a_ref[...] = _norm_modulate(h_ref[...], nw_ref[...], sh_ref[0], sc_ref[0]).astype(BF16)

    for k in range(tm // BLK):
        rows = slice(k * BLK, (k + 1) * BLK)
        a = a_ref[rows, :]
        cg = jnp.dot(a, wc_ref[...], preferred_element_type=F32)
        hv = jnp.dot(a, wv_ref[...], preferred_element_type=F32)
        v = _conv3(cg * hv, cw_ref[...], GRID_W)
        bg = jnp.dot(a, wb_ref[...], preferred_element_type=F32)
        o_ref[rows, :] = (bg * v).astype(BF16)


def _inproj_sc_call(h2d, shift, scale, norm_w, w_b, w_c, w_v, conv_w, *, tm, tc, rows_per_mod):
    m, d = h2d.shape
    per = rows_per_mod // tm
    kern = functools.partial(_inproj_sc_kernel, tm=tm)
    wspec = pl.BlockSpec((d, tc), lambda i, n: (0, n))
    return pl.pallas_call(
        kern,
        grid=(m // tm, D_SC // tc),
        in_specs=[pl.BlockSpec((tm, d), lambda i, n: (i, 0)),
                  pl.BlockSpec((1, 1, d), lambda i, n: (i // per, 0, 0)),
                  pl.BlockSpec((1, 1, d), lambda i, n: (i // per, 0, 0)),
                  pl.BlockSpec((1, d), lambda i, n: (0, 0)),
                  wspec, wspec, wspec,
                  pl.BlockSpec((3, tc), lambda i, n: (0, n))],
        out_specs=pl.BlockSpec((tm, tc), lambda i, n: (i, n)),
        out_shape=jax.ShapeDtypeStruct((m, D_SC), BF16),
        scratch_shapes=[pltpu.VMEM((tm, d), BF16)],
        compiler_params=_cparams(("arbitrary", "arbitrary")),
        name="inproj_sc",
    )(h2d, shift, scale, norm_w, w_b, w_c, w_v, conv_w)


def _ssd_kernel(*refs, reverse, mode, nblk):
    xbc_ref, dt_ref, dtT_ref, alr_ref, alc_ref, br_ref, bc_ref, e_ref, init_ref = refs[:9]
    rest = refs[9:]
    if mode == "state":
        out_ref, s_ref = rest
    elif mode == "partial":
        out_ref, s_ref = rest
    else:
        z_ref, yo_ref, dsk_ref, nw_ref, out_ref, s_ref = rest
    j = pl.program_id(1)

    @pl.when(j == 0)
    def _():
        s_ref[...] = init_ref[0]

    ii = lax.broadcasted_iota(I32, (SSD_CHUNK, SSD_CHUNK), 0)
    jj = lax.broadcasted_iota(I32, (SSD_CHUNK, SSD_CHUNK), 1)
    tri = jnp.where(jj <= ii, 1.0, 0.0).astype(F32)
    triT = jnp.where(ii <= jj, 1.0, 0.0).astype(F32)
    mask = (jj >= ii) if reverse else (ii >= jj)
    lane4 = lax.broadcasted_iota(I32, (SSD_CHUNK, 4 * SSM_HEAD_DIM), 1) // SSM_HEAD_DIM
    aneg_r = -jnp.exp(alr_ref[...])
    aneg_c = -jnp.exp(alc_ref[...])
    lane0 = SSM_HEADS if reverse else 0
    hi = lax.Precision.HIGHEST

    def chunk(c):
        rows = slice(c * SSD_CHUNK, (c + 1) * SSD_CHUNK)
        dt = _softplus(dt_ref[rows, :] + br_ref[...])
        dtT = _softplus(dtT_ref[:, rows] + bc_ref[...])
        la = dt * aneg_r
        laT = dtT * aneg_c
        cs = jnp.dot(tri, la, precision=hi, preferred_element_type=F32)
        csT = jnp.dot(laT, triT, precision=hi, preferred_element_type=F32)
        total = cs[SSD_CHUNK - 1:SSD_CHUNK, :]
        if reverse:
            ecs = cs - la
            col, rowm = -ecs, csT - laT
            yscale = jnp.exp(total - ecs)
            w = dt * jnp.exp(ecs)
        else:
            col, rowm = cs, -csT
            yscale = jnp.exp(cs)
            w = dt * jnp.exp(total - cs)
        t1 = total.astype(BF16).astype(F32)
        r1 = total - t1
        t2 = r1.astype(BF16).astype(F32)
        t3 = r1 - t2
        r16 = lax.broadcasted_iota(I32, (16, LANES), 0)
        tail = jnp.where(r16 == 0, t1, jnp.where(r16 == 1, t2, jnp.where(r16 == 2, t3, 0.0)))
        stack = jnp.concatenate([yscale, w, tail], axis=0).astype(BF16)
        ex = jnp.dot(stack, e_ref[...], preferred_element_type=F32)
        yscale_x = ex[0:SSD_CHUNK]
        w_x = ex[SSD_CHUNK:2 * SSD_CHUNK]
        tb = 2 * SSD_CHUNK
        sdec_x = jnp.exp(ex[tb:tb + 1] + ex[tb + 1:tb + 2] + ex[tb + 2:tb + 3])

        for g in range(SSM_GROUPS):
            gsl = slice(g * GROUP_W, (g + 1) * GROUP_W)
            xg = xbc_ref[rows, gsl]
            bg = xbc_ref[rows, D_SSM + g * SSM_STATE:D_SSM + (g + 1) * SSM_STATE]
            s_old = s_ref[g]
            if mode != "state":
                cg = xbc_ref[rows, D_SSM + GN + g * SSM_STATE:D_SSM + GN + (g + 1) * SSM_STATE]
                cb = lax.dot_general(cg, bg, (((1,), (1,)), ((), ())), preferred_element_type=F32)
                y_off = jnp.dot(cg, s_old.astype(BF16), preferred_element_type=F32)
                halves = []
                for q in range(2):
                    xq = xg[:, q * 256:(q + 1) * 256]
                    acc = jnp.zeros((SSD_CHUNK, 256), F32)
                    for r4 in range(4):
                        lane = lane0 + g * SSM_HPG + q * 4 + r4
                        seg = col[:, lane:lane + 1] + rowm[lane:lane + 1, :]
                        dec = jnp.exp(jnp.where(mask, seg, -jnp.inf))
                        mh = (cb * dec * dtT[lane:lane + 1, :]).astype(BF16)
                        xm = jnp.where(lane4 == r4, xq, jnp.zeros_like(xq))
                        acc = acc + jnp.dot(mh, xm, preferred_element_type=F32)
                    halves.append(acc)
                y = jnp.concatenate(halves, axis=1) + yscale_x[:, gsl] * y_off
                if mode == "partial":
                    out_ref[rows, gsl] = y.astype(BF16)
                else:
                    y = y + yo_ref[rows, gsl].astype(F32) + dsk_ref[:, gsl] * xg.astype(F32)
                    y = y * _silu(z_ref[rows, gsl].astype(F32))
                    ms = jnp.mean(y * y, axis=-1, keepdims=True)
                    out_ref[rows, gsl] = (y * lax.rsqrt(ms + EPS) * nw_ref[:, gsl]).astype(BF16)
            xw = (xg.astype(F32) * w_x[:, gsl]).astype(BF16)
            upd = lax.dot_general(bg, xw, (((0,), (0,)), ((), ())), preferred_element_type=F32)
            s_ref[g] = sdec_x[:, gsl] * s_old + upd

    for c in ((1, 0) if reverse else (0, 1)):
        chunk(c)

    if mode == "state":
        @pl.when(j == nblk - 1)
        def _():
            out_ref[0] = s_ref[...]


def _ssd_call(xbc, dt, dtT, prm, init, *, batch, nblk, reverse, mode, extra=()):
    m = xbc.shape[0]

    def blk(b, j):
        return b * nblk + (nblk - 1 - j if reverse else j)

    state_spec = pl.BlockSpec((1, SSM_GROUPS, SSM_STATE, GROUP_W), lambda b, j: (b, 0, 0, 0))
    const2 = lambda b, j: (0, 0)
    in_specs = [pl.BlockSpec((BLK, D_XBC), lambda b, j: (blk(b, j), 0)),
                pl.BlockSpec((BLK, LANES), lambda b, j: (blk(b, j), 0)),
                pl.BlockSpec((LANES, BLK), lambda b, j: (0, blk(b, j))),
                pl.BlockSpec((1, LANES), const2), pl.BlockSpec((LANES, 1), const2),
                pl.BlockSpec((1, LANES), const2), pl.BlockSpec((LANES, 1), const2),
                pl.BlockSpec((LANES, D_SSM), const2),
                state_spec]
    args = [xbc, dt, dtT, prm["alog_row"], prm["alog_col"], prm["bias_row"], prm["bias_col"],
            prm["expand_rev" if reverse else "expand_fwd"], init]
    if mode == "state":
        out_specs = state_spec
        out_shape = jax.ShapeDtypeStruct((batch, SSM_GROUPS, SSM_STATE, GROUP_W), F32)
    else:
        out_specs = pl.BlockSpec((BLK, D_SSM), lambda b, j: (blk(b, j), 0))
        out_shape = jax.ShapeDtypeStruct((m, D_SSM), BF16)
    if mode == "final":
        z, y_other = extra
        in_specs += [pl.BlockSpec((BLK, D_SSM), lambda b, j: (blk(b, j), 0)),
                     pl.BlockSpec((BLK, D_SSM), lambda b, j: (blk(b, j), 0)),
                     pl.BlockSpec((1, D_SSM), const2), pl.BlockSpec((1, D_SSM), const2)]
        args += [z, y_other, prm["dskip"], prm["norm_w"]]
    kern = functools.partial(_ssd_kernel, reverse=reverse, mode=mode, nblk=nblk)
    return pl.pallas_call(
        kern,
        grid=(batch, nblk),
        in_specs=in_specs,
        out_specs=out_specs,
        out_shape=out_shape,
        scratch_shapes=[pltpu.VMEM((SSM_GROUPS, SSM_STATE, GROUP_W), F32)],
        compiler_params=_cparams(("arbitrary", "arbitrary")),
        name=f"ssd_{mode}_{'rev' if reverse else 'fwd'}",
    )(*args)


def _outproj_kernel(ys_ref, yc_ref, scw_ref, w_ref, x_ref, g1_ref, nw_ref, sh_ref, sc_ref, wr_ref,
                    h_ref, f_ref, aff_ref):
    yc = yc_ref[...].astype(F32)
    ms = jnp.mean(yc * yc, axis=-1, keepdims=True)
    ycn = (yc * lax.rsqrt(ms + EPS) * scw_ref[...]).astype(BF16)
    mix = jnp.dot(ys_ref[...], w_ref[0:D_SSM, :], preferred_element_type=F32)
    mix = mix + jnp.dot(ycn, w_ref[D_SSM:, :], preferred_element_type=F32)
    h = x_ref[...] + g1_ref[0] * mix
    h_ref[...] = h
    f = _norm_modulate(h, nw_ref[...], sh_ref[0], sc_ref[0])
    f_ref[...] = f
    logits = lax.dot_general(wr_ref[...], f.astype(BF16), (((1,), (1,)), ((), ())),
                             preferred_element_type=F32)
    mx = jnp.max(logits, axis=0, keepdims=True)
    ex = jnp.exp(logits - mx)
    aff_ref[0] = ex / jnp.sum(ex, axis=0, keepdims=True)


def _outproj_call(y_ssm, y_sc, sc_norm_w, w_out, x2d, g1, norm_w, shift, scale, w_rT, *, batch, tm):
    m, d = x2d.shape
    n_tok = m // batch
    per = n_tok // tm
    vec = lambda i: (i // per, 0, 0)
    const2 = lambda i: (0, 0)
    return pl.pallas_call(
        _outproj_kernel,
        grid=(m // tm,),
        in_specs=[pl.BlockSpec((tm, D_SSM), lambda i: (i, 0)),
                  pl.BlockSpec((tm, D_SC), lambda i: (i, 0)),
                  pl.BlockSpec((1, D_SC), const2),
                  pl.BlockSpec((D_SSM + D_SC, d), const2),
                  pl.BlockSpec((tm, d), lambda i: (i, 0)),
                  pl.BlockSpec((1, 1, d), vec),
                  pl.BlockSpec((1, d), const2),
                  pl.BlockSpec((1, 1, d), vec),
                  pl.BlockSpec((1, 1, d), vec),
                  pl.BlockSpec((N_EXPERTS, d), const2)],
        out_specs=[pl.BlockSpec((tm, d), lambda i: (i, 0)),
                   pl.BlockSpec((tm, d), lambda i: (i, 0)),
                   pl.BlockSpec((1, N_EXPERTS, tm), lambda i: (i // per, 0, i % per))],
        out_shape=[jax.ShapeDtypeStruct((m, d), F32),
                   jax.ShapeDtypeStruct((m, d), F32),
                   jax.ShapeDtypeStruct((batch, N_EXPERTS, n_tok), F32)],
        compiler_params=_cparams(("arbitrary",)),
        name="outproj",
    )(y_ssm, y_sc, sc_norm_w, w_out, x2d, g1, norm_w, shift, scale, w_rT)


def _prefix_lanes(m01):
    rows, n = m01.shape
    ki = lax.broadcasted_iota(I32, (LANES, LANES), 0)
    ci = lax.broadcasted_iota(I32, (LANES, LANES), 1)
    triu = jnp.where(ki <= ci, 1.0, 0.0).astype(BF16)
    off = jnp.zeros((rows, 1), F32)
    parts = []
    for k in range(n // LANES):
        p = jnp.dot(m01[:, k * LANES:(k + 1) * LANES].astype(BF16), triu,
                    preferred_element_type=F32)
        parts.append(p + off)
        off = off + p[:, LANES - 1:LANES]
    return jnp.concatenate(parts, axis=1)


def _route_kernel(aff_ref, idx_ref, gcol_ref, *, cap):
    a = aff_ref[0]
    n_e, n = a.shape
    thr = jnp.zeros((n_e, 1), I32)
    for bit in range(30, -1, -1):
        cand = thr | (1 << bit)
        cand_f = lax.bitcast_convert_type(cand, F32)
        cnt = jnp.sum(jnp.where(a >= cand_f, 1.0, 0.0), axis=1, keepdims=True)
        thr = jnp.where(cnt >= cap, cand, thr)
    thr_f = lax.bitcast_convert_type(thr, F32)
    gt = a > thr_f
    eq = a == thr_f
    need = cap - jnp.sum(jnp.where(gt, 1.0, 0.0), axis=1, keepdims=True)
    eqf = jnp.where(eq, 1.0, 0.0)
    ties_before = _prefix_lanes(eqf) - eqf
    sel = jnp.logical_or(gt, jnp.logical_and(eq, ties_before < need))
    self_ = jnp.where(sel, 1.0, 0.0)
    slot = jnp.where(sel, _prefix_lanes(self_) - 1.0, -1.0).astype(I32)

    tok = lax.broadcasted_iota(I32, (1, n), 1)
    t_hi = (tok // 64).astype(F32)
    t_lo = (tok % 64).astype(F32)
    a1 = a.astype(BF16).astype(F32)
    ra = a - a1
    a2 = ra.astype(BF16).astype(F32)
    a3 = ra - a2
    srow = lax.broadcasted_iota(I32, (cap, n), 0)
    r8 = lax.broadcasted_iota(I32, (8, n), 0)
    zpad = jnp.zeros((LANES - 8, n), F32)
    for e in range(n_e):
        onehot = jnp.where(slot[e:e + 1, :] == srow, 1.0, 0.0).astype(BF16)
        v8 = jnp.where(r8 == 0, t_hi, jnp.where(r8 == 1, t_lo, jnp.where(
            r8 == 2, a1[e:e + 1], jnp.where(r8 == 3, a2[e:e + 1], jnp.where(
                r8 == 4, a3[e:e + 1], 0.0)))))
        vals = jnp.concatenate([v8, zpad], axis=0).astype(BF16)
        o_row = lax.dot_general(vals, onehot, (((1,), (1,)), ((), ())),
                                preferred_element_type=F32)
        o_col = lax.dot_general(onehot, vals, (((1,), (1,)), ((), ())),
                                preferred_element_type=F32)
        idx_ref[0, e:e + 1, :] = (o_row[0:1] * 64.0 + o_row[1:2]).astype(I32)
        gcol_ref[0, e] = o_col


def _route_call(aff, *, cap):
    batch, n_e, n = aff.shape
    return pl.pallas_call(
        functools.partial(_route_kernel, cap=cap),
        grid=(batch,),
        in_specs=[pl.BlockSpec((1, n_e, n), lambda b: (b, 0, 0))],
        out_specs=[pl.BlockSpec((1, n_e, cap), lambda b: (b, 0, 0)),
                   pl.BlockSpec((1, n_e, cap, LANES), lambda b: (b, 0, 0, 0))],
        out_shape=[jax.ShapeDtypeStruct((batch, n_e, cap), I32),
                   jax.ShapeDtypeStruct((batch, n_e, cap, LANES), F32)],
        compiler_params=_cparams(("arbitrary",)),
        name="route",
    )(aff)


MOE_PARTS = 4
MOE_NCHUNK = 512


def _moe_kernel(idx_ref, f_hbm, gcol_ref, g2_ref, wg_ref, wu_ref, wd_ref, hin_hbm, hout_hbm,
                xb0, xb1, hb0, hb1, sem, *, cap, n_tok, n_pairs):
    e = pl.program_id(0)
    i = pl.program_id(1)
    first = jnp.logical_and(e == 0, i == 0)
    last = jnp.logical_and(e == pl.num_programs(0) - 1, i == n_pairs - 1)
    i_next = jnp.where(i == n_pairs - 1, 0, i + 1)
    e_next = jnp.where(i == n_pairs - 1, e + 1, e)
    xbufs, hbufs = (xb0, xb1), (hb0, hb1)
    part = cap // MOE_PARTS

    def row_of(ex, b, jrow):
        return b * n_tok + idx_ref[(b * N_EXPERTS + ex) * cap + jrow]

    def gather_start(ex, b, s, jrow):
        row = row_of(ex, b, jrow)
        pltpu.make_async_copy(f_hbm.at[pl.ds(row, 1)], xbufs[s].at[pl.ds(jrow, 1)],
                              sem.at[s]).start()
        pltpu.make_async_copy(hin_hbm.at[pl.ds(row, 1)], hbufs[s].at[pl.ds(jrow, 1)],
                              sem.at[2 + s]).start()

    def scatter_start(ex, b, s, jrow):
        pltpu.make_async_copy(hbufs[s].at[pl.ds(jrow, 1)],
                              hout_hbm.at[pl.ds(row_of(ex, b, jrow), 1)], sem.at[4 + s]).start()

    def wait_gather(s):
        pltpu.make_async_copy(f_hbm.at[pl.ds(0, cap)], xbufs[s], sem.at[s]).wait()
        pltpu.make_async_copy(hin_hbm.at[pl.ds(0, cap)], hbufs[s], sem.at[2 + s]).wait()

    def wait_scatter(s):
        pltpu.make_async_copy(hbufs[s], hout_hbm.at[pl.ds(0, cap)], sem.at[4 + s]).wait()

    def compute_part(s, b_local, q, between):
        rows = slice(q * part, (q + 1) * part)
        xq = xbufs[s][rows, :].astype(BF16)
        gc = gcol_ref[b_local, 0, rows, :]
        gate = gc[:, 2:3] + gc[:, 3:4] + gc[:, 4:5]
        g2 = g2_ref[2 * i + b_local]
        nc = MOE_NCHUNK
        hid = []
        for c in range(EXPERT_FF // nc):
            cols = slice(c * nc, (c + 1) * nc)
            between[2 * c]()
            hg = jnp.dot(xq, wg_ref[0, :, cols], preferred_element_type=F32)
            between[2 * c + 1]()
            hu = jnp.dot(xq, wu_ref[0, :, cols], preferred_element_type=F32)
            hid.append((_silu(hg) * hu).astype(BF16))
        hid = jnp.concatenate(hid, axis=1)
        k0 = 2 * (EXPERT_FF // nc)
        for c in range(D_MODEL // nc):
            cols = slice(c * nc, (c + 1) * nc)
            between[k0 + c]()
            ye = jnp.dot(hid, wd_ref[0, :, cols], preferred_element_type=F32)
            hbufs[s][rows, cols] = hbufs[s][rows, cols] + g2[:, cols] * (ye * gate)

    n_between = 2 * (EXPERT_FF // MOE_NCHUNK) + D_MODEL // MOE_NCHUNK

    def spread(jobs):
        n = len(jobs)
        cuts = [n * k // n_between for k in range(n_between + 1)]

        def group(k):
            def run():
                for job in jobs[cuts[k]:cuts[k + 1]]:
                    job()
            return run
        return [group(k) for k in range(n_between)]

    b_a, b_b = 2 * i, 2 * i + 1
    b_n = 2 * i_next

    @pl.when(first)
    def _():
        @pl.loop(0, cap)
        def _(jrow):
            gather_start(e, b_a, 0, jrow)

    wait_gather(0)
    for q in range(MOE_PARTS):
        jobs = [functools.partial(gather_start, e, b_b, 1, jrow)
                for jrow in range(q * part, (q + 1) * part)]
        if q > 0:
            jobs += [functools.partial(scatter_start, e, b_a, 0, jrow)
                     for jrow in range((q - 1) * part, q * part)]
        compute_part(0, 0, q, spread(jobs))

    wait_gather(1)
    rows_next = [cap * k // (MOE_PARTS - 1) for k in range(MOE_PARTS)]
    for q in range(MOE_PARTS):
        if q == 0:
            jobs = [functools.partial(scatter_start, e, b_a, 0, jrow)
                    for jrow in range((MOE_PARTS - 1) * part, cap)]
        else:
            jobs = [functools.partial(scatter_start, e, b_b, 1, jrow)
                    for jrow in range((q - 1) * part, q * part)]
            jobs += [functools.partial(gather_start, e_next, b_n, 0, jrow)
                     for jrow in range(rows_next[q - 1], rows_next[q])]
        if q == 1:
            wait_scatter(0)
        compute_part(1, 1, q, spread(jobs))

    for jrow in range((MOE_PARTS - 1) * part, cap):
        scatter_start(e, b_b, 1, jrow)
    wait_scatter(1)

    @pl.when(last)
    def _():
        wait_gather(0)


def _moe_call(idx_flat, f2d, gcol, g2, wg, wu, wd, h2d, *, batch, cap):
    m, d = h2d.shape
    n_tok = m // batch
    assert batch % 2 == 0 and cap % MOE_PARTS == 0
    n_pairs = batch // 2
    kern = functools.partial(_moe_kernel, cap=cap, n_tok=n_tok, n_pairs=n_pairs)
    grid_spec = pltpu.PrefetchScalarGridSpec(
        num_scalar_prefetch=1,
        grid=(N_EXPERTS, n_pairs),
        in_specs=[pl.BlockSpec(memory_space=pl.ANY),
                  pl.BlockSpec((2, 1, cap, LANES), lambda e, i, idx: (i, e, 0, 0)),
                  pl.BlockSpec((batch, 1, d), lambda e, i, idx: (0, 0, 0)),
                  pl.BlockSpec((1, d, EXPERT_FF), lambda e, i, idx: (e, 0, 0)),
                  pl.BlockSpec((1, d, EXPERT_FF), lambda e, i, idx: (e, 0, 0)),
                  pl.BlockSpec((1, EXPERT_FF, d), lambda e, i, idx: (e, 0, 0)),
                  pl.BlockSpec(memory_space=pl.ANY)],
        out_specs=pl.BlockSpec(memory_space=pl.ANY),
        scratch_shapes=[pltpu.VMEM((cap, d), F32), pltpu.VMEM((cap, d), F32),
                        pltpu.VMEM((cap, d), F32), pltpu.VMEM((cap, d), F32),
                        pltpu.SemaphoreType.DMA((6,))],
    )
    return pl.pallas_call(
        kern,
        grid_spec=grid_spec,
        out_shape=jax.ShapeDtypeStruct((m, d), F32),
        input_output_aliases={7: 0},
        compiler_params=_cparams(("arbitrary", "arbitrary")),
        name="moe",
    )(idx_flat, f2d, gcol, g2, wg, wu, wd, h2d)


def _final_kernel(h_ref, w_ref, o_ref):
    h = h_ref[...]
    ms = jnp.mean(h * h, axis=-1, keepdims=True)
    o_ref[...] = h * lax.rsqrt(ms + EPS) * w_ref[...]


def _final_call(h2d, w, *, tm):
    m, d = h2d.shape
    return pl.pallas_call(
        _final_kernel,
        grid=(m // tm,),
        in_specs=[pl.BlockSpec((tm, d), lambda i: (i, 0)), pl.BlockSpec((1, d), lambda i: (0, 0))],
        out_specs=pl.BlockSpec((tm, d), lambda i: (i, 0)),
        out_shape=jax.ShapeDtypeStruct((m, d), F32),
        compiler_params=_cparams(("arbitrary",)),
        name="final_norm",
    )(h2d, w)


def _pick_tile(n, pref):
    t = pref
    while n % t:
        t //= 2
    return t


def _ssd_params(a_log, dt_bias, d_skip, norm_w):
    padw = LANES - N_DT
    alog = jnp.pad(a_log.astype(F32), (0, padw))
    bias = jnp.pad(dt_bias.astype(F32), (0, padw))
    lane = jnp.arange(LANES)[:, None]
    head = jnp.arange(D_SSM)[None, :] // SSM_HEAD_DIM
    return {
        "alog_row": alog[None, :], "alog_col": alog[:, None],
        "bias_row": bias[None, :], "bias_col": bias[:, None],
        "expand_fwd": (lane == head).astype(BF16),
        "expand_rev": (lane == head + SSM_HEADS).astype(BF16),
        "dskip": jnp.repeat(d_skip.astype(F32), SSM_HEAD_DIM)[None, :],
        "norm_w": norm_w.astype(F32)[None, :],
    }


def kernel(x, c, ctx, c_ctx, w_mod, b_mod, norm_mix_w, w_in, ssm_conv_w, ssm_conv_b, ssm_a_log,
           ssm_dt_bias, ssm_d, ssm_norm_w, sc_conv_w, sc_norm_w, w_out, norm_ffn_w, w_router,
           w_gate, w_up, w_down, final_norm_w):
    batch, n_lat, d = x.shape
    n_ctx = ctx.shape[1]
    depth = w_mod.shape[0]
    assert depth == 1 and d == D_MODEL and n_ctx == BLK and n_lat % BLK == 0
    layer = 0
    cap = EC_CAPACITY_FACTOR * n_lat // N_EXPERTS

    pad_rows = -(batch + 1) % 8
    cvec = jnp.concatenate([c, c_ctx[None, :], jnp.zeros((pad_rows, d), F32)], axis=0)
    mod = _mod_call(cvec, w_mod[layer], b_mod[layer][None, :])
    sh1, sc1, g1, sh2, sc2, g2 = [mod[:batch, k * d:(k + 1) * d][:, None, :] for k in range(N_MOD)]
    sh1_c = mod[batch:batch + 1, 0:d][:, None, :]
    sc1_c = mod[batch:batch + 1, d:2 * d][:, None, :]

    w_in_l = w_in[layer]
    w_zx = w_in_l[:, :D_ZX].astype(BF16)
    w_dt = jnp.pad(w_in_l[:, D_ZX:D_SSM_IN], ((0, 0), (0, LANES - N_DT))).astype(BF16)
    w_dtT = w_dt.T
    tc = 512
    w_b, w_c, w_v = [w_in_l[:, D_SSM_IN + k * D_SC:D_SSM_IN + (k + 1) * D_SC].astype(BF16)
                     for k in range(3)]
    w_out_b = w_out[layer].astype(BF16)
    w_rT = w_router[layer].T.astype(BF16)
    wg, wu, wd = w_gate[layer].astype(BF16), w_up[layer].astype(BF16), w_down[layer].astype(BF16)
    norm_mix = norm_mix_w[layer][None, :]
    conv_w, conv_b = ssm_conv_w[layer], ssm_conv_b[layer][None, :]
    prm = _ssd_params(ssm_a_log[layer], ssm_dt_bias[layer], ssm_d[layer], ssm_norm_w[layer])

    ctx2d = ctx.reshape(batch * n_ctx, d)
    xbc_c, dt_c, dtT_c = _inproj_zx_call(
        ctx2d, sh1_c, sc1_c, norm_mix, w_zx, w_dt, w_dtT, conv_w, conv_b,
        with_z=False, period=n_ctx, tm=_pick_tile(batch * n_ctx, 1024), rows_per_mod=batch * n_ctx)
    zero_state = jnp.zeros((batch, SSM_GROUPS, SSM_STATE, GROUP_W), F32)
    st_f = _ssd_call(xbc_c, dt_c, dtT_c, prm, zero_state, batch=batch, nblk=1, reverse=False,
                     mode="state")
    st_r = _ssd_call(xbc_c, dt_c, dtT_c, prm, zero_state, batch=batch, nblk=1, reverse=True,
                     mode="state")

    x2d = x.reshape(batch * n_lat, d)
    tm = _pick_tile(n_lat, 1024)
    z, xbc, dt, dtT = _inproj_zx_call(
        x2d, sh1, sc1, norm_mix, w_zx, w_dt, w_dtT, conv_w, conv_b,
        with_z=True, period=GRID_W, tm=tm, rows_per_mod=n_lat)
    y_sc = _inproj_sc_call(x2d, sh1, sc1, norm_mix, w_b, w_c, w_v, sc_conv_w[layer], tm=tm, tc=tc,
                           rows_per_mod=n_lat)
    nblk = n_lat // BLK
    y_rev = _ssd_call(xbc, dt, dtT, prm, st_r, batch=batch, nblk=nblk, reverse=True, mode="partial")
    y_ssm = _ssd_call(xbc, dt, dtT, prm, st_f, batch=batch, nblk=nblk, reverse=False, mode="final",
                      extra=(z, y_rev))

    h1, f_lat, aff = _outproj_call(
        y_ssm, y_sc, sc_norm_w[layer][None, :], w_out_b, x2d, g1, norm_ffn_w[layer][None, :],
        sh2, sc2, w_rT, batch=batch, tm=_pick_tile(n_lat, 256))
    idx, gcol = _route_call(aff, cap=cap)
    h2 = _moe_call(idx.reshape(-1), f_lat, gcol, g2, wg, wu, wd, h1, batch=batch, cap=cap)
    out = _final_call(h2, final_norm_w[None, :], tm=_pick_tile(batch * n_lat, 512))
    return out.reshape(batch, n_lat, d)
```

```python
import functools

import jax
import jax.numpy as jnp
from jax import lax
from jax.experimental import pallas as pl
from jax.experimental.pallas import tpu as pltpu

F32 = jnp.float32
BF16 = jnp.bfloat16
I32 = jnp.int32

D_MODEL = 2048
GRID_W = 64
D_SSM = 2048
SSM_HEAD_DIM = 64
SSM_HEADS = D_SSM // SSM_HEAD_DIM
SSM_GROUPS = 4
SSM_HPG = SSM_HEADS // SSM_GROUPS
SSM_STATE = 128
SSD_CHUNK = 128
GN = SSM_GROUPS * SSM_STATE
D_XBC = D_SSM + 2 * GN
D_ZX = D_SSM + D_XBC
N_DT = 2 * SSM_HEADS
D_SSM_IN = D_ZX + N_DT
D_SC = 2048
N_EXPERTS = 16
EC_CAPACITY_FACTOR = 2
EXPERT_FF = 1024
N_MOD = 6
EPS = 1e-6
LOG2E = 1.4426950408889634

LANES = 128
BLK = 2 * SSD_CHUNK
GROUP_W = SSM_HPG * SSM_HEAD_DIM
VMEM_LIMIT = 56 * 1024 * 1024


def _cparams(sem):
    return pltpu.CompilerParams(dimension_semantics=sem, vmem_limit_bytes=VMEM_LIMIT)


def _silu(v):
    return v * jax.nn.sigmoid(v)


def _softplus(v):
    return jnp.maximum(v, 0.0) + jnp.log1p(jnp.exp(-jnp.abs(v)))


def _mod_kernel(c_ref, w_ref, b_ref, o_ref):
    s = _silu(c_ref[...]).astype(BF16)
    o_ref[...] = jnp.dot(s, w_ref[...].astype(BF16), preferred_element_type=F32) + b_ref[...]


def _mod_call(cvec, w_mod, b_mod):
    rows, d = cvec.shape
    n = w_mod.shape[1]
    tn = 1024
    return pl.pallas_call(
        _mod_kernel,
        grid=(n // tn,),
        in_specs=[pl.BlockSpec((rows, d), lambda j: (0, 0)),
                  pl.BlockSpec((d, tn), lambda j: (0, j)),
                  pl.BlockSpec((1, tn), lambda j: (0, j))],
        out_specs=pl.BlockSpec((rows, tn), lambda j: (0, j)),
        out_shape=jax.ShapeDtypeStruct((rows, n), F32),
        compiler_params=_cparams(("arbitrary",)),
        name="mod",
    )(cvec, w_mod, b_mod)


def _norm_modulate(h, nw, shift, scale):
    ms = jnp.mean(h * h, axis=-1, keepdims=True)
    hn = h * lax.rsqrt(ms + EPS) * nw
    return hn * (1.0 + scale) + shift


def _conv3(v, cw, period):
    rows = v.shape[0]
    r = lax.broadcasted_iota(I32, (rows, 1), 0) % period
    prev = jnp.where(r == 0, 0.0, pltpu.roll(v, 1, 0))
    nxt = jnp.where(r == period - 1, 0.0, pltpu.roll(v, rows - 1, 0))
    return prev * cw[0:1] + v * cw[1:2] + nxt * cw[2:3]


def _inproj_zx_kernel(h_ref, sh_ref, sc_ref, nw_ref, w_ref, wdt_ref, cw_ref, cb_ref,
                      *rest, with_z, period, tm, n_z):
    if with_z:
        z_ref, xbc_ref, dt_ref, dtT_ref, a_ref = rest
    else:
        xbc_ref, dt_ref, dtT_ref, a_ref = rest
    n = pl.program_id(1)

    @pl.when(n == 0)
    def _():
        a = _norm_modulate(h_ref[...], nw_ref[...], sh_ref[0], sc_ref[0]).astype(BF16)
        a_ref[...] = a
        dt = jnp.dot(a, wdt_ref[...].astype(BF16), preferred_element_type=F32)
        dt_ref[...] = dt
        dtT_ref[...] = dt.T

    wb = w_ref[...].astype(BF16)

    if with_z:
        @pl.when(n < n_z)
        def _():
            for k in range(tm // BLK):
                rows = slice(k * BLK, (k + 1) * BLK)
                z_ref[rows, :] = jnp.dot(a_ref[rows, :], wb,
                                         preferred_element_type=F32).astype(BF16)

    @pl.when(n >= n_z)
    def _():
        for k in range(tm // BLK):
            rows = slice(k * BLK, (k + 1) * BLK)
            acc = jnp.dot(a_ref[rows, :], wb, preferred_element_type=F32)
            y = _conv3(acc, cw_ref[...], period) + cb_ref[...]
            xbc_ref[rows, :] = _silu(y).astype(BF16)


def _inproj_zx_call(h2d, shift, scale, norm_w, w_in, conv_w, conv_b, *,
                    with_z, period, tm, rows_per_mod):
    m, d = h2d.shape
    tn = 1024
    n_z = D_SSM // tn if with_z else 0
    n_x = D_XBC // tn
    n_off = 0 if with_z else D_SSM // tn
    per = rows_per_mod // tm
    kern = functools.partial(_inproj_zx_kernel, with_z=with_z, period=period, tm=tm, n_z=n_z)
    out_shape = [jax.ShapeDtypeStruct((m, D_XBC), BF16),
                 jax.ShapeDtypeStruct((m, LANES), F32),
                 jax.ShapeDtypeStruct((LANES, m), F32)]
    out_specs = [pl.BlockSpec((tm, tn), lambda i, n: (i, jnp.maximum(n - n_z, 0))),
                 pl.BlockSpec((tm, LANES), lambda i, n: (i, 0)),
                 pl.BlockSpec((LANES, tm), lambda i, n: (0, i))]
    if with_z:
        out_shape = [jax.ShapeDtypeStruct((m, D_SSM), BF16)] + out_shape
        out_specs = [pl.BlockSpec((tm, tn), lambda i, n: (i, jnp.minimum(n, n_z - 1)))] + out_specs
    return pl.pallas_call(
        kern,
        grid=(m // tm, n_z + n_x),
        in_specs=[pl.BlockSpec((tm, d), lambda i, n: (i, 0)),
                  pl.BlockSpec((1, 1, d), lambda i, n: (i // per, 0, 0)),
                  pl.BlockSpec((1, 1, d), lambda i, n: (i // per, 0, 0)),
                  pl.BlockSpec((1, d), lambda i, n: (0, 0)),
                  pl.BlockSpec((d, tn), lambda i, n: (0, n + n_off)),
                  pl.BlockSpec((d, LANES), lambda i, n: (0, D_ZX // LANES)),
                  pl.BlockSpec((3, tn), lambda i, n: (0, jnp.maximum(n - n_z, 0))),
                  pl.BlockSpec((1, tn), lambda i, n: (0, jnp.maximum(n - n_z, 0)))],
        out_specs=out_specs,
        out_shape=out_shape,
        scratch_shapes=[pltpu.VMEM((tm, d), BF16)],
        compiler_params=_cparams(("arbitrary", "arbitrary")),
        name="inproj_zx" if with_z else "inproj_ctx",
    )(h2d, shift, scale, norm_w, w_in, w_in, conv_w, conv_b)


def _inproj_sc_kernel(h_ref, sh_ref, sc_ref, nw_ref, wb_ref, wc_ref, wv_ref, cw_ref, o_ref, a_ref,
                      *, tm):
    n = pl.program_id(1)

    @pl.when(n == 0)
    def _():
        a_ref[...] = _norm_modulate(h_ref[...], nw_ref[...], sh_ref[0], sc_ref[0]).astype(BF16)

    for k in range(tm // BLK):
        rows = slice(k * BLK, (k + 1) * BLK)
        a = a_ref[rows, :]
        cg = jnp.dot(a, wc_ref[...], preferred_element_type=F32)
        hv = jnp.dot(a, wv_ref[...], preferred_element_type=F32)
        v = _conv3(cg * hv, cw_ref[...], GRID_W)
        bg = jnp.dot(a, wb_ref[...], preferred_element_type=F32)
        o_ref[rows, :] = (bg * v).astype(BF16)


def _inproj_sc_call(h2d, shift, scale, norm_w, w_b, w_c, w_v, conv_w, *, tm, tc, rows_per_mod):
    m, d = h2d.shape
    per = rows_per_mod // tm
    kern = functools.partial(_inproj_sc_kernel, tm=tm)
    wspec = pl.BlockSpec((d, tc), lambda i, n: (0, n))
    return pl.pallas_call(
        kern,
        grid=(m // tm, D_SC // tc),
        in_specs=[pl.BlockSpec((tm, d), lambda i, n: (i, 0)),
                  pl.BlockSpec((1, 1, d), lambda i, n: (i // per, 0, 0)),
                  pl.BlockSpec((1, 1, d), lambda i, n: (i // per, 0, 0)),
                  pl.BlockSpec((1, d), lambda i, n: (0, 0)),
                  wspec, wspec, wspec,
                  pl.BlockSpec((3, tc), lambda i, n: (0, n))],
        out_specs=pl.BlockSpec((tm, tc), lambda i, n: (i, n)),
        out_shape=jax.ShapeDtypeStruct((m, D_SC), BF16),
        scratch_shapes=[pltpu.VMEM((tm, d), BF16)],
        compiler_params=_cparams(("arbitrary", "arbitrary")),
        name="inproj_sc",
    )(h2d, shift, scale, norm_w, w_b, w_c, w_v, conv_w)


def _ssd_kernel(*refs, reverse, mode, nblk):
    xbc_ref, dt_ref, dtT_ref, alr_ref, alc_ref, br_ref, bc_ref, e_ref, init_ref = refs[:9]
    rest = refs[9:]
    if mode == "state":
        out_ref, s_ref = rest
    elif mode == "partial":
        out_ref, s_ref = rest
    else:
        z_ref, yo_ref, dsk_ref, nw_ref, out_ref, s_ref = rest
    j = pl.program_id(1)

    @pl.when(j == 0)
    def _():
        s_ref[...] = init_ref[0]

    ii = lax.broadcasted_iota(I32, (SSD_CHUNK, SSD_CHUNK), 0)
    jj = lax.broadcasted_iota(I32, (SSD_CHUNK, SSD_CHUNK), 1)
    tri = jnp.where(jj <= ii, 1.0, 0.0).astype(F32)
    triT = jnp.where(ii <= jj, 1.0, 0.0).astype(F32)
    mask = (jj >= ii) if reverse else (ii >= jj)
    lane4 = lax.broadcasted_iota(I32, (SSD_CHUNK, 4 * SSM_HEAD_DIM), 1) // SSM_HEAD_DIM
    aneg_r = -jnp.exp(alr_ref[...]) * LOG2E
    aneg_c = -jnp.exp(alc_ref[...]) * LOG2E
    lane0 = SSM_HEADS if reverse else 0
    hi = lax.Precision.HIGHEST

    def chunk(c):
        rows = slice(c * SSD_CHUNK, (c + 1) * SSD_CHUNK)
        dt = _softplus(dt_ref[rows, :] + br_ref[...])
        dtT = _softplus(dtT_ref[:, rows] + bc_ref[...])
        la = dt * aneg_r
        laT = dtT * aneg_c
        cs = jnp.dot(tri, la, precision=hi, preferred_element_type=F32)
        csT = jnp.dot(laT, triT, precision=hi, preferred_element_type=F32)
        total = cs[SSD_CHUNK - 1:SSD_CHUNK, :]
        log2_dtT = jnp.log(dtT) * LOG2E
        if reverse:
            ecs = cs - la
            col, rowm = -ecs, csT - laT + log2_dtT
            yscale = jnp.exp2(total - ecs)
            w = dt * jnp.exp2(ecs)
        else:
            col, rowm = cs, log2_dtT - csT
            yscale = jnp.exp2(cs)
            w = dt * jnp.exp2(total - cs)
        t1 = total.astype(BF16).astype(F32)
        r1 = total - t1
        t2 = r1.astype(BF16).astype(F32)
        t3 = r1 - t2
        r16 = lax.broadcasted_iota(I32, (16, LANES), 0)
        tail = jnp.where(r16 == 0, t1, jnp.where(r16 == 1, t2, jnp.where(r16 == 2, t3, 0.0)))
        stack = jnp.concatenate([yscale, w, tail], axis=0).astype(BF16)
        ex = jnp.dot(stack, e_ref[...], preferred_element_type=F32)
        yscale_x = ex[0:SSD_CHUNK]
        w_x = ex[SSD_CHUNK:2 * SSD_CHUNK]
        tb = 2 * SSD_CHUNK
        sdec_x = jnp.exp2(ex[tb:tb + 1] + ex[tb + 1:tb + 2] + ex[tb + 2:tb + 3])

        for g in range(SSM_GROUPS):
            gsl = slice(g * GROUP_W, (g + 1) * GROUP_W)
            xg = xbc_ref[rows, gsl]
            bg = xbc_ref[rows, D_SSM + g * SSM_STATE:D_SSM + (g + 1) * SSM_STATE]
            s_old = s_ref[g]
            if mode != "state":
                cg = xbc_ref[rows, D_SSM + GN + g * SSM_STATE:D_SSM + GN + (g + 1) * SSM_STATE]
                cb = lax.dot_general(cg, bg, (((1,), (1,)), ((), ())), preferred_element_type=F32)
                y_off = jnp.dot(cg, s_old.astype(BF16), preferred_element_type=F32)
                halves = []
                for q in range(2):
                    xq = xg[:, q * 256:(q + 1) * 256]
                    acc = jnp.zeros((SSD_CHUNK, 256), F32)
                    for r4 in range(4):
                        lane = lane0 + g * SSM_HPG + q * 4 + r4
                        seg = col[:, lane:lane + 1] + rowm[lane:lane + 1, :]
                        dec = jnp.exp2(jnp.where(mask, seg, -jnp.inf))
                        mh = (cb * dec).astype(BF16)
                        xm = jnp.where(lane4 == r4, xq, jnp.zeros_like(xq))
                        acc = acc + jnp.dot(mh, xm, preferred_element_type=F32)
                    halves.append(acc)
                y = jnp.concatenate(halves, axis=1) + yscale_x[:, gsl] * y_off
                if mode == "partial":
                    out_ref[rows, gsl] = y.astype(BF16)
                else:
                    y = y + yo_ref[rows, gsl].astype(F32) + dsk_ref[:, gsl] * xg.astype(F32)
                    y = y * _silu(z_ref[rows, gsl].astype(F32))
                    ms = jnp.mean(y * y, axis=-1, keepdims=True)
                    out_ref[rows, gsl] = (y * lax.rsqrt(ms + EPS) * nw_ref[:, gsl]).astype(BF16)
            xw = (xg.astype(F32) * w_x[:, gsl]).astype(BF16)
            upd = lax.dot_general(bg, xw, (((0,), (0,)), ((), ())), preferred_element_type=F32)
            s_ref[g] = sdec_x[:, gsl] * s_old + upd

    for c in ((1, 0) if reverse else (0, 1)):
        chunk(c)

    if mode == "state":
        @pl.when(j == nblk - 1)
        def _():
            out_ref[0] = s_ref[...]


def _ssd_call(xbc, dt, dtT, prm, init, *, batch, nblk, reverse, mode, extra=()):
    m = xbc.shape[0]

    def blk(b, j):
        return b * nblk + (nblk - 1 - j if reverse else j)

    state_spec = pl.BlockSpec((1, SSM_GROUPS, SSM_STATE, GROUP_W), lambda b, j: (b, 0, 0, 0))
    const2 = lambda b, j: (0, 0)
    in_specs = [pl.BlockSpec((BLK, D_XBC), lambda b, j: (blk(b, j), 0)),
                pl.BlockSpec((BLK, LANES), lambda b, j: (blk(b, j), 0)),
                pl.BlockSpec((LANES, BLK), lambda b, j: (0, blk(b, j))),
                pl.BlockSpec((1, LANES), const2), pl.BlockSpec((LANES, 1), const2),
                pl.BlockSpec((1, LANES), const2), pl.BlockSpec((LANES, 1), const2),
                pl.BlockSpec((LANES, D_SSM), const2),
                state_spec]
    args = [xbc, dt, dtT, prm["alog_row"], prm["alog_col"], prm["bias_row"], prm["bias_col"],
            prm["expand_rev" if reverse else "expand_fwd"], init]
    if mode == "state":
        out_specs = state_spec
        out_shape = jax.ShapeDtypeStruct((batch, SSM_GROUPS, SSM_STATE, GROUP_W), F32)
    else:
        out_specs = pl.BlockSpec((BLK, D_SSM), lambda b, j: (blk(b, j), 0))
        out_shape = jax.ShapeDtypeStruct((m, D_SSM), BF16)
    if mode == "final":
        z, y_other = extra
        in_specs += [pl.BlockSpec((BLK, D_SSM), lambda b, j: (blk(b, j), 0)),
                     pl.BlockSpec((BLK, D_SSM), lambda b, j: (blk(b, j), 0)),
                     pl.BlockSpec((1, D_SSM), const2), pl.BlockSpec((1, D_SSM), const2)]
        args += [z, y_other, prm["dskip"], prm["norm_w"]]
    kern = functools.partial(_ssd_kernel, reverse=reverse, mode=mode, nblk=nblk)
    return pl.pallas_call(
        kern,
        grid=(batch, nblk),
        in_specs=in_specs,
        out_specs=out_specs,
        out_shape=out_shape,
        scratch_shapes=[pltpu.VMEM((SSM_GROUPS, SSM_STATE, GROUP_W), F32)],
        compiler_params=_cparams(("arbitrary", "arbitrary")),
        name=f"ssd_{mode}_{'rev' if reverse else 'fwd'}",
    )(*args)


def _outproj_kernel(ys_ref, yc_ref, scw_ref, w_ref, x_ref, g1_ref, nw_ref, sh_ref, sc_ref, wr_ref,
                    h_ref, f_ref, aff_ref):
    yc = yc_ref[...].astype(F32)
    ms = jnp.mean(yc * yc, axis=-1, keepdims=True)
    ycn = (yc * lax.rsqrt(ms + EPS) * scw_ref[...]).astype(BF16)
    mix = jnp.dot(ys_ref[...], w_ref[0:D_SSM, :], preferred_element_type=F32)
    mix = mix + jnp.dot(ycn, w_ref[D_SSM:, :], preferred_element_type=F32)
    h = x_ref[...] + g1_ref[0] * mix
    h_ref[...] = h
    f = _norm_modulate(h, nw_ref[...], sh_ref[0], sc_ref[0])
    f_ref[...] = f
    logits = lax.dot_general(wr_ref[...], f.astype(BF16), (((1,), (1,)), ((), ())),
                             preferred_element_type=F32)
    mx = jnp.max(logits, axis=0, keepdims=True)
    ex = jnp.exp(logits - mx)
    aff_ref[0] = ex / jnp.sum(ex, axis=0, keepdims=True)


def _outproj_call(y_ssm, y_sc, sc_norm_w, w_out, x2d, g1, norm_w, shift, scale, w_rT, *, batch, tm):
    m, d = x2d.shape
    n_tok = m // batch
    per = n_tok // tm
    vec = lambda i: (i // per, 0, 0)
    const2 = lambda i: (0, 0)
    return pl.pallas_call(
        _outproj_kernel,
        grid=(m // tm,),
        in_specs=[pl.BlockSpec((tm, D_SSM), lambda i: (i, 0)),
                  pl.BlockSpec((tm, D_SC), lambda i: (i, 0)),
                  pl.BlockSpec((1, D_SC), const2),
                  pl.BlockSpec((D_SSM + D_SC, d), const2),
                  pl.BlockSpec((tm, d), lambda i: (i, 0)),
                  pl.BlockSpec((1, 1, d), vec),
                  pl.BlockSpec((1, d), const2),
                  pl.BlockSpec((1, 1, d), vec),
                  pl.BlockSpec((1, 1, d), vec),
                  pl.BlockSpec((N_EXPERTS, d), const2)],
        out_specs=[pl.BlockSpec((tm, d), lambda i: (i, 0)),
                   pl.BlockSpec((tm, d), lambda i: (i, 0)),
                   pl.BlockSpec((1, N_EXPERTS, tm), lambda i: (i // per, 0, i % per))],
        out_shape=[jax.ShapeDtypeStruct((m, d), F32),
                   jax.ShapeDtypeStruct((m, d), F32),
                   jax.ShapeDtypeStruct((batch, N_EXPERTS, n_tok), F32)],
        compiler_params=_cparams(("arbitrary",)),
        name="outproj",
    )(y_ssm, y_sc, sc_norm_w, w_out, x2d, g1, norm_w, shift, scale, w_rT)


def _prefix_lanes(m01):
    rows, n = m01.shape
    ki = lax.broadcasted_iota(I32, (LANES, LANES), 0)
    ci = lax.broadcasted_iota(I32, (LANES, LANES), 1)
    triu = jnp.where(ki <= ci, 1.0, 0.0).astype(BF16)
    off = jnp.zeros((rows, 1), F32)
    parts = []
    for k in range(n // LANES):
        p = jnp.dot(m01[:, k * LANES:(k + 1) * LANES].astype(BF16), triu,
                    preferred_element_type=F32)
        parts.append(p + off)
        off = off + p[:, LANES - 1:LANES]
    return jnp.concatenate(parts, axis=1)


def _route_kernel(aff_ref, idx_ref, gcol_ref, *, cap):
    a = aff_ref[0]
    n_e, n = a.shape
    thr = jnp.zeros((n_e, 1), I32)
    for bit in range(30, -1, -1):
        cand = thr | (1 << bit)
        cand_f = lax.bitcast_convert_type(cand, F32)
        cnt = jnp.sum(jnp.where(a >= cand_f, 1.0, 0.0), axis=1, keepdims=True)
        thr = jnp.where(cnt >= cap, cand, thr)
    thr_f = lax.bitcast_convert_type(thr, F32)
    gt = a > thr_f
    eq = a == thr_f
    need = cap - jnp.sum(jnp.where(gt, 1.0, 0.0), axis=1, keepdims=True)
    eqf = jnp.where(eq, 1.0, 0.0)
    ties_before = _prefix_lanes(eqf) - eqf
    sel = jnp.logical_or(gt, jnp.logical_and(eq, ties_before < need))
    self_ = jnp.where(sel, 1.0, 0.0)
    slot = jnp.where(sel, _prefix_lanes(self_) - 1.0, -1.0).astype(I32)

    tok = lax.broadcasted_iota(I32, (1, n), 1)
    t_hi = (tok // 64).astype(F32)
    t_lo = (tok % 64).astype(F32)
    a1 = a.astype(BF16).astype(F32)
    ra = a - a1
    a2 = ra.astype(BF16).astype(F32)
    a3 = ra - a2
    srow = lax.broadcasted_iota(I32, (cap, n), 0)
    r8 = lax.broadcasted_iota(I32, (8, n), 0)
    zpad = jnp.zeros((LANES - 8, n), F32)
    for e in range(n_e):
        onehot = jnp.where(slot[e:e + 1, :] == srow, 1.0, 0.0).astype(BF16)
        v8 = jnp.where(r8 == 0, t_hi, jnp.where(r8 == 1, t_lo, jnp.where(
            r8 == 2, a1[e:e + 1], jnp.where(r8 == 3, a2[e:e + 1], jnp.where(
                r8 == 4, a3[e:e + 1], 0.0)))))
        vals = jnp.concatenate([v8, zpad], axis=0).astype(BF16)
        o_col = lax.dot_general(onehot, vals, (((1,), (1,)), ((), ())),
                                preferred_element_type=F32)
        o_row = o_col.T
        idx_ref[0, e:e + 1, :] = (o_row[0:1] * 64.0 + o_row[1:2]).astype(I32)
        gcol_ref[0, e] = o_col


def _route_call(aff, *, cap):
    batch, n_e, n = aff.shape
    return pl.pallas_call(
        functools.partial(_route_kernel, cap=cap),
        grid=(batch,),
        in_specs=[pl.BlockSpec((1, n_e, n), lambda b: (b, 0, 0))],
        out_specs=[pl.BlockSpec((1, n_e, cap), lambda b: (b, 0, 0)),
                   pl.BlockSpec((1, n_e, cap, LANES), lambda b: (b, 0, 0, 0))],
        out_shape=[jax.ShapeDtypeStruct((batch, n_e, cap), I32),
                   jax.ShapeDtypeStruct((batch, n_e, cap, LANES), F32)],
        compiler_params=_cparams(("arbitrary",)),
        name="route",
    )(aff)


MOE_PARTS = 4
MOE_NCHUNK = 512


def _moe_kernel(idx_ref, f_hbm, gcol_ref, g2_ref, wg_ref, wu_ref, wd_ref, hin_hbm, hout_hbm,
                xb0, xb1, hb0, hb1, sem, *, cap, n_tok, n_pairs):
    e = pl.program_id(0)
    i = pl.program_id(1)
    first = jnp.logical_and(e == 0, i == 0)
    last = jnp.logical_and(e == pl.num_programs(0) - 1, i == n_pairs - 1)
    i_next = jnp.where(i == n_pairs - 1, 0, i + 1)
    e_next = jnp.where(i == n_pairs - 1, e + 1, e)
    xbufs, hbufs = (xb0, xb1), (hb0, hb1)
    part = cap // MOE_PARTS

    def row_of(ex, b, jrow):
        return b * n_tok + idx_ref[(b * N_EXPERTS + ex) * cap + jrow]

    def gather_start(ex, b, s, jrow):
        row = row_of(ex, b, jrow)
        pltpu.make_async_copy(f_hbm.at[pl.ds(row, 1)], xbufs[s].at[pl.ds(jrow, 1)],
                              sem.at[s]).start()
        pltpu.make_async_copy(hin_hbm.at[pl.ds(row, 1)], hbufs[s].at[pl.ds(jrow, 1)],
                              sem.at[2 + s]).start()

    def scatter_start(ex, b, s, jrow):
        pltpu.make_async_copy(hbufs[s].at[pl.ds(jrow, 1)],
                              hout_hbm.at[pl.ds(row_of(ex, b, jrow), 1)], sem.at[4 + s]).start()

    def wait_gather(s):
        pltpu.make_async_copy(f_hbm.at[pl.ds(0, cap)], xbufs[s], sem.at[s]).wait()
        pltpu.make_async_copy(hin_hbm.at[pl.ds(0, cap)], hbufs[s], sem.at[2 + s]).wait()

    def wait_scatter(s):
        pltpu.make_async_copy(hbufs[s], hout_hbm.at[pl.ds(0, cap)], sem.at[4 + s]).wait()

    def compute_part(s, b_local, q, between):
        rows = slice(q * part, (q + 1) * part)
        xq = xbufs[s][rows, :].astype(BF16)
        gc = gcol_ref[b_local, 0, rows, :]
        gate = gc[:, 2:3] + gc[:, 3:4] + gc[:, 4:5]
        g2 = g2_ref[2 * i + b_local]
        nc = MOE_NCHUNK
        hid = []
        for c in range(EXPERT_FF // nc):
            cols = slice(c * nc, (c + 1) * nc)
            between[2 * c]()
            hg = jnp.dot(xq, wg_ref[0, :, cols], preferred_element_type=F32)
            between[2 * c + 1]()
            hu = jnp.dot(xq, wu_ref[0, :, cols], preferred_element_type=F32)
            hid.append((_silu(hg) * hu).astype(BF16))
        hid = jnp.concatenate(hid, axis=1)
        k0 = 2 * (EXPERT_FF // nc)
        for c in range(D_MODEL // nc):
            cols = slice(c * nc, (c + 1) * nc)
            between[k0 + c]()
            ye = jnp.dot(hid, wd_ref[0, :, cols], preferred_element_type=F32)
            hbufs[s][rows, cols] = hbufs[s][rows, cols] + g2[:, cols] * (ye * gate)

    n_between = 2 * (EXPERT_FF // MOE_NCHUNK) + D_MODEL // MOE_NCHUNK

    def spread(jobs):
        n = len(jobs)
        cuts = [n * k // n_between for k in range(n_between + 1)]

        def group(k):
            def run():
                for job in jobs[cuts[k]:cuts[k + 1]]:
                    job()
            return run
        return [group(k) for k in range(n_between)]

    b_a, b_b = 2 * i, 2 * i + 1
    b_n = 2 * i_next

    @pl.when(first)
    def _():
        @pl.loop(0, cap)
        def _(jrow):
            gather_start(e, b_a, 0, jrow)

    wait_gather(0)
    for q in range(MOE_PARTS):
        jobs = [functools.partial(gather_start, e, b_b, 1, jrow)
                for jrow in range(q * part, (q + 1) * part)]
        if q > 0:
            jobs += [functools.partial(scatter_start, e, b_a, 0, jrow)
                     for jrow in range((q - 1) * part, q * part)]
        compute_part(0, 0, q, spread(jobs))

    wait_gather(1)
    rows_next = [cap * k // (MOE_PARTS - 1) for k in range(MOE_PARTS)]
    for q in range(MOE_PARTS):
        if q == 0:
            jobs = [functools.partial(scatter_start, e, b_a, 0, jrow)
                    for jrow in range((MOE_PARTS - 1) * part, cap)]
        else:
            jobs = [functools.partial(scatter_start, e, b_b, 1, jrow)
                    for jrow in range((q - 1) * part, q * part)]
            jobs += [functools.partial(gather_start, e_next, b_n, 0, jrow)
                     for jrow in range(rows_next[q - 1], rows_next[q])]
        if q == 1:
            wait_scatter(0)
        compute_part(1, 1, q, spread(jobs))

    for jrow in range((MOE_PARTS - 1) * part, cap):
        scatter_start(e, b_b, 1, jrow)
    wait_scatter(1)

    @pl.when(last)
    def _():
        wait_gather(0)


def _moe_call(idx_flat, f2d, gcol, g2, wg, wu, wd, h2d, *, batch, cap):
    m, d = h2d.shape
    n_tok = m // batch
    assert batch % 2 == 0 and cap % MOE_PARTS == 0
    n_pairs = batch // 2
    kern = functools.partial(_moe_kernel, cap=cap, n_tok=n_tok, n_pairs=n_pairs)
    grid_spec = pltpu.PrefetchScalarGridSpec(
        num_scalar_prefetch=1,
        grid=(N_EXPERTS, n_pairs),
        in_specs=[pl.BlockSpec(memory_space=pl.ANY),
                  pl.BlockSpec((2, 1, cap, LANES), lambda e, i, idx: (i, e, 0, 0)),
                  pl.BlockSpec((batch, 1, d), lambda e, i, idx: (0, 0, 0)),
                  pl.BlockSpec((1, d, EXPERT_FF), lambda e, i, idx: (e, 0, 0)),
                  pl.BlockSpec((1, d, EXPERT_FF), lambda e, i, idx: (e, 0, 0)),
                  pl.BlockSpec((1, EXPERT_FF, d), lambda e, i, idx: (e, 0, 0)),
                  pl.BlockSpec(memory_space=pl.ANY)],
        out_specs=pl.BlockSpec(memory_space=pl.ANY),
        scratch_shapes=[pltpu.VMEM((cap, d), F32), pltpu.VMEM((cap, d), F32),
                        pltpu.VMEM((cap, d), F32), pltpu.VMEM((cap, d), F32),
                        pltpu.SemaphoreType.DMA((6,))],
    )
    return pl.pallas_call(
        kern,
        grid_spec=grid_spec,
        out_shape=jax.ShapeDtypeStruct((m, d), F32),
        input_output_aliases={7: 0},
        compiler_params=_cparams(("arbitrary", "arbitrary")),
        name="moe",
    )(idx_flat, f2d, gcol, g2, wg, wu, wd, h2d)


def _final_kernel(h_ref, w_ref, o_ref):
    h = h_ref[...]
    ms = jnp.mean(h * h, axis=-1, keepdims=True)
    o_ref[...] = h * lax.rsqrt(ms + EPS) * w_ref[...]


def _final_call(h2d, w, *, tm):
    m, d = h2d.shape
    return pl.pallas_call(
        _final_kernel,
        grid=(m // tm,),
        in_specs=[pl.BlockSpec((tm, d), lambda i: (i, 0)), pl.BlockSpec((1, d), lambda i: (0, 0))],
        out_specs=pl.BlockSpec((tm, d), lambda i: (i, 0)),
        out_shape=jax.ShapeDtypeStruct((m, d), F32),
        compiler_params=_cparams(("arbitrary",)),
        name="final_norm",
    )(h2d, w)


def _pick_tile(n, pref):
    t = pref
    while n % t:
        t //= 2
    return t


def _ssd_params(a_log, dt_bias, d_skip, norm_w):
    padw = LANES - N_DT
    alog = jnp.pad(a_log.astype(F32), (0, padw))
    bias = jnp.pad(dt_bias.astype(F32), (0, padw))
    lane = jnp.arange(LANES)[:, None]
    head = jnp.arange(D_SSM)[None, :] // SSM_HEAD_DIM
    return {
        "alog_row": alog[None, :], "alog_col": alog[:, None],
        "bias_row": bias[None, :], "bias_col": bias[:, None],
        "expand_fwd": (lane == head).astype(BF16),
        "expand_rev": (lane == head + SSM_HEADS).astype(BF16),
        "dskip": jnp.repeat(d_skip.astype(F32), SSM_HEAD_DIM)[None, :],
        "norm_w": norm_w.astype(F32)[None, :],
    }


def kernel(x, c, ctx, c_ctx, w_mod, b_mod, norm_mix_w, w_in, ssm_conv_w, ssm_conv_b, ssm_a_log,
           ssm_dt_bias, ssm_d, ssm_norm_w, sc_conv_w, sc_norm_w, w_out, norm_ffn_w, w_router,
           w_gate, w_up, w_down, final_norm_w):
    batch, n_lat, d = x.shape
    n_ctx = ctx.shape[1]
    depth = w_mod.shape[0]
    assert depth == 1 and d == D_MODEL and n_ctx == BLK and n_lat % BLK == 0
    layer = 0
    cap = EC_CAPACITY_FACTOR * n_lat // N_EXPERTS

    pad_rows = -(batch + 1) % 8
    cvec = jnp.concatenate([c, c_ctx[None, :], jnp.zeros((pad_rows, d), F32)], axis=0)
    mod = _mod_call(cvec, w_mod[layer], b_mod[layer][None, :])
    sh1, sc1, g1, sh2, sc2, g2 = [mod[:batch, k * d:(k + 1) * d][:, None, :] for k in range(N_MOD)]
    sh1_c = mod[batch:batch + 1, 0:d][:, None, :]
    sc1_c = mod[batch:batch + 1, d:2 * d][:, None, :]

    w_in_l = w_in[layer]
    tc = 512
    w_b, w_c, w_v = [w_in_l[:, D_SSM_IN + k * D_SC:D_SSM_IN + (k + 1) * D_SC].astype(BF16)
                     for k in range(3)]
    w_out_b = w_out[layer].astype(BF16)
    w_rT = w_router[layer].T.astype(BF16)
    wg, wu, wd = w_gate[layer].astype(BF16), w_up[layer].astype(BF16), w_down[layer].astype(BF16)
    norm_mix = norm_mix_w[layer][None, :]
    conv_w, conv_b = ssm_conv_w[layer], ssm_conv_b[layer][None, :]
    prm = _ssd_params(ssm_a_log[layer], ssm_dt_bias[layer], ssm_d[layer], ssm_norm_w[layer])

    ctx2d = ctx.reshape(batch * n_ctx, d)
    xbc_c, dt_c, dtT_c = _inproj_zx_call(
        ctx2d, sh1_c, sc1_c, norm_mix, w_in_l, conv_w, conv_b,
        with_z=False, period=n_ctx, tm=_pick_tile(batch * n_ctx, 1024), rows_per_mod=batch * n_ctx)
    zero_state = jnp.zeros((batch, SSM_GROUPS, SSM_STATE, GROUP_W), F32)
    st_f = _ssd_call(xbc_c, dt_c, dtT_c, prm, zero_state, batch=batch, nblk=1, reverse=False,
                     mode="state")
    st_r = _ssd_call(xbc_c, dt_c, dtT_c, prm, zero_state, batch=batch, nblk=1, reverse=True,
                     mode="state")

    x2d = x.reshape(batch * n_lat, d)
    tm = _pick_tile(n_lat, 1024)
    z, xbc, dt, dtT = _inproj_zx_call(
        x2d, sh1, sc1, norm_mix, w_in_l, conv_w, conv_b,
        with_z=True, period=GRID_W, tm=tm, rows_per_mod=n_lat)
    y_sc = _inproj_sc_call(x2d, sh1, sc1, norm_mix, w_b, w_c, w_v, sc_conv_w[layer], tm=tm, tc=tc,
                           rows_per_mod=n_lat)
    nblk = n_lat // BLK
    y_rev = _ssd_call(xbc, dt, dtT, prm, st_r, batch=batch, nblk=nblk, reverse=True, mode="partial")
    y_ssm = _ssd_call(xbc, dt, dtT, prm, st_f, batch=batch, nblk=nblk, reverse=False, mode="final",
                      extra=(z, y_rev))

    h1, f_lat, aff = _outproj_call(
        y_ssm, y_sc, sc_norm_w[layer][None, :], w_out_b, x2d, g1, norm_ffn_w[layer][None, :],
        sh2, sc2, w_rT, batch=batch, tm=_pick_tile(n_lat, 256))
    idx, gcol = _route_call(aff, cap=cap)
    h2 = _moe_call(idx.reshape(-1), f_lat, gcol, g2, wg, wu, wd, h1, batch=batch, cap=cap)
    out = _final_call(h2, final_norm_w[None, :], tm=_pick_tile(batch * n_lat, 512))
    return out.reshape(batch, n_lat, d)
```

```python
import functools

import jax
import jax.numpy as jnp
from jax import lax
from jax.experimental import pallas as pl
from jax.experimental.pallas import tpu as pltpu

F32 = jnp.float32
BF16 = jnp.bfloat16
I32 = jnp.int32

D_MODEL = 2048
GRID_W = 64
D_SSM = 2048
SSM_HEAD_DIM = 64
SSM_HEADS = D_SSM // SSM_HEAD_DIM
SSM_GROUPS = 4
SSM_HPG = SSM_HEADS // SSM_GROUPS
SSM_STATE = 128
SSD_CHUNK = 128
GN = SSM_GROUPS * SSM_STATE
D_XBC = D_SSM + 2 * GN
D_ZX = D_SSM + D_XBC
N_DT = 2 * SSM_HEADS
D_SSM_IN = D_ZX + N_DT
D_SC = 2048
N_EXPERTS = 16
EC_CAPACITY_FACTOR = 2
EXPERT_FF = 1024
N_MOD = 6
EPS = 1e-6
LOG2E = 1.4426950408889634

LANES = 128
BLK = 2 * SSD_CHUNK
GROUP_W = SSM_HPG * SSM_HEAD_DIM
VMEM_LIMIT = 56 * 1024 * 1024


def _cparams(sem):
    return pltpu.CompilerParams(dimension_semantics=sem, vmem_limit_bytes=VMEM_LIMIT)


def _silu(v):
    return v * jax.nn.sigmoid(v)


def _softplus(v):
    return jnp.maximum(v, 0.0) + jnp.log1p(jnp.exp(-jnp.abs(v)))


def _mod_kernel(c_ref, w_ref, b_ref, o_ref):
    s = _silu(c_ref[...]).astype(BF16)
    o_ref[...] = jnp.dot(s, w_ref[...].astype(BF16), preferred_element_type=F32) + b_ref[...]


def _mod_call(cvec, w_mod, b_mod):
    rows, d = cvec.shape
    n = w_mod.shape[1]
    tn = 1024
    return pl.pallas_call(
        _mod_kernel,
        grid=(n // tn,),
        in_specs=[pl.BlockSpec((rows, d), lambda j: (0, 0)),
                  pl.BlockSpec((d, tn), lambda j: (0, j)),
                  pl.BlockSpec((1, tn), lambda j: (0, j))],
        out_specs=pl.BlockSpec((rows, tn), lambda j: (0, j)),
        out_shape=jax.ShapeDtypeStruct((rows, n), F32),
        compiler_params=_cparams(("arbitrary",)),
        name="mod",
    )(cvec, w_mod, b_mod)


def _dot_nt(a, bt):
    return lax.dot_general(a, bt, (((1,), (1,)), ((), ())), preferred_element_type=F32)


def _norm_modulate(h, nw, shift, scale):
    ms = jnp.mean(h * h, axis=-1, keepdims=True)
    hn = h * lax.rsqrt(ms + EPS) * nw
    return hn * (1.0 + scale) + shift


def _conv3(v, cw, period):
    rows = v.shape[0]
    r = lax.broadcasted_iota(I32, (rows, 1), 0) % period
    prev = jnp.where(r == 0, 0.0, pltpu.roll(v, 1, 0))
    nxt = jnp.where(r == period - 1, 0.0, pltpu.roll(v, rows - 1, 0))
    return prev * cw[0:1] + v * cw[1:2] + nxt * cw[2:3]


def _inproj_zx_kernel(h_ref, sh_ref, sc_ref, nw_ref, w_ref, wdt_ref, cw_ref, cb_ref,
                      *rest, with_z, period, tm, n_z):
    if with_z:
        z_ref, xbc_ref, dt_ref, dtT_ref, a_ref = rest
    else:
        xbc_ref, dt_ref, dtT_ref, a_ref = rest
    n = pl.program_id(1)

    @pl.when(n == 0)
    def _():
        a = _norm_modulate(h_ref[...], nw_ref[...], sh_ref[0], sc_ref[0]).astype(BF16)
        a_ref[...] = a
        wdt = wdt_ref[...].astype(BF16)
        dt_ref[...] = _dot_nt(a, wdt)
        dtT_ref[...] = _dot_nt(wdt, a)

    wb = w_ref[...].astype(BF16)

    if with_z:
        @pl.when(n < n_z)
        def _():
            for k in range(tm // BLK):
                rows = slice(k * BLK, (k + 1) * BLK)
                z_ref[rows, :] = _dot_nt(a_ref[rows, :], wb).astype(BF16)

    @pl.when(n >= n_z)
    def _():
        for k in range(tm // BLK):
            rows = slice(k * BLK, (k + 1) * BLK)
            acc = _dot_nt(a_ref[rows, :], wb)
            y = _conv3(acc, cw_ref[...], period) + cb_ref[...]
            xbc_ref[rows, :] = _silu(y).astype(BF16)


def _inproj_zx_call(h2d, shift, scale, norm_w, w_inT, conv_w, conv_b, *,
                    with_z, period, tm, rows_per_mod):
    m, d = h2d.shape
    tn = 1024
    n_z = D_SSM // tn if with_z else 0
    n_x = D_XBC // tn
    n_off = 0 if with_z else D_SSM // tn
    per = rows_per_mod // tm
    kern = functools.partial(_inproj_zx_kernel, with_z=with_z, period=period, tm=tm, n_z=n_z)
    out_shape = [jax.ShapeDtypeStruct((m, D_XBC), BF16),
                 jax.ShapeDtypeStruct((m, LANES), F32),
                 jax.ShapeDtypeStruct((LANES, m), F32)]
    out_specs = [pl.BlockSpec((tm, tn), lambda i, n: (i, jnp.maximum(n - n_z, 0))),
                 pl.BlockSpec((tm, LANES), lambda i, n: (i, 0)),
                 pl.BlockSpec((LANES, tm), lambda i, n: (0, i))]
    if with_z:
        out_shape = [jax.ShapeDtypeStruct((m, D_SSM), BF16)] + out_shape
        out_specs = [pl.BlockSpec((tm, tn), lambda i, n: (i, jnp.minimum(n, n_z - 1)))] + out_specs
    return pl.pallas_call(
        kern,
        grid=(m // tm, n_z + n_x),
        in_specs=[pl.BlockSpec((tm, d), lambda i, n: (i, 0)),
                  pl.BlockSpec((1, 1, d), lambda i, n: (i // per, 0, 0)),
                  pl.BlockSpec((1, 1, d), lambda i, n: (i // per, 0, 0)),
                  pl.BlockSpec((1, d), lambda i, n: (0, 0)),
                  pl.BlockSpec((tn, d), lambda i, n: (n + n_off, 0)),
                  pl.BlockSpec((LANES, d), lambda i, n: (D_ZX // LANES, 0)),
                  pl.BlockSpec((3, tn), lambda i, n: (0, jnp.maximum(n - n_z, 0))),
                  pl.BlockSpec((1, tn), lambda i, n: (0, jnp.maximum(n - n_z, 0)))],
        out_specs=out_specs,
        out_shape=out_shape,
        scratch_shapes=[pltpu.VMEM((tm, d), BF16)],
        compiler_params=_cparams(("arbitrary", "arbitrary")),
        name="inproj_zx" if with_z else "inproj_ctx",
    )(h2d, shift, scale, norm_w, w_inT, w_inT, conv_w, conv_b)


def _inproj_sc_kernel(h_ref, sh_ref, sc_ref, nw_ref, wb_ref, wc_ref, wv_ref, cw_ref, o_ref, a_ref,
                      *, tm):
    n = pl.program_id(1)

    @pl.when(n == 0)
    def _():
        a_ref[...] = _norm_modulate(h_ref[...], nw_ref[...], sh_ref[0], sc_ref[0]).astype(BF16)

    wb, wc, wv = [r[...].astype(BF16) for r in (wb_ref, wc_ref, wv_ref)]
    for k in range(tm // BLK):
        rows = slice(k * BLK, (k + 1) * BLK)
        a = a_ref[rows, :]
        cg = _dot_nt(a, wc)
        hv = _dot_nt(a, wv)
        v = _conv3(cg * hv, cw_ref[...], GRID_W)
        bg = _dot_nt(a, wb)
        o_ref[rows, :] = (bg * v).astype(BF16)


def _inproj_sc_call(h2d, shift, scale, norm_w, w_inT, conv_w, *, tm, tc, rows_per_mod):
    m, d = h2d.shape
    per = rows_per_mod // tm
    kern = functools.partial(_inproj_sc_kernel, tm=tm)

    def wspec(k):
        return pl.BlockSpec((pl.Element(tc), pl.Element(d)),
                            lambda i, n: (pl.multiple_of(D_SSM_IN + k * D_SC + n * tc, 8), 0))

    return pl.pallas_call(
        kern,
        grid=(m // tm, D_SC // tc),
        in_specs=[pl.BlockSpec((tm, d), lambda i, n: (i, 0)),
                  pl.BlockSpec((1, 1, d), lambda i, n: (i // per, 0, 0)),
                  pl.BlockSpec((1, 1, d), lambda i, n: (i // per, 0, 0)),
                  pl.BlockSpec((1, d), lambda i, n: (0, 0)),
                  wspec(0), wspec(1), wspec(2),
                  pl.BlockSpec((3, tc), lambda i, n: (0, n))],
        out_specs=pl.BlockSpec((tm, tc), lambda i, n: (i, n)),
        out_shape=jax.ShapeDtypeStruct((m, D_SC), BF16),
        scratch_shapes=[pltpu.VMEM((tm, d), BF16)],
        compiler_params=_cparams(("arbitrary", "arbitrary")),
        name="inproj_sc",
    )(h2d, shift, scale, norm_w, w_inT, w_inT, w_inT, conv_w)


def _ssd_kernel(*refs, reverse, mode, nblk):
    xbc_ref, dt_ref, dtT_ref, alr_ref, alc_ref, br_ref, bc_ref, e_ref, init_ref = refs[:9]
    rest = refs[9:]
    if mode == "state":
        out_ref, s_ref = rest
    elif mode == "partial":
        out_ref, s_ref = rest
    else:
        z_ref, yo_ref, dsk_ref, nw_ref, out_ref, s_ref = rest
    j = pl.program_id(1)

    @pl.when(j == 0)
    def _():
        s_ref[...] = init_ref[0]

    ii = lax.broadcasted_iota(I32, (SSD_CHUNK, SSD_CHUNK), 0)
    jj = lax.broadcasted_iota(I32, (SSD_CHUNK, SSD_CHUNK), 1)
    tri = jnp.where(jj <= ii, 1.0, 0.0).astype(F32)
    triT = jnp.where(ii <= jj, 1.0, 0.0).astype(F32)
    mask = (jj >= ii) if reverse else (ii >= jj)
    lane4 = lax.broadcasted_iota(I32, (SSD_CHUNK, 4 * SSM_HEAD_DIM), 1) // SSM_HEAD_DIM
    aneg_r = -jnp.exp(alr_ref[...]) * LOG2E
    aneg_c = -jnp.exp(alc_ref[...]) * LOG2E
    lane0 = SSM_HEADS if reverse else 0
    hi = lax.Precision.HIGHEST

    def chunk(c):
        rows = slice(c * SSD_CHUNK, (c + 1) * SSD_CHUNK)
        dt = _softplus(dt_ref[rows, :] + br_ref[...])
        dtT = _softplus(dtT_ref[:, rows] + bc_ref[...])
        la = dt * aneg_r
        laT = dtT * aneg_c
        cs = jnp.dot(tri, la, precision=hi, preferred_element_type=F32)
        csT = jnp.dot(laT, triT, precision=hi, preferred_element_type=F32)
        total = cs[SSD_CHUNK - 1:SSD_CHUNK, :]
        log2_dtT = jnp.log(dtT) * LOG2E
        if reverse:
            ecs = cs - la
            col, rowm = -ecs, csT - laT + log2_dtT
            yscale = jnp.exp2(total - ecs)
            w = dt * jnp.exp2(ecs)
        else:
            col, rowm = cs, log2_dtT - csT
            yscale = jnp.exp2(cs)
            w = dt * jnp.exp2(total - cs)
        t1 = total.astype(BF16).astype(F32)
        r1 = total - t1
        t2 = r1.astype(BF16).astype(F32)
        t3 = r1 - t2
        r16 = lax.broadcasted_iota(I32, (16, LANES), 0)
        tail = jnp.where(r16 == 0, t1, jnp.where(r16 == 1, t2, jnp.where(r16 == 2, t3, 0.0)))
        stack = jnp.concatenate([yscale, w, tail], axis=0).astype(BF16)
        ex = jnp.dot(stack, e_ref[...], preferred_element_type=F32)
        yscale_x = ex[0:SSD_CHUNK]
        w_x = ex[SSD_CHUNK:2 * SSD_CHUNK]
        tb = 2 * SSD_CHUNK
        sdec_x = jnp.exp2(ex[tb:tb + 1] + ex[tb + 1:tb + 2] + ex[tb + 2:tb + 3])

        for g in range(SSM_GROUPS):
            gsl = slice(g * GROUP_W, (g + 1) * GROUP_W)
            xg = xbc_ref[rows, gsl]
            bg = xbc_ref[rows, D_SSM + g * SSM_STATE:D_SSM + (g + 1) * SSM_STATE]
            s_old = s_ref[g]
            if mode != "state":
                cg = xbc_ref[rows, D_SSM + GN + g * SSM_STATE:D_SSM + GN + (g + 1) * SSM_STATE]
                cb = lax.dot_general(cg, bg, (((1,), (1,)), ((), ())), preferred_element_type=F32)
                y_off = jnp.dot(cg, s_old.astype(BF16), preferred_element_type=F32)
                halves = []
                for q in range(2):
                    xq = xg[:, q * 256:(q + 1) * 256]
                    acc = jnp.zeros((SSD_CHUNK, 256), F32)
                    for r4 in range(4):
                        lane = lane0 + g * SSM_HPG + q * 4 + r4
                        seg = col[:, lane:lane + 1] + rowm[lane:lane + 1, :]
                        dec = jnp.exp2(jnp.where(mask, seg, -jnp.inf))
                        mh = (cb * dec).astype(BF16)
                        xm = jnp.where(lane4 == r4, xq, jnp.zeros_like(xq))
                        acc = acc + jnp.dot(mh, xm, preferred_element_type=F32)
                    halves.append(acc)
                y = jnp.concatenate(halves, axis=1) + yscale_x[:, gsl] * y_off
                if mode == "partial":
                    out_ref[rows, gsl] = y.astype(BF16)
                else:
                    y = y + yo_ref[rows, gsl].astype(F32) + dsk_ref[:, gsl] * xg.astype(F32)
                    y = y * _silu(z_ref[rows, gsl].astype(F32))
                    ms = jnp.mean(y * y, axis=-1, keepdims=True)
                    out_ref[rows, gsl] = (y * lax.rsqrt(ms + EPS) * nw_ref[:, gsl]).astype(BF16)
            xw = (xg.astype(F32) * w_x[:, gsl]).astype(BF16)
            upd = lax.dot_general(bg, xw, (((0,), (0,)), ((), ())), preferred_element_type=F32)
            s_ref[g] = sdec_x[:, gsl] * s_old + upd

    for c in ((1, 0) if reverse else (0, 1)):
        chunk(c)

    if mode == "state":
        @pl.when(j == nblk - 1)
        def _():
            out_ref[0] = s_ref[...]


def _ssd_call(xbc, dt, dtT, prm, init, *, batch, nblk, reverse, mode, extra=()):
    m = xbc.shape[0]

    def blk(b, j):
        return b * nblk + (nblk - 1 - j if reverse else j)

    state_spec = pl.BlockSpec((1, SSM_GROUPS, SSM_STATE, GROUP_W), lambda b, j: (b, 0, 0, 0))
    const2 = lambda b, j: (0, 0)
    in_specs = [pl.BlockSpec((BLK, D_XBC), lambda b, j: (blk(b, j), 0)),
                pl.BlockSpec((BLK, LANES), lambda b, j: (blk(b, j), 0)),
                pl.BlockSpec((LANES, BLK), lambda b, j: (0, blk(b, j))),
                pl.BlockSpec((1, LANES), const2), pl.BlockSpec((LANES, 1), const2),
                pl.BlockSpec((1, LANES), const2), pl.BlockSpec((LANES, 1), const2),
                pl.BlockSpec((LANES, D_SSM), const2),
                state_spec]
    args = [xbc, dt, dtT, prm["alog_row"], prm["alog_col"], prm["bias_row"], prm["bias_col"],
            prm["expand_rev" if reverse else "expand_fwd"], init]
    if mode == "state":
        out_specs = state_spec
        out_shape = jax.ShapeDtypeStruct((batch, SSM_GROUPS, SSM_STATE, GROUP_W), F32)
    else:
        out_specs = pl.BlockSpec((BLK, D_SSM), lambda b, j: (blk(b, j), 0))
        out_shape = jax.ShapeDtypeStruct((m, D_SSM), BF16)
    if mode == "final":
        z, y_other = extra
        in_specs += [pl.BlockSpec((BLK, D_SSM), lambda b, j: (blk(b, j), 0)),
                     pl.BlockSpec((BLK, D_SSM), lambda b, j: (blk(b, j), 0)),
                     pl.BlockSpec((1, D_SSM), const2), pl.BlockSpec((1, D_SSM), const2)]
        args += [z, y_other, prm["dskip"], prm["norm_w"]]
    kern = functools.partial(_ssd_kernel, reverse=reverse, mode=mode, nblk=nblk)
    return pl.pallas_call(
        kern,
        grid=(batch, nblk),
        in_specs=in_specs,
        out_specs=out_specs,
        out_shape=out_shape,
        scratch_shapes=[pltpu.VMEM((SSM_GROUPS, SSM_STATE, GROUP_W), F32)],
        compiler_params=_cparams(("arbitrary", "arbitrary")),
        name=f"ssd_{mode}_{'rev' if reverse else 'fwd'}",
    )(*args)


def _outproj_kernel(ys_ref, yc_ref, scw_ref, w_ref, x_ref, g1_ref, nw_ref, sh_ref, sc_ref, wr_ref,
                    h_ref, f_ref, aff_ref):
    yc = yc_ref[...].astype(F32)
    ms = jnp.mean(yc * yc, axis=-1, keepdims=True)
    ycn = (yc * lax.rsqrt(ms + EPS) * scw_ref[...]).astype(BF16)
    mix = jnp.dot(ys_ref[...], w_ref[0:D_SSM, :], preferred_element_type=F32)
    mix = mix + jnp.dot(ycn, w_ref[D_SSM:, :], preferred_element_type=F32)
    h = x_ref[...] + g1_ref[0] * mix
    h_ref[...] = h
    f = _norm_modulate(h, nw_ref[...], sh_ref[0], sc_ref[0])
    f_ref[...] = f
    logits = lax.dot_general(wr_ref[...], f.astype(BF16), (((1,), (1,)), ((), ())),
                             preferred_element_type=F32)
    mx = jnp.max(logits, axis=0, keepdims=True)
    ex = jnp.exp(logits - mx)
    aff_ref[0] = ex / jnp.sum(ex, axis=0, keepdims=True)


def _outproj_call(y_ssm, y_sc, sc_norm_w, w_out, x2d, g1, norm_w, shift, scale, w_rT, *, batch, tm):
    m, d = x2d.shape
    n_tok = m // batch
    per = n_tok // tm
    vec = lambda i: (i // per, 0, 0)
    const2 = lambda i: (0, 0)
    return pl.pallas_call(
        _outproj_kernel,
        grid=(m // tm,),
        in_specs=[pl.BlockSpec((tm, D_SSM), lambda i: (i, 0)),
                  pl.BlockSpec((tm, D_SC), lambda i: (i, 0)),
                  pl.BlockSpec((1, D_SC), const2),
                  pl.BlockSpec((D_SSM + D_SC, d), const2),
                  pl.BlockSpec((tm, d), lambda i: (i, 0)),
                  pl.BlockSpec((1, 1, d), vec),
                  pl.BlockSpec((1, d), const2),
                  pl.BlockSpec((1, 1, d), vec),
                  pl.BlockSpec((1, 1, d), vec),
                  pl.BlockSpec((N_EXPERTS, d), const2)],
        out_specs=[pl.BlockSpec((tm, d), lambda i: (i, 0)),
                   pl.BlockSpec((tm, d), lambda i: (i, 0)),
                   pl.BlockSpec((1, N_EXPERTS, tm), lambda i: (i // per, 0, i % per))],
        out_shape=[jax.ShapeDtypeStruct((m, d), F32),
                   jax.ShapeDtypeStruct((m, d), F32),
                   jax.ShapeDtypeStruct((batch, N_EXPERTS, n_tok), F32)],
        compiler_params=_cparams(("arbitrary",)),
        name="outproj",
    )(y_ssm, y_sc, sc_norm_w, w_out, x2d, g1, norm_w, shift, scale, w_rT)


def _prefix_lanes(m01):
    rows, n = m01.shape
    ki = lax.broadcasted_iota(I32, (LANES, LANES), 0)
    ci = lax.broadcasted_iota(I32, (LANES, LANES), 1)
    triu = jnp.where(ki <= ci, 1.0, 0.0).astype(BF16)
    off = jnp.zeros((rows, 1), F32)
    parts = []
    for k in range(n // LANES):
        p = jnp.dot(m01[:, k * LANES:(k + 1) * LANES].astype(BF16), triu,
                    preferred_element_type=F32)
        parts.append(p + off)
        off = off + p[:, LANES - 1:LANES]
    return jnp.concatenate(parts, axis=1)


def _route_kernel(aff_ref, idx_ref, gcol_ref, *, cap):
    a = aff_ref[0]
    n_e, n = a.shape
    thr = jnp.zeros((n_e, 1), I32)
    for bit in range(30, -1, -1):
        cand = thr | (1 << bit)
        cand_f = lax.bitcast_convert_type(cand, F32)
        cnt = jnp.sum(jnp.where(a >= cand_f, 1.0, 0.0), axis=1, keepdims=True)
        thr = jnp.where(cnt >= cap, cand, thr)
    thr_f = lax.bitcast_convert_type(thr, F32)
    gt = a > thr_f
    eq = a == thr_f
    need = cap - jnp.sum(jnp.where(gt, 1.0, 0.0), axis=1, keepdims=True)
    eqf = jnp.where(eq, 1.0, 0.0)
    ties_before = _prefix_lanes(eqf) - eqf
    sel = jnp.logical_or(gt, jnp.logical_and(eq, ties_before < need))
    self_ = jnp.where(sel, 1.0, 0.0)
    slot = jnp.where(sel, _prefix_lanes(self_) - 1.0, -1.0).astype(I32)

    tok = lax.broadcasted_iota(I32, (1, n), 1)
    t_hi = (tok // 64).astype(F32)
    t_lo = (tok % 64).astype(F32)
    a1 = a.astype(BF16).astype(F32)
    ra = a - a1
    a2 = ra.astype(BF16).astype(F32)
    a3 = ra - a2
    srow = lax.broadcasted_iota(I32, (cap, n), 0)
    r8 = lax.broadcasted_iota(I32, (8, n), 0)
    zpad = jnp.zeros((LANES - 8, n), F32)
    for e in range(n_e):
        onehot = jnp.where(slot[e:e + 1, :] == srow, 1.0, 0.0).astype(BF16)
        v8 = jnp.where(r8 == 0, t_hi, jnp.where(r8 == 1, t_lo, jnp.where(
            r8 == 2, a1[e:e + 1], jnp.where(r8 == 3, a2[e:e + 1], jnp.where(
                r8 == 4, a3[e:e + 1], 0.0)))))
        vals = jnp.concatenate([v8, zpad], axis=0).astype(BF16)
        o_col = lax.dot_general(onehot, vals, (((1,), (1,)), ((), ())),
                                preferred_element_type=F32)
        o_row = o_col.T
        idx_ref[0, e:e + 1, :] = (o_row[0:1] * 64.0 + o_row[1:2]).astype(I32)
        gcol_ref[0, e] = o_col


def _route_call(aff, *, cap):
    batch, n_e, n = aff.shape
    return pl.pallas_call(
        functools.partial(_route_kernel, cap=cap),
        grid=(batch,),
        in_specs=[pl.BlockSpec((1, n_e, n), lambda b: (b, 0, 0))],
        out_specs=[pl.BlockSpec((1, n_e, cap), lambda b: (b, 0, 0)),
                   pl.BlockSpec((1, n_e, cap, LANES), lambda b: (b, 0, 0, 0))],
        out_shape=[jax.ShapeDtypeStruct((batch, n_e, cap), I32),
                   jax.ShapeDtypeStruct((batch, n_e, cap, LANES), F32)],
        compiler_params=_cparams(("arbitrary",)),
        name="route",
    )(aff)


MOE_PARTS = 4
MOE_NCHUNK = 512


def _moe_kernel(idx_ref, f_hbm, gcol_ref, g2_ref, wg_ref, wu_ref, wd_ref, hin_hbm, hout_hbm,
                xb0, xb1, hb0, hb1, sem, *, cap, n_tok, n_pairs):
    e = pl.program_id(0)
    i = pl.program_id(1)
    first = jnp.logical_and(e == 0, i == 0)
    last = jnp.logical_and(e == pl.num_programs(0) - 1, i == n_pairs - 1)
    i_next = jnp.where(i == n_pairs - 1, 0, i + 1)
    e_next = jnp.where(i == n_pairs - 1, e + 1, e)
    xbufs, hbufs = (xb0, xb1), (hb0, hb1)
    part = cap // MOE_PARTS

    def row_of(ex, b, jrow):
        return b * n_tok + idx_ref[(b * N_EXPERTS + ex) * cap + jrow]

    def gather_start(ex, b, s, jrow):
        row = row_of(ex, b, jrow)
        pltpu.make_async_copy(f_hbm.at[pl.ds(row, 1)], xbufs[s].at[pl.ds(jrow, 1)],
                              sem.at[s]).start()
        pltpu.make_async_copy(hin_hbm.at[pl.ds(row, 1)], hbufs[s].at[pl.ds(jrow, 1)],
                              sem.at[2 + s]).start()

    def scatter_start(ex, b, s, jrow):
        pltpu.make_async_copy(hbufs[s].at[pl.ds(jrow, 1)],
                              hout_hbm.at[pl.ds(row_of(ex, b, jrow), 1)], sem.at[4 + s]).start()

    def wait_gather(s):
        pltpu.make_async_copy(f_hbm.at[pl.ds(0, cap)], xbufs[s], sem.at[s]).wait()
        pltpu.make_async_copy(hin_hbm.at[pl.ds(0, cap)], hbufs[s], sem.at[2 + s]).wait()

    def wait_scatter(s):
        pltpu.make_async_copy(hbufs[s], hout_hbm.at[pl.ds(0, cap)], sem.at[4 + s]).wait()

    def compute_part(s, b_local, q, between):
        rows = slice(q * part, (q + 1) * part)
        xq = xbufs[s][rows, :].astype(BF16)
        gc = gcol_ref[b_local, 0, rows, :]
        gate = gc[:, 2:3] + gc[:, 3:4] + gc[:, 4:5]
        g2 = g2_ref[2 * i + b_local]
        nc = MOE_NCHUNK
        hid = []
        for c in range(EXPERT_FF // nc):
            cols = slice(c * nc, (c + 1) * nc)
            between[2 * c]()
            hg = jnp.dot(xq, wg_ref[0, :, cols], preferred_element_type=F32)
            between[2 * c + 1]()
            hu = jnp.dot(xq, wu_ref[0, :, cols], preferred_element_type=F32)
            hid.append((_silu(hg) * hu).astype(BF16))
        hid = jnp.concatenate(hid, axis=1)
        k0 = 2 * (EXPERT_FF // nc)
        for c in range(D_MODEL // nc):
            cols = slice(c * nc, (c + 1) * nc)
            between[k0 + c]()
            ye = jnp.dot(hid, wd_ref[0, :, cols], preferred_element_type=F32)
            hbufs[s][rows, cols] = hbufs[s][rows, cols] + g2[:, cols] * (ye * gate)

    n_between = 2 * (EXPERT_FF // MOE_NCHUNK) + D_MODEL // MOE_NCHUNK

    def spread(jobs):
        n = len(jobs)
        cuts = [n * k // n_between for k in range(n_between + 1)]

        def group(k):
            def run():
                for job in jobs[cuts[k]:cuts[k + 1]]:
                    job()
            return run
        return [group(k) for k in range(n_between)]

    b_a, b_b = 2 * i, 2 * i + 1
    b_n = 2 * i_next

    @pl.when(first)
    def _():
        @pl.loop(0, cap)
        def _(jrow):
            gather_start(e, b_a, 0, jrow)

    wait_gather(0)
    for q in range(MOE_PARTS):
        jobs = [functools.partial(gather_start, e, b_b, 1, jrow)
                for jrow in range(q * part, (q + 1) * part)]
        if q > 0:
            jobs += [functools.partial(scatter_start, e, b_a, 0, jrow)
                     for jrow in range((q - 1) * part, q * part)]
        compute_part(0, 0, q, spread(jobs))

    wait_gather(1)
    rows_next = [cap * k // (MOE_PARTS - 1) for k in range(MOE_PARTS)]
    for q in range(MOE_PARTS):
        if q == 0:
            jobs = [functools.partial(scatter_start, e, b_a, 0, jrow)
                    for jrow in range((MOE_PARTS - 1) * part, cap)]
        else:
            jobs = [functools.partial(scatter_start, e, b_b, 1, jrow)
                    for jrow in range((q - 1) * part, q * part)]
            jobs += [functools.partial(gather_start, e_next, b_n, 0, jrow)
                     for jrow in range(rows_next[q - 1], rows_next[q])]
        if q == 1:
            wait_scatter(0)
        compute_part(1, 1, q, spread(jobs))

    for jrow in range((MOE_PARTS - 1) * part, cap):
        scatter_start(e, b_b, 1, jrow)
    wait_scatter(1)

    @pl.when(last)
    def _():
        wait_gather(0)


def _moe_call(idx_flat, f2d, gcol, g2, wg, wu, wd, h2d, *, batch, cap):
    m, d = h2d.shape
    n_tok = m // batch
    assert batch % 2 == 0 and cap % MOE_PARTS == 0
    n_pairs = batch // 2
    kern = functools.partial(_moe_kernel, cap=cap, n_tok=n_tok, n_pairs=n_pairs)
    grid_spec = pltpu.PrefetchScalarGridSpec(
        num_scalar_prefetch=1,
        grid=(N_EXPERTS, n_pairs),
        in_specs=[pl.BlockSpec(memory_space=pl.ANY),
                  pl.BlockSpec((2, 1, cap, LANES), lambda e, i, idx: (i, e, 0, 0)),
                  pl.BlockSpec((batch, 1, d), lambda e, i, idx: (0, 0, 0)),
                  pl.BlockSpec((1, d, EXPERT_FF), lambda e, i, idx: (e, 0, 0)),
                  pl.BlockSpec((1, d, EXPERT_FF), lambda e, i, idx: (e, 0, 0)),
                  pl.BlockSpec((1, EXPERT_FF, d), lambda e, i, idx: (e, 0, 0)),
                  pl.BlockSpec(memory_space=pl.ANY)],
        out_specs=pl.BlockSpec(memory_space=pl.ANY),
        scratch_shapes=[pltpu.VMEM((cap, d), F32), pltpu.VMEM((cap, d), F32),
                        pltpu.VMEM((cap, d), F32), pltpu.VMEM((cap, d), F32),
                        pltpu.SemaphoreType.DMA((6,))],
    )
    return pl.pallas_call(
        kern,
        grid_spec=grid_spec,
        out_shape=jax.ShapeDtypeStruct((m, d), F32),
        input_output_aliases={7: 0},
        compiler_params=_cparams(("arbitrary", "arbitrary")),
        name="moe",
    )(idx_flat, f2d, gcol, g2, wg, wu, wd, h2d)


def _final_kernel(h_ref, w_ref, o_ref):
    h = h_ref[...]
    ms = jnp.mean(h * h, axis=-1, keepdims=True)
    o_ref[...] = h * lax.rsqrt(ms + EPS) * w_ref[...]


def _final_call(h2d, w, *, tm):
    m, d = h2d.shape
    return pl.pallas_call(
        _final_kernel,
        grid=(m // tm,),
        in_specs=[pl.BlockSpec((tm, d), lambda i: (i, 0)), pl.BlockSpec((1, d), lambda i: (0, 0))],
        out_specs=pl.BlockSpec((tm, d), lambda i: (i, 0)),
        out_shape=jax.ShapeDtypeStruct((m, d), F32),
        compiler_params=_cparams(("arbitrary",)),
        name="final_norm",
    )(h2d, w)


def _pick_tile(n, pref):
    t = pref
    while n % t:
        t //= 2
    return t


def _ssd_params(a_log, dt_bias, d_skip, norm_w):
    padw = LANES - N_DT
    alog = jnp.pad(a_log.astype(F32), (0, padw))
    bias = jnp.pad(dt_bias.astype(F32), (0, padw))
    lane = jnp.arange(LANES)[:, None]
    head = jnp.arange(D_SSM)[None, :] // SSM_HEAD_DIM
    return {
        "alog_row": alog[None, :], "alog_col": alog[:, None],
        "bias_row": bias[None, :], "bias_col": bias[:, None],
        "expand_fwd": (lane == head).astype(BF16),
        "expand_rev": (lane == head + SSM_HEADS).astype(BF16),
        "dskip": jnp.repeat(d_skip.astype(F32), SSM_HEAD_DIM)[None, :],
        "norm_w": norm_w.astype(F32)[None, :],
    }


def kernel(x, c, ctx, c_ctx, w_mod, b_mod, norm_mix_w, w_in, ssm_conv_w, ssm_conv_b, ssm_a_log,
           ssm_dt_bias, ssm_d, ssm_norm_w, sc_conv_w, sc_norm_w, w_out, norm_ffn_w, w_router,
           w_gate, w_up, w_down, final_norm_w):
    batch, n_lat, d = x.shape
    n_ctx = ctx.shape[1]
    depth = w_mod.shape[0]
    assert depth == 1 and d == D_MODEL and n_ctx == BLK and n_lat % BLK == 0
    layer = 0
    cap = EC_CAPACITY_FACTOR * n_lat // N_EXPERTS

    pad_rows = -(batch + 1) % 8
    cvec = jnp.concatenate([c, c_ctx[None, :], jnp.zeros((pad_rows, d), F32)], axis=0)
    mod = _mod_call(cvec, w_mod[layer], b_mod[layer][None, :])
    sh1, sc1, g1, sh2, sc2, g2 = [mod[:batch, k * d:(k + 1) * d][:, None, :] for k in range(N_MOD)]
    sh1_c = mod[batch:batch + 1, 0:d][:, None, :]
    sc1_c = mod[batch:batch + 1, d:2 * d][:, None, :]

    w_inT = w_in[layer].T
    tc = 256
    w_out_b = w_out[layer].astype(BF16)
    w_rT = w_router[layer].T.astype(BF16)
    wg, wu, wd = w_gate[layer].astype(BF16), w_up[layer].astype(BF16), w_down[layer].astype(BF16)
    norm_mix = norm_mix_w[layer][None, :]
    conv_w, conv_b = ssm_conv_w[layer], ssm_conv_b[layer][None, :]
    prm = _ssd_params(ssm_a_log[layer], ssm_dt_bias[layer], ssm_d[layer], ssm_norm_w[layer])

    ctx2d = ctx.reshape(batch * n_ctx, d)
    xbc_c, dt_c, dtT_c = _inproj_zx_call(
        ctx2d, sh1_c, sc1_c, norm_mix, w_inT, conv_w, conv_b,
        with_z=False, period=n_ctx, tm=_pick_tile(batch * n_ctx, 1024), rows_per_mod=batch * n_ctx)
    zero_state = jnp.zeros((batch, SSM_GROUPS, SSM_STATE, GROUP_W), F32)
    st_f = _ssd_call(xbc_c, dt_c, dtT_c, prm, zero_state, batch=batch, nblk=1, reverse=False,
                     mode="state")
    st_r = _ssd_call(xbc_c, dt_c, dtT_c, prm, zero_state, batch=batch, nblk=1, reverse=True,
                     mode="state")

    x2d = x.reshape(batch * n_lat, d)
    tm = _pick_tile(n_lat, 1024)
    z, xbc, dt, dtT = _inproj_zx_call(
        x2d, sh1, sc1, norm_mix, w_inT, conv_w, conv_b,
        with_z=True, period=GRID_W, tm=tm, rows_per_mod=n_lat)
    y_sc = _inproj_sc_call(x2d, sh1, sc1, norm_mix, w_inT, sc_conv_w[layer], tm=tm, tc=tc,
                           rows_per_mod=n_lat)
    nblk = n_lat // BLK
    y_rev = _ssd_call(xbc, dt, dtT, prm, st_r, batch=batch, nblk=nblk, reverse=True, mode="partial")
    y_ssm = _ssd_call(xbc, dt, dtT, prm, st_f, batch=batch, nblk=nblk, reverse=False, mode="final",
                      extra=(z, y_rev))

    h1, f_lat, aff = _outproj_call(
        y_ssm, y_sc, sc_norm_w[layer][None, :], w_out_b, x2d, g1, norm_ffn_w[layer][None, :],
        sh2, sc2, w_rT, batch=batch, tm=_pick_tile(n_lat, 256))
    idx, gcol = _route_call(aff, cap=cap)
    h2 = _moe_call(idx.reshape(-1), f_lat, gcol, g2, wg, wu, wd, h1, batch=batch, cap=cap)
    out = _final_call(h2, final_norm_w[None, :], tm=_pick_tile(batch * n_lat, 512))
    return out.reshape(batch, n_lat, d)
```

```python
import functools

import jax
import jax.numpy as jnp
from jax import lax
from jax.experimental import pallas as pl
from jax.experimental.pallas import tpu as pltpu

F32 = jnp.float32
BF16 = jnp.bfloat16
I32 = jnp.int32

D_MODEL = 2048
GRID_W = 64
D_SSM = 2048
SSM_HEAD_DIM = 64
SSM_HEADS = D_SSM // SSM_HEAD_DIM
SSM_GROUPS = 4
SSM_HPG = SSM_HEADS // SSM_GROUPS
SSM_STATE = 128
SSD_CHUNK = 128
GN = SSM_GROUPS * SSM_STATE
D_XBC = D_SSM + 2 * GN
D_ZX = D_SSM + D_XBC
N_DT = 2 * SSM_HEADS
D_SSM_IN = D_ZX + N_DT
D_SC = 2048
N_EXPERTS = 16
EC_CAPACITY_FACTOR = 2
EXPERT_FF = 1024
N_MOD = 6
EPS = 1e-6
LOG2E = 1.4426950408889634

LANES = 128
BLK = 2 * SSD_CHUNK
GROUP_W = SSM_HPG * SSM_HEAD_DIM
VMEM_LIMIT = 56 * 1024 * 1024


def _cparams(sem):
    return pltpu.CompilerParams(dimension_semantics=sem, vmem_limit_bytes=VMEM_LIMIT)


def _silu(v):
    return v * jax.nn.sigmoid(v)


def _softplus(v):
    return jnp.maximum(v, 0.0) + jnp.log1p(jnp.exp(-jnp.abs(v)))


def _mod_kernel(c_ref, w_ref, b_ref, o_ref):
    s = _silu(c_ref[...]).astype(BF16)
    o_ref[...] = jnp.dot(s, w_ref[...].astype(BF16), preferred_element_type=F32) + b_ref[...]


def _mod_call(cvec, w_mod, b_mod):
    rows, d = cvec.shape
    n = w_mod.shape[1]
    tn = 1024
    return pl.pallas_call(
        _mod_kernel,
        grid=(n // tn,),
        in_specs=[pl.BlockSpec((rows, d), lambda j: (0, 0)),
                  pl.BlockSpec((d, tn), lambda j: (0, j)),
                  pl.BlockSpec((1, tn), lambda j: (0, j))],
        out_specs=pl.BlockSpec((rows, tn), lambda j: (0, j)),
        out_shape=jax.ShapeDtypeStruct((rows, n), F32),
        compiler_params=_cparams(("arbitrary",)),
        name="mod",
    )(cvec, w_mod, b_mod)


def _dot_nt(a, bt):
    return lax.dot_general(a, bt, (((1,), (1,)), ((), ())), preferred_element_type=F32)


def _norm_modulate(h, nw, shift, scale):
    ms = jnp.mean(h * h, axis=-1, keepdims=True)
    hn = h * lax.rsqrt(ms + EPS) * nw
    return hn * (1.0 + scale) + shift


def _conv3(v, cw, period):
    rows = v.shape[0]
    r = lax.broadcasted_iota(I32, (rows, 1), 0) % period
    prev = jnp.where(r == 0, 0.0, pltpu.roll(v, 1, 0))
    nxt = jnp.where(r == period - 1, 0.0, pltpu.roll(v, rows - 1, 0))
    return prev * cw[0:1] + v * cw[1:2] + nxt * cw[2:3]


def _inproj_zx_kernel(h_ref, sh_ref, sc_ref, nw_ref, w_ref, wdt_ref, cw_ref, cb_ref,
                      *rest, with_z, period, tm, n_z):
    if with_z:
        z_ref, xbc_ref, dt_ref, dtT_ref, a_ref = rest
    else:
        xbc_ref, dt_ref, dtT_ref, a_ref = rest
    n = pl.program_id(1)

    @pl.when(n == 0)
    def _():
        a = _norm_modulate(h_ref[...], nw_ref[...], sh_ref[0], sc_ref[0]).astype(BF16)
        a_ref[...] = a
        wdt = wdt_ref[...]
        dt_ref[...] = _dot_nt(a, wdt)
        dtT_ref[...] = _dot_nt(wdt, a)

    wb = w_ref[...]

    if with_z:
        @pl.when(n < n_z)
        def _():
            for k in range(tm // BLK):
                rows = slice(k * BLK, (k + 1) * BLK)
                z_ref[rows, :] = _dot_nt(a_ref[rows, :], wb).astype(BF16)

    @pl.when(n >= n_z)
    def _():
        for k in range(tm // BLK):
            rows = slice(k * BLK, (k + 1) * BLK)
            acc = _dot_nt(a_ref[rows, :], wb)
            y = _conv3(acc, cw_ref[...], period) + cb_ref[...]
            xbc_ref[rows, :] = _silu(y).astype(BF16)


def _inproj_zx_call(h2d, shift, scale, norm_w, w_inT, conv_w, conv_b, *,
                    with_z, period, tm, rows_per_mod):
    m, d = h2d.shape
    tn = 1024
    n_z = D_SSM // tn if with_z else 0
    n_x = D_XBC // tn
    n_off = 0 if with_z else D_SSM // tn
    per = rows_per_mod // tm
    kern = functools.partial(_inproj_zx_kernel, with_z=with_z, period=period, tm=tm, n_z=n_z)
    out_shape = [jax.ShapeDtypeStruct((m, D_XBC), BF16),
                 jax.ShapeDtypeStruct((m, LANES), F32),
                 jax.ShapeDtypeStruct((LANES, m), F32)]
    out_specs = [pl.BlockSpec((tm, tn), lambda i, n: (i, jnp.maximum(n - n_z, 0))),
                 pl.BlockSpec((tm, LANES), lambda i, n: (i, 0)),
                 pl.BlockSpec((LANES, tm), lambda i, n: (0, i))]
    if with_z:
        out_shape = [jax.ShapeDtypeStruct((m, D_SSM), BF16)] + out_shape
        out_specs = [pl.BlockSpec((tm, tn), lambda i, n: (i, jnp.minimum(n, n_z - 1)))] + out_specs
    return pl.pallas_call(
        kern,
        grid=(m // tm, n_z + n_x),
        in_specs=[pl.BlockSpec((tm, d), lambda i, n: (i, 0)),
                  pl.BlockSpec((1, 1, d), lambda i, n: (i // per, 0, 0)),
                  pl.BlockSpec((1, 1, d), lambda i, n: (i // per, 0, 0)),
                  pl.BlockSpec((1, d), lambda i, n: (0, 0)),
                  pl.BlockSpec((tn, d), lambda i, n: (n + n_off, 0)),
                  pl.BlockSpec((LANES, d), lambda i, n: (D_ZX // LANES, 0)),
                  pl.BlockSpec((3, tn), lambda i, n: (0, jnp.maximum(n - n_z, 0))),
                  pl.BlockSpec((1, tn), lambda i, n: (0, jnp.maximum(n - n_z, 0)))],
        out_specs=out_specs,
        out_shape=out_shape,
        scratch_shapes=[pltpu.VMEM((tm, d), BF16)],
        compiler_params=_cparams(("arbitrary", "arbitrary")),
        name="inproj_zx" if with_z else "inproj_ctx",
    )(h2d, shift, scale, norm_w, w_inT, w_inT, conv_w, conv_b)


def _inproj_sc_kernel(h_ref, sh_ref, sc_ref, nw_ref, wb_ref, wc_ref, wv_ref, cw_ref, o_ref, a_ref,
                      *, tm):
    n = pl.program_id(1)

    @pl.when(n == 0)
    def _():
        a_ref[...] = _norm_modulate(h_ref[...], nw_ref[...], sh_ref[0], sc_ref[0]).astype(BF16)

    wb, wc, wv = wb_ref[...], wc_ref[...], wv_ref[...]
    for k in range(tm // BLK):
        rows = slice(k * BLK, (k + 1) * BLK)
        a = a_ref[rows, :]
        cg = _dot_nt(a, wc)
        hv = _dot_nt(a, wv)
        v = _conv3(cg * hv, cw_ref[...], GRID_W)
        bg = _dot_nt(a, wb)
        o_ref[rows, :] = (bg * v).astype(BF16)


def _inproj_sc_call(h2d, shift, scale, norm_w, w_inT, conv_w, *, tm, tc, rows_per_mod):
    m, d = h2d.shape
    per = rows_per_mod // tm
    kern = functools.partial(_inproj_sc_kernel, tm=tm)

    def wspec(k):
        return pl.BlockSpec((pl.Element(tc), pl.Element(d)),
                            lambda i, n: (pl.multiple_of(D_SSM_IN + k * D_SC + n * tc, 16), 0))

    return pl.pallas_call(
        kern,
        grid=(m // tm, D_SC // tc),
        in_specs=[pl.BlockSpec((tm, d), lambda i, n: (i, 0)),
                  pl.BlockSpec((1, 1, d), lambda i, n: (i // per, 0, 0)),
                  pl.BlockSpec((1, 1, d), lambda i, n: (i // per, 0, 0)),
                  pl.BlockSpec((1, d), lambda i, n: (0, 0)),
                  wspec(0), wspec(1), wspec(2),
                  pl.BlockSpec((3, tc), lambda i, n: (0, n))],
        out_specs=pl.BlockSpec((tm, tc), lambda i, n: (i, n)),
        out_shape=jax.ShapeDtypeStruct((m, D_SC), BF16),
        scratch_shapes=[pltpu.VMEM((tm, d), BF16)],
        compiler_params=_cparams(("arbitrary", "arbitrary")),
        name="inproj_sc",
    )(h2d, shift, scale, norm_w, w_inT, w_inT, w_inT, conv_w)


def _ssd_kernel(*refs, reverse, mode, nblk):
    xbc_ref, dt_ref, dtT_ref, alr_ref, alc_ref, br_ref, bc_ref, e_ref, init_ref = refs[:9]
    rest = refs[9:]
    if mode == "state":
        out_ref, s_ref = rest
    elif mode == "partial":
        out_ref, s_ref = rest
    else:
        z_ref, yo_ref, dsk_ref, nw_ref, out_ref, s_ref = rest
    j = pl.program_id(1)

    @pl.when(j == 0)
    def _():
        s_ref[...] = init_ref[0]

    ii = lax.broadcasted_iota(I32, (SSD_CHUNK, SSD_CHUNK), 0)
    jj = lax.broadcasted_iota(I32, (SSD_CHUNK, SSD_CHUNK), 1)
    tri = jnp.where(jj <= ii, 1.0, 0.0).astype(F32)
    triT = jnp.where(ii <= jj, 1.0, 0.0).astype(F32)
    mask = (jj >= ii) if reverse else (ii >= jj)
    lane4 = lax.broadcasted_iota(I32, (SSD_CHUNK, 4 * SSM_HEAD_DIM), 1) // SSM_HEAD_DIM
    aneg_r = -jnp.exp(alr_ref[...]) * LOG2E
    aneg_c = -jnp.exp(alc_ref[...]) * LOG2E
    lane0 = SSM_HEADS if reverse else 0
    hi = lax.Precision.HIGHEST

    def chunk(c):
        rows = slice(c * SSD_CHUNK, (c + 1) * SSD_CHUNK)
        dt = _softplus(dt_ref[rows, :] + br_ref[...])
        dtT = _softplus(dtT_ref[:, rows] + bc_ref[...])
        la = dt * aneg_r
        laT = dtT * aneg_c
        cs = jnp.dot(tri, la, precision=hi, preferred_element_type=F32)
        csT = jnp.dot(laT, triT, precision=hi, preferred_element_type=F32)
        total = cs[SSD_CHUNK - 1:SSD_CHUNK, :]
        log2_dtT = jnp.log(dtT) * LOG2E
        if reverse:
            ecs = cs - la
            col, rowm = -ecs, csT - laT + log2_dtT
            yscale = jnp.exp2(total - ecs)
            w = dt * jnp.exp2(ecs)
        else:
            col, rowm = cs, log2_dtT - csT
            yscale = jnp.exp2(cs)
            w = dt * jnp.exp2(total - cs)
        t1 = total.astype(BF16).astype(F32)
        r1 = total - t1
        t2 = r1.astype(BF16).astype(F32)
        t3 = r1 - t2
        r16 = lax.broadcasted_iota(I32, (16, LANES), 0)
        tail = jnp.where(r16 == 0, t1, jnp.where(r16 == 1, t2, jnp.where(r16 == 2, t3, 0.0)))
        stack = jnp.concatenate([yscale, w, tail], axis=0).astype(BF16)
        ex = jnp.dot(stack, e_ref[...], preferred_element_type=F32)
        yscale_x = ex[0:SSD_CHUNK]
        w_x = ex[SSD_CHUNK:2 * SSD_CHUNK]
        tb = 2 * SSD_CHUNK
        sdec_x = jnp.exp2(ex[tb:tb + 1] + ex[tb + 1:tb + 2] + ex[tb + 2:tb + 3])

        for g in range(SSM_GROUPS):
            gsl = slice(g * GROUP_W, (g + 1) * GROUP_W)
            xg = xbc_ref[rows, gsl]
            bg = xbc_ref[rows, D_SSM + g * SSM_STATE:D_SSM + (g + 1) * SSM_STATE]
            s_old = s_ref[g]
            if mode != "state":
                cg = xbc_ref[rows, D_SSM + GN + g * SSM_STATE:D_SSM + GN + (g + 1) * SSM_STATE]
                cb = lax.dot_general(cg, bg, (((1,), (1,)), ((), ())), preferred_element_type=F32)
                y_off = jnp.dot(cg, s_old.astype(BF16), preferred_element_type=F32)
                halves = []
                for q in range(2):
                    xq = xg[:, q * 256:(q + 1) * 256]
                    acc = jnp.zeros((SSD_CHUNK, 256), F32)
                    for r4 in range(4):
                        lane = lane0 + g * SSM_HPG + q * 4 + r4
                        seg = col[:, lane:lane + 1] + rowm[lane:lane + 1, :]
                        dec = jnp.exp2(jnp.where(mask, seg, -jnp.inf))
                        mh = (cb * dec).astype(BF16)
                        xm = jnp.where(lane4 == r4, xq, jnp.zeros_like(xq))
                        acc = acc + jnp.dot(mh, xm, preferred_element_type=F32)
                    halves.append(acc)
                y = jnp.concatenate(halves, axis=1) + yscale_x[:, gsl] * y_off
                if mode == "partial":
                    out_ref[rows, gsl] = y.astype(BF16)
                else:
                    y = y + yo_ref[rows, gsl].astype(F32) + dsk_ref[:, gsl] * xg.astype(F32)
                    y = y * _silu(z_ref[rows, gsl].astype(F32))
                    ms = jnp.mean(y * y, axis=-1, keepdims=True)
                    out_ref[rows, gsl] = (y * lax.rsqrt(ms + EPS) * nw_ref[:, gsl]).astype(BF16)
            xw = (xg.astype(F32) * w_x[:, gsl]).astype(BF16)
            upd = lax.dot_general(bg, xw, (((0,), (0,)), ((), ())), preferred_element_type=F32)
            s_ref[g] = sdec_x[:, gsl] * s_old + upd

    for c in ((1, 0) if reverse else (0, 1)):
        chunk(c)

    if mode == "state":
        @pl.when(j == nblk - 1)
        def _():
            out_ref[0] = s_ref[...]


def _ssd_call(xbc, dt, dtT, prm, init, *, batch, nblk, reverse, mode, extra=()):
    m = xbc.shape[0]

    def blk(b, j):
        return b * nblk + (nblk - 1 - j if reverse else j)

    state_spec = pl.BlockSpec((1, SSM_GROUPS, SSM_STATE, GROUP_W), lambda b, j: (b, 0, 0, 0))
    const2 = lambda b, j: (0, 0)
    in_specs = [pl.BlockSpec((BLK, D_XBC), lambda b, j: (blk(b, j), 0)),
                pl.BlockSpec((BLK, LANES), lambda b, j: (blk(b, j), 0)),
                pl.BlockSpec((LANES, BLK), lambda b, j: (0, blk(b, j))),
                pl.BlockSpec((1, LANES), const2), pl.BlockSpec((LANES, 1), const2),
                pl.BlockSpec((1, LANES), const2), pl.BlockSpec((LANES, 1), const2),
                pl.BlockSpec((LANES, D_SSM), const2),
                state_spec]
    args = [xbc, dt, dtT, prm["alog_row"], prm["alog_col"], prm["bias_row"], prm["bias_col"],
            prm["expand_rev" if reverse else "expand_fwd"], init]
    if mode == "state":
        out_specs = state_spec
        out_shape = jax.ShapeDtypeStruct((batch, SSM_GROUPS, SSM_STATE, GROUP_W), F32)
    else:
        out_specs = pl.BlockSpec((BLK, D_SSM), lambda b, j: (blk(b, j), 0))
        out_shape = jax.ShapeDtypeStruct((m, D_SSM), BF16)
    if mode == "final":
        z, y_other = extra
        in_specs += [pl.BlockSpec((BLK, D_SSM), lambda b, j: (blk(b, j), 0)),
                     pl.BlockSpec((BLK, D_SSM), lambda b, j: (blk(b, j), 0)),
                     pl.BlockSpec((1, D_SSM), const2), pl.BlockSpec((1, D_SSM), const2)]
        args += [z, y_other, prm["dskip"], prm["norm_w"]]
    kern = functools.partial(_ssd_kernel, reverse=reverse, mode=mode, nblk=nblk)
    return pl.pallas_call(
        kern,
        grid=(batch, nblk),
        in_specs=in_specs,
        out_specs=out_specs,
        out_shape=out_shape,
        scratch_shapes=[pltpu.VMEM((SSM_GROUPS, SSM_STATE, GROUP_W), F32)],
        compiler_params=_cparams(("arbitrary", "arbitrary")),
        name=f"ssd_{mode}_{'rev' if reverse else 'fwd'}",
    )(*args)


def _outproj_kernel(ys_ref, yc_ref, scw_ref, w_ref, x_ref, g1_ref, nw_ref, sh_ref, sc_ref, wr_ref,
                    h_ref, f_ref, aff_ref, *, tm, sub):
    for k in range(tm // sub):
        rows = slice(k * sub, (k + 1) * sub)
        yc = yc_ref[rows, :].astype(F32)
        ms = jnp.mean(yc * yc, axis=-1, keepdims=True)
        ycn = (yc * lax.rsqrt(ms + EPS) * scw_ref[...]).astype(BF16)
        mix = jnp.dot(ys_ref[rows, :], w_ref[0:D_SSM, :], preferred_element_type=F32)
        mix = mix + jnp.dot(ycn, w_ref[D_SSM:, :], preferred_element_type=F32)
        h = x_ref[rows, :] + g1_ref[0] * mix
        h_ref[rows, :] = h
        f = _norm_modulate(h, nw_ref[...], sh_ref[0], sc_ref[0])
        f_ref[rows, :] = f
        logits = _dot_nt(wr_ref[...], f.astype(BF16))
        mx = jnp.max(logits, axis=0, keepdims=True)
        ex = jnp.exp(logits - mx)
        aff_ref[0, :, rows] = ex / jnp.sum(ex, axis=0, keepdims=True)


def _outproj_call(y_ssm, y_sc, sc_norm_w, w_out, x2d, g1, norm_w, shift, scale, w_rT, *, batch, tm):
    m, d = x2d.shape
    n_tok = m // batch
    per = n_tok // tm
    vec = lambda i: (i // per, 0, 0)
    const2 = lambda i: (0, 0)
    sub = min(tm, 256)
    return pl.pallas_call(
        functools.partial(_outproj_kernel, tm=tm, sub=sub),
        grid=(m // tm,),
        in_specs=[pl.BlockSpec((tm, D_SSM), lambda i: (i, 0)),
                  pl.BlockSpec((tm, D_SC), lambda i: (i, 0)),
                  pl.BlockSpec((1, D_SC), const2),
                  pl.BlockSpec((D_SSM + D_SC, d), const2, pipeline_mode=pl.Buffered(1)),
                  pl.BlockSpec((tm, d), lambda i: (i, 0)),
                  pl.BlockSpec((1, 1, d), vec),
                  pl.BlockSpec((1, d), const2),
                  pl.BlockSpec((1, 1, d), vec),
                  pl.BlockSpec((1, 1, d), vec),
                  pl.BlockSpec((N_EXPERTS, d), const2)],
        out_specs=[pl.BlockSpec((tm, d), lambda i: (i, 0)),
                   pl.BlockSpec((tm, d), lambda i: (i, 0)),
                   pl.BlockSpec((1, N_EXPERTS, tm), lambda i: (i // per, 0, i % per))],
        out_shape=[jax.ShapeDtypeStruct((m, d), F32),
                   jax.ShapeDtypeStruct((m, d), F32),
                   jax.ShapeDtypeStruct((batch, N_EXPERTS, n_tok), F32)],
        compiler_params=_cparams(("arbitrary",)),
        name="outproj",
    )(y_ssm, y_sc, sc_norm_w, w_out, x2d, g1, norm_w, shift, scale, w_rT)


def _prefix_lanes(m01):
    rows, n = m01.shape
    ki = lax.broadcasted_iota(I32, (LANES, LANES), 0)
    ci = lax.broadcasted_iota(I32, (LANES, LANES), 1)
    triu = jnp.where(ki <= ci, 1.0, 0.0).astype(BF16)
    off = jnp.zeros((rows, 1), F32)
    parts = []
    for k in range(n // LANES):
        p = jnp.dot(m01[:, k * LANES:(k + 1) * LANES].astype(BF16), triu,
                    preferred_element_type=F32)
        parts.append(p + off)
        off = off + p[:, LANES - 1:LANES]
    return jnp.concatenate(parts, axis=1)


def _route_kernel(aff_ref, idx_ref, gcol_ref, *, cap):
    a = aff_ref[0]
    n_e, n = a.shape
    thr = jnp.zeros((n_e, 1), I32)
    for bit in range(30, -1, -1):
        cand = thr | (1 << bit)
        cand_f = lax.bitcast_convert_type(cand, F32)
        cnt = jnp.sum(jnp.where(a >= cand_f, 1.0, 0.0), axis=1, keepdims=True)
        thr = jnp.where(cnt >= cap, cand, thr)
    thr_f = lax.bitcast_convert_type(thr, F32)
    gt = a > thr_f
    eq = a == thr_f
    need = cap - jnp.sum(jnp.where(gt, 1.0, 0.0), axis=1, keepdims=True)
    eqf = jnp.where(eq, 1.0, 0.0)
    ties_before = _prefix_lanes(eqf) - eqf
    sel = jnp.logical_or(gt, jnp.logical_and(eq, ties_before < need))
    self_ = jnp.where(sel, 1.0, 0.0)
    slot = jnp.where(sel, _prefix_lanes(self_) - 1.0, -1.0).astype(I32)

    tok = lax.broadcasted_iota(I32, (1, n), 1)
    t_hi = (tok // 64).astype(F32)
    t_lo = (tok % 64).astype(F32)
    a1 = a.astype(BF16).astype(F32)
    ra = a - a1
    a2 = ra.astype(BF16).astype(F32)
    a3 = ra - a2
    srow = lax.broadcasted_iota(I32, (cap, n), 0)
    r8 = lax.broadcasted_iota(I32, (8, n), 0)
    zpad = jnp.zeros((LANES - 8, n), F32)
    for e in range(n_e):
        onehot = jnp.where(slot[e:e + 1, :] == srow, 1.0, 0.0).astype(BF16)
        v8 = jnp.where(r8 == 0, t_hi, jnp.where(r8 == 1, t_lo, jnp.where(
            r8 == 2, a1[e:e + 1], jnp.where(r8 == 3, a2[e:e + 1], jnp.where(
                r8 == 4, a3[e:e + 1], 0.0)))))
        vals = jnp.concatenate([v8, zpad], axis=0).astype(BF16)
        o_col = lax.dot_general(onehot, vals, (((1,), (1,)), ((), ())),
                                preferred_element_type=F32)
        o_row = o_col.T
        idx_ref[0, e:e + 1, :] = (o_row[0:1] * 64.0 + o_row[1:2]).astype(I32)
        gcol_ref[0, e] = o_col


def _route_call(aff, *, cap):
    batch, n_e, n = aff.shape
    return pl.pallas_call(
        functools.partial(_route_kernel, cap=cap),
        grid=(batch,),
        in_specs=[pl.BlockSpec((1, n_e, n), lambda b: (b, 0, 0))],
        out_specs=[pl.BlockSpec((1, n_e, cap), lambda b: (b, 0, 0)),
                   pl.BlockSpec((1, n_e, cap, LANES), lambda b: (b, 0, 0, 0))],
        out_shape=[jax.ShapeDtypeStruct((batch, n_e, cap), I32),
                   jax.ShapeDtypeStruct((batch, n_e, cap, LANES), F32)],
        compiler_params=_cparams(("arbitrary",)),
        name="route",
    )(aff)


MOE_PARTS = 4
MOE_NCHUNK = 512


def _moe_kernel(idx_ref, f_hbm, gcol_ref, g2_ref, wg_ref, wu_ref, wd_ref, hin_hbm, hout_hbm,
                xb0, xb1, hb0, hb1, sem, *, cap, n_tok, n_pairs):
    e = pl.program_id(0)
    i = pl.program_id(1)
    first = jnp.logical_and(e == 0, i == 0)
    last = jnp.logical_and(e == pl.num_programs(0) - 1, i == n_pairs - 1)
    i_next = jnp.where(i == n_pairs - 1, 0, i + 1)
    e_next = jnp.where(i == n_pairs - 1, e + 1, e)
    xbufs, hbufs = (xb0, xb1), (hb0, hb1)
    part = cap // MOE_PARTS

    def row_of(ex, b, jrow):
        return b * n_tok + idx_ref[(b * N_EXPERTS + ex) * cap + jrow]

    def gather_start(ex, b, s, jrow):
        row = row_of(ex, b, jrow)
        pltpu.make_async_copy(f_hbm.at[pl.ds(row, 1)], xbufs[s].at[pl.ds(jrow, 1)],
                              sem.at[s]).start()
        pltpu.make_async_copy(hin_hbm.at[pl.ds(row, 1)], hbufs[s].at[pl.ds(jrow, 1)],
                              sem.at[2 + s]).start()

    def scatter_start(ex, b, s, jrow):
        pltpu.make_async_copy(hbufs[s].at[pl.ds(jrow, 1)],
                              hout_hbm.at[pl.ds(row_of(ex, b, jrow), 1)], sem.at[4 + s]).start()

    def wait_gather(s):
        pltpu.make_async_copy(f_hbm.at[pl.ds(0, cap)], xbufs[s], sem.at[s]).wait()
        pltpu.make_async_copy(hin_hbm.at[pl.ds(0, cap)], hbufs[s], sem.at[2 + s]).wait()

    def wait_scatter(s):
        pltpu.make_async_copy(hbufs[s], hout_hbm.at[pl.ds(0, cap)], sem.at[4 + s]).wait()

    def compute_part(s, b_local, q, between):
        rows = slice(q * part, (q + 1) * part)
        xq = xbufs[s][rows, :].astype(BF16)
        gc = gcol_ref[b_local, 0, rows, :]
        gate = gc[:, 2:3] + gc[:, 3:4] + gc[:, 4:5]
        g2 = g2_ref[2 * i + b_local]
        nc = MOE_NCHUNK
        hid = []
        for c in range(EXPERT_FF // nc):
            cols = slice(c * nc, (c + 1) * nc)
            between[2 * c]()
            hg = jnp.dot(xq, wg_ref[0, :, cols], preferred_element_type=F32)
            between[2 * c + 1]()
            hu = jnp.dot(xq, wu_ref[0, :, cols], preferred_element_type=F32)
            hid.append((_silu(hg) * hu).astype(BF16))
        hid = jnp.concatenate(hid, axis=1)
        k0 = 2 * (EXPERT_FF // nc)
        for c in range(D_MODEL // nc):
            cols = slice(c * nc, (c + 1) * nc)
            between[k0 + c]()
            ye = jnp.dot(hid, wd_ref[0, :, cols], preferred_element_type=F32)
            hbufs[s][rows, cols] = hbufs[s][rows, cols] + g2[:, cols] * (ye * gate)

    n_between = 2 * (EXPERT_FF // MOE_NCHUNK) + D_MODEL // MOE_NCHUNK

    def spread(jobs):
        n = len(jobs)
        cuts = [n * k // n_between for k in range(n_between + 1)]

        def group(k):
            def run():
                for job in jobs[cuts[k]:cuts[k + 1]]:
                    job()
            return run
        return [group(k) for k in range(n_between)]

    b_a, b_b = 2 * i, 2 * i + 1
    b_n = 2 * i_next

    @pl.when(first)
    def _():
        @pl.loop(0, cap)
        def _(jrow):
            gather_start(e, b_a, 0, jrow)

    wait_gather(0)
    for q in range(MOE_PARTS):
        jobs = [functools.partial(gather_start, e, b_b, 1, jrow)
                for jrow in range(q * part, (q + 1) * part)]
        if q > 0:
            jobs += [functools.partial(scatter_start, e, b_a, 0, jrow)
                     for jrow in range((q - 1) * part, q * part)]
        compute_part(0, 0, q, spread(jobs))

    wait_gather(1)
    rows_next = [cap * k // (MOE_PARTS - 1) for k in range(MOE_PARTS)]
    for q in range(MOE_PARTS):
        if q == 0:
            jobs = [functools.partial(scatter_start, e, b_a, 0, jrow)
                    for jrow in range((MOE_PARTS - 1) * part, cap)]
        else:
            jobs = [functools.partial(scatter_start, e, b_b, 1, jrow)
                    for jrow in range((q - 1) * part, q * part)]
            jobs += [functools.partial(gather_start, e_next, b_n, 0, jrow)
                     for jrow in range(rows_next[q - 1], rows_next[q])]
        if q == 1:
            wait_scatter(0)
        compute_part(1, 1, q, spread(jobs))

    for jrow in range((MOE_PARTS - 1) * part, cap):
        scatter_start(e, b_b, 1, jrow)
    wait_scatter(1)

    @pl.when(last)
    def _():
        wait_gather(0)


def _moe_call(idx_flat, f2d, gcol, g2, wg, wu, wd, h2d, *, batch, cap):
    m, d = h2d.shape
    n_tok = m // batch
    assert batch % 2 == 0 and cap % MOE_PARTS == 0
    n_pairs = batch // 2
    kern = functools.partial(_moe_kernel, cap=cap, n_tok=n_tok, n_pairs=n_pairs)
    grid_spec = pltpu.PrefetchScalarGridSpec(
        num_scalar_prefetch=1,
        grid=(N_EXPERTS, n_pairs),
        in_specs=[pl.BlockSpec(memory_space=pl.ANY),
                  pl.BlockSpec((2, 1, cap, LANES), lambda e, i, idx: (i, e, 0, 0)),
                  pl.BlockSpec((batch, 1, d), lambda e, i, idx: (0, 0, 0)),
                  pl.BlockSpec((1, d, EXPERT_FF), lambda e, i, idx: (e, 0, 0)),
                  pl.BlockSpec((1, d, EXPERT_FF), lambda e, i, idx: (e, 0, 0)),
                  pl.BlockSpec((1, EXPERT_FF, d), lambda e, i, idx: (e, 0, 0)),
                  pl.BlockSpec(memory_space=pl.ANY)],
        out_specs=pl.BlockSpec(memory_space=pl.ANY),
        scratch_shapes=[pltpu.VMEM((cap, d), F32), pltpu.VMEM((cap, d), F32),
                        pltpu.VMEM((cap, d), F32), pltpu.VMEM((cap, d), F32),
                        pltpu.SemaphoreType.DMA((6,))],
    )
    return pl.pallas_call(
        kern,
        grid_spec=grid_spec,
        out_shape=jax.ShapeDtypeStruct((m, d), F32),
        input_output_aliases={7: 0},
        compiler_params=_cparams(("arbitrary", "arbitrary")),
        name="moe",
    )(idx_flat, f2d, gcol, g2, wg, wu, wd, h2d)


def _final_kernel(h_ref, w_ref, o_ref):
    h = h_ref[...]
    ms = jnp.mean(h * h, axis=-1, keepdims=True)
    o_ref[...] = h * lax.rsqrt(ms + EPS) * w_ref[...]


def _final_call(h2d, w, *, tm):
    m, d = h2d.shape
    return pl.pallas_call(
        _final_kernel,
        grid=(m // tm,),
        in_specs=[pl.BlockSpec((tm, d), lambda i: (i, 0)), pl.BlockSpec((1, d), lambda i: (0, 0))],
        out_specs=pl.BlockSpec((tm, d), lambda i: (i, 0)),
        out_shape=jax.ShapeDtypeStruct((m, d), F32),
        compiler_params=_cparams(("arbitrary",)),
        name="final_norm",
    )(h2d, w)


def _pick_tile(n, pref):
    t = pref
    while n % t:
        t //= 2
    return t


def _ssd_params(a_log, dt_bias, d_skip, norm_w):
    padw = LANES - N_DT
    alog = jnp.pad(a_log.astype(F32), (0, padw))
    bias = jnp.pad(dt_bias.astype(F32), (0, padw))
    lane = jnp.arange(LANES)[:, None]
    head = jnp.arange(D_SSM)[None, :] // SSM_HEAD_DIM
    return {
        "alog_row": alog[None, :], "alog_col": alog[:, None],
        "bias_row": bias[None, :], "bias_col": bias[:, None],
        "expand_fwd": (lane == head).astype(BF16),
        "expand_rev": (lane == head + SSM_HEADS).astype(BF16),
        "dskip": jnp.repeat(d_skip.astype(F32), SSM_HEAD_DIM)[None, :],
        "norm_w": norm_w.astype(F32)[None, :],
    }


def kernel(x, c, ctx, c_ctx, w_mod, b_mod, norm_mix_w, w_in, ssm_conv_w, ssm_conv_b, ssm_a_log,
           ssm_dt_bias, ssm_d, ssm_norm_w, sc_conv_w, sc_norm_w, w_out, norm_ffn_w, w_router,
           w_gate, w_up, w_down, final_norm_w):
    batch, n_lat, d = x.shape
    n_ctx = ctx.shape[1]
    depth = w_mod.shape[0]
    assert depth == 1 and d == D_MODEL and n_ctx == BLK and n_lat % BLK == 0
    layer = 0
    cap = EC_CAPACITY_FACTOR * n_lat // N_EXPERTS

    pad_rows = -(batch + 1) % 8
    cvec = jnp.concatenate([c, c_ctx[None, :], jnp.zeros((pad_rows, d), F32)], axis=0)
    mod = _mod_call(cvec, w_mod[layer], b_mod[layer][None, :])
    sh1, sc1, g1, sh2, sc2, g2 = [mod[:batch, k * d:(k + 1) * d][:, None, :] for k in range(N_MOD)]
    sh1_c = mod[batch:batch + 1, 0:d][:, None, :]
    sc1_c = mod[batch:batch + 1, d:2 * d][:, None, :]

    w_inT = w_in[layer].T.astype(BF16)
    tc = 512
    w_out_b = w_out[layer].astype(BF16)
    w_rT = w_router[layer].T.astype(BF16)
    wg, wu, wd = w_gate[layer].astype(BF16), w_up[layer].astype(BF16), w_down[layer].astype(BF16)
    norm_mix = norm_mix_w[layer][None, :]
    conv_w, conv_b = ssm_conv_w[layer], ssm_conv_b[layer][None, :]
    prm = _ssd_params(ssm_a_log[layer], ssm_dt_bias[layer], ssm_d[layer], ssm_norm_w[layer])

    ctx2d = ctx.reshape(batch * n_ctx, d)
    xbc_c, dt_c, dtT_c = _inproj_zx_call(
        ctx2d, sh1_c, sc1_c, norm_mix, w_inT, conv_w, conv_b,
        with_z=False, period=n_ctx, tm=_pick_tile(batch * n_ctx, 1024), rows_per_mod=batch * n_ctx)
    zero_state = jnp.zeros((batch, SSM_GROUPS, SSM_STATE, GROUP_W), F32)
    st_f = _ssd_call(xbc_c, dt_c, dtT_c, prm, zero_state, batch=batch, nblk=1, reverse=False,
                     mode="state")
    st_r = _ssd_call(xbc_c, dt_c, dtT_c, prm, zero_state, batch=batch, nblk=1, reverse=True,
                     mode="state")

    x2d = x.reshape(batch * n_lat, d)
    tm = _pick_tile(n_lat, 1024)
    z, xbc, dt, dtT = _inproj_zx_call(
        x2d, sh1, sc1, norm_mix, w_inT, conv_w, conv_b,
        with_z=True, period=GRID_W, tm=tm, rows_per_mod=n_lat)
    y_sc = _inproj_sc_call(x2d, sh1, sc1, norm_mix, w_inT, sc_conv_w[layer], tm=tm, tc=tc,
                           rows_per_mod=n_lat)
    nblk = n_lat // BLK
    y_rev = _ssd_call(xbc, dt, dtT, prm, st_r, batch=batch, nblk=nblk, reverse=True, mode="partial")
    y_ssm = _ssd_call(xbc, dt, dtT, prm, st_f, batch=batch, nblk=nblk, reverse=False, mode="final",
                      extra=(z, y_rev))

    h1, f_lat, aff = _outproj_call(
        y_ssm, y_sc, sc_norm_w[layer][None, :], w_out_b, x2d, g1, norm_ffn_w[layer][None, :],
        sh2, sc2, w_rT, batch=batch, tm=_pick_tile(n_lat, 512))
    idx, gcol = _route_call(aff, cap=cap)
    h2 = _moe_call(idx.reshape(-1), f_lat, gcol, g2, wg, wu, wd, h1, batch=batch, cap=cap)
    out = _final_call(h2, final_norm_w[None, :], tm=_pick_tile(batch * n_lat, 512))
    return out.reshape(batch, n_lat, d)
```

```python
import functools

import jax
import jax.numpy as jnp
from jax import lax
from jax.experimental import pallas as pl
from jax.experimental.pallas import tpu as pltpu

F32 = jnp.float32
BF16 = jnp.bfloat16
I32 = jnp.int32

D_MODEL = 2048
GRID_W = 64
D_SSM = 2048
SSM_HEAD_DIM = 64
SSM_HEADS = D_SSM // SSM_HEAD_DIM
SSM_GROUPS = 4
SSM_HPG = SSM_HEADS // SSM_GROUPS
SSM_STATE = 128
SSD_CHUNK = 128
GN = SSM_GROUPS * SSM_STATE
D_XBC = D_SSM + 2 * GN
D_ZX = D_SSM + D_XBC
N_DT = 2 * SSM_HEADS
D_SSM_IN = D_ZX + N_DT
D_SC = 2048
N_EXPERTS = 16
EC_CAPACITY_FACTOR = 2
EXPERT_FF = 1024
N_MOD = 6
EPS = 1e-6
LOG2E = 1.4426950408889634

LANES = 128
BLK = 2 * SSD_CHUNK
GROUP_W = SSM_HPG * SSM_HEAD_DIM
VMEM_LIMIT = 56 * 1024 * 1024


def _cparams(sem):
    return pltpu.CompilerParams(dimension_semantics=sem, vmem_limit_bytes=VMEM_LIMIT)


def _silu(v):
    return v * jax.nn.sigmoid(v)


def _softplus(v):
    return jnp.maximum(v, 0.0) + jnp.log1p(jnp.exp(-jnp.abs(v)))


def _mod_kernel(c_ref, w_ref, b_ref, o_ref):
    s = _silu(c_ref[...]).astype(BF16)
    o_ref[...] = jnp.dot(s, w_ref[...].astype(BF16), preferred_element_type=F32) + b_ref[...]


def _mod_call(cvec, w_mod, b_mod):
    rows, d = cvec.shape
    n = w_mod.shape[1]
    tn = 1024
    return pl.pallas_call(
        _mod_kernel,
        grid=(n // tn,),
        in_specs=[pl.BlockSpec((rows, d), lambda j: (0, 0)),
                  pl.BlockSpec((d, tn), lambda j: (0, j)),
                  pl.BlockSpec((1, tn), lambda j: (0, j))],
        out_specs=pl.BlockSpec((rows, tn), lambda j: (0, j)),
        out_shape=jax.ShapeDtypeStruct((rows, n), F32),
        compiler_params=_cparams(("arbitrary",)),
        name="mod",
    )(cvec, w_mod, b_mod)


def _dot_nt(a, bt):
    return lax.dot_general(a, bt, (((1,), (1,)), ((), ())), preferred_element_type=F32)


def _norm_modulate(h, nw, shift, scale):
    ms = jnp.mean(h * h, axis=-1, keepdims=True)
    hn = h * lax.rsqrt(ms + EPS) * nw
    return hn * (1.0 + scale) + shift


def _conv3(v, cw, period):
    rows = v.shape[0]
    r = lax.broadcasted_iota(I32, (rows, 1), 0) % period
    prev = jnp.where(r == 0, 0.0, pltpu.roll(v, 1, 0))
    nxt = jnp.where(r == period - 1, 0.0, pltpu.roll(v, rows - 1, 0))
    return prev * cw[0:1] + v * cw[1:2] + nxt * cw[2:3]


def _inproj_zx_kernel(h_ref, sh_ref, sc_ref, nw_ref, w_ref, wdt_ref, cw_ref, cb_ref,
                      *rest, with_z, period, tm, n_z):
    if with_z:
        z_ref, xbc_ref, dt_ref, dtT_ref, ao_ref, a_ref = rest
    else:
        xbc_ref, dt_ref, dtT_ref, a_ref = rest
    n = pl.program_id(1)

    @pl.when(n == 0)
    def _():
        a = _norm_modulate(h_ref[...], nw_ref[...], sh_ref[0], sc_ref[0]).astype(BF16)
        a_ref[...] = a
        if with_z:
            ao_ref[...] = a
        wdt = wdt_ref[...]
        dt_ref[...] = _dot_nt(a, wdt)
        dtT_ref[...] = _dot_nt(wdt, a)

    wb = w_ref[...]

    if with_z:
        @pl.when(n < n_z)
        def _():
            for k in range(tm // BLK):
                rows = slice(k * BLK, (k + 1) * BLK)
                z_ref[rows, :] = _dot_nt(a_ref[rows, :], wb).astype(BF16)

    @pl.when(n >= n_z)
    def _():
        for k in range(tm // BLK):
            rows = slice(k * BLK, (k + 1) * BLK)
            acc = _dot_nt(a_ref[rows, :], wb)
            y = _conv3(acc, cw_ref[...], period) + cb_ref[...]
            xbc_ref[rows, :] = _silu(y).astype(BF16)


def _inproj_zx_call(h2d, shift, scale, norm_w, w_inT, conv_w, conv_b, *,
                    with_z, period, tm, rows_per_mod):
    m, d = h2d.shape
    tn = 1024
    n_z = D_SSM // tn if with_z else 0
    n_x = D_XBC // tn
    n_off = 0 if with_z else D_SSM // tn
    per = rows_per_mod // tm
    kern = functools.partial(_inproj_zx_kernel, with_z=with_z, period=period, tm=tm, n_z=n_z)
    out_shape = [jax.ShapeDtypeStruct((m, D_XBC), BF16),
                 jax.ShapeDtypeStruct((m, LANES), F32),
                 jax.ShapeDtypeStruct((LANES, m), F32)]
    out_specs = [pl.BlockSpec((tm, tn), lambda i, n: (i, jnp.maximum(n - n_z, 0))),
                 pl.BlockSpec((tm, LANES), lambda i, n: (i, 0)),
                 pl.BlockSpec((LANES, tm), lambda i, n: (0, i))]
    if with_z:
        out_shape = ([jax.ShapeDtypeStruct((m, D_SSM), BF16)] + out_shape
                     + [jax.ShapeDtypeStruct((m, d), BF16)])
        out_specs = ([pl.BlockSpec((tm, tn), lambda i, n: (i, jnp.minimum(n, n_z - 1)))] + out_specs
                     + [pl.BlockSpec((tm, d), lambda i, n: (i, 0))])
    return pl.pallas_call(
        kern,
        grid=(m // tm, n_z + n_x),
        in_specs=[pl.BlockSpec((tm, d), lambda i, n: (i, 0)),
                  pl.BlockSpec((1, 1, d), lambda i, n: (i // per, 0, 0)),
                  pl.BlockSpec((1, 1, d), lambda i, n: (i // per, 0, 0)),
                  pl.BlockSpec((1, d), lambda i, n: (0, 0)),
                  pl.BlockSpec((tn, d), lambda i, n: (n + n_off, 0)),
                  pl.BlockSpec((LANES, d), lambda i, n: (D_ZX // LANES, 0)),
                  pl.BlockSpec((3, tn), lambda i, n: (0, jnp.maximum(n - n_z, 0))),
                  pl.BlockSpec((1, tn), lambda i, n: (0, jnp.maximum(n - n_z, 0)))],
        out_specs=out_specs,
        out_shape=out_shape,
        scratch_shapes=[pltpu.VMEM((tm, d), BF16)],
        compiler_params=_cparams(("arbitrary", "arbitrary")),
        name="inproj_zx" if with_z else "inproj_ctx",
    )(h2d, shift, scale, norm_w, w_inT, w_inT, conv_w, conv_b)


def _inproj_sc_kernel(a_ref, wb_ref, wc_ref, wv_ref, cw_ref, o_ref, *, tm):
    wb, wc, wv = wb_ref[...], wc_ref[...], wv_ref[...]
    for k in range(tm // BLK):
        rows = slice(k * BLK, (k + 1) * BLK)
        a = a_ref[rows, :]
        cg = _dot_nt(a, wc)
        hv = _dot_nt(a, wv)
        v = _conv3(cg * hv, cw_ref[...], GRID_W)
        bg = _dot_nt(a, wb)
        o_ref[rows, :] = (bg * v).astype(BF16)


def _inproj_sc_call(a2d, w_inT, conv_w, *, tm, tc):
    m, d = a2d.shape
    kern = functools.partial(_inproj_sc_kernel, tm=tm)

    def wspec(k):
        return pl.BlockSpec((pl.Element(tc), pl.Element(d)),
                            lambda i, n: (pl.multiple_of(D_SSM_IN + k * D_SC + n * tc, 16), 0))

    return pl.pallas_call(
        kern,
        grid=(m // tm, D_SC // tc),
        in_specs=[pl.BlockSpec((tm, d), lambda i, n: (i, 0)),
                  wspec(0), wspec(1), wspec(2),
                  pl.BlockSpec((3, tc), lambda i, n: (0, n))],
        out_specs=pl.BlockSpec((tm, tc), lambda i, n: (i, n)),
        out_shape=jax.ShapeDtypeStruct((m, D_SC), BF16),
        compiler_params=_cparams(("arbitrary", "arbitrary")),
        name="inproj_sc",
    )(a2d, w_inT, w_inT, w_inT, conv_w)


def _ssd_kernel(*refs, reverse, mode, nblk):
    xbc_ref, dt_ref, dtT_ref, alr_ref, alc_ref, br_ref, bc_ref, e_ref, init_ref = refs[:9]
    rest = refs[9:]
    if mode == "state":
        out_ref, s_ref = rest
    elif mode == "partial":
        out_ref, s_ref = rest
    else:
        z_ref, yo_ref, dsk_ref, nw_ref, out_ref, s_ref = rest
    j = pl.program_id(1)

    @pl.when(j == 0)
    def _():
        s_ref[...] = init_ref[0]

    ii = lax.broadcasted_iota(I32, (SSD_CHUNK, SSD_CHUNK), 0)
    jj = lax.broadcasted_iota(I32, (SSD_CHUNK, SSD_CHUNK), 1)
    tri = jnp.where(jj <= ii, 1.0, 0.0).astype(F32)
    triT = jnp.where(ii <= jj, 1.0, 0.0).astype(F32)
    mask = (jj >= ii) if reverse else (ii >= jj)
    lane4 = lax.broadcasted_iota(I32, (SSD_CHUNK, 4 * SSM_HEAD_DIM), 1) // SSM_HEAD_DIM
    aneg_r = -jnp.exp(alr_ref[...]) * LOG2E
    aneg_c = -jnp.exp(alc_ref[...]) * LOG2E
    lane0 = SSM_HEADS if reverse else 0
    hi = lax.Precision.HIGHEST

    def chunk(c):
        rows = slice(c * SSD_CHUNK, (c + 1) * SSD_CHUNK)
        dt = _softplus(dt_ref[rows, :] + br_ref[...])
        dtT = _softplus(dtT_ref[:, rows] + bc_ref[...])
        la = dt * aneg_r
        laT = dtT * aneg_c
        cs = jnp.dot(tri, la, precision=hi, preferred_element_type=F32)
        csT = jnp.dot(laT, triT, precision=hi, preferred_element_type=F32)
        total = cs[SSD_CHUNK - 1:SSD_CHUNK, :]
        log2_dtT = jnp.log(dtT) * LOG2E
        if reverse:
            ecs = cs - la
            col, rowm = -ecs, csT - laT + log2_dtT
            yscale = jnp.exp2(total - ecs)
            w = dt * jnp.exp2(ecs)
        else:
            col, rowm = cs, log2_dtT - csT
            yscale = jnp.exp2(cs)
            w = dt * jnp.exp2(total - cs)
        t1 = total.astype(BF16).astype(F32)
        r1 = total - t1
        t2 = r1.astype(BF16).astype(F32)
        t3 = r1 - t2
        r16 = lax.broadcasted_iota(I32, (16, LANES), 0)
        tail = jnp.where(r16 == 0, t1, jnp.where(r16 == 1, t2, jnp.where(r16 == 2, t3, 0.0)))
        stack = jnp.concatenate([yscale, w, tail], axis=0).astype(BF16)
        ex = jnp.dot(stack, e_ref[...], preferred_element_type=F32)
        yscale_x = ex[0:SSD_CHUNK]
        w_x = ex[SSD_CHUNK:2 * SSD_CHUNK]
        tb = 2 * SSD_CHUNK
        sdec_x = jnp.exp2(ex[tb:tb + 1] + ex[tb + 1:tb + 2] + ex[tb + 2:tb + 3])

        for g in range(SSM_GROUPS):
            gsl = slice(g * GROUP_W, (g + 1) * GROUP_W)
            xg = xbc_ref[rows, gsl]
            bg = xbc_ref[rows, D_SSM + g * SSM_STATE:D_SSM + (g + 1) * SSM_STATE]
            s_old = s_ref[g]
            if mode != "state":
                cg = xbc_ref[rows, D_SSM + GN + g * SSM_STATE:D_SSM + GN + (g + 1) * SSM_STATE]
                cb = lax.dot_general(cg, bg, (((1,), (1,)), ((), ())), preferred_element_type=F32)
                y_off = jnp.dot(cg, s_old.astype(BF16), preferred_element_type=F32)
                halves = []
                for q in range(2):
                    xq = xg[:, q * 256:(q + 1) * 256]
                    acc = jnp.zeros((SSD_CHUNK, 256), F32)
                    for r4 in range(4):
                        lane = lane0 + g * SSM_HPG + q * 4 + r4
                        seg = col[:, lane:lane + 1] + rowm[lane:lane + 1, :]
                        dec = jnp.exp2(jnp.where(mask, seg, -jnp.inf))
                        mh = (cb * dec).astype(BF16)
                        xm = jnp.where(lane4 == r4, xq, jnp.zeros_like(xq))
                        acc = acc + jnp.dot(mh, xm, preferred_element_type=F32)
                    halves.append(acc)
                y = jnp.concatenate(halves, axis=1) + yscale_x[:, gsl] * y_off
                if mode == "partial":
                    out_ref[rows, gsl] = y.astype(BF16)
                else:
                    y = y + yo_ref[rows, gsl].astype(F32) + dsk_ref[:, gsl] * xg.astype(F32)
                    y = y * _silu(z_ref[rows, gsl].astype(F32))
                    ms = jnp.mean(y * y, axis=-1, keepdims=True)
                    out_ref[rows, gsl] = (y * lax.rsqrt(ms + EPS) * nw_ref[:, gsl]).astype(BF16)
            xw = (xg.astype(F32) * w_x[:, gsl]).astype(BF16)
            upd = lax.dot_general(bg, xw, (((0,), (0,)), ((), ())), preferred_element_type=F32)
            s_ref[g] = sdec_x[:, gsl] * s_old + upd

    for c in ((1, 0) if reverse else (0, 1)):
        chunk(c)

    if mode == "state":
        @pl.when(j == nblk - 1)
        def _():
            out_ref[0] = s_ref[...]


def _ssd_call(xbc, dt, dtT, prm, init, *, batch, nblk, reverse, mode, extra=()):
    m = xbc.shape[0]

    def blk(b, j):
        return b * nblk + (nblk - 1 - j if reverse else j)

    state_spec = pl.BlockSpec((1, SSM_GROUPS, SSM_STATE, GROUP_W), lambda b, j: (b, 0, 0, 0))
    const2 = lambda b, j: (0, 0)
    in_specs = [pl.BlockSpec((BLK, D_XBC), lambda b, j: (blk(b, j), 0)),
                pl.BlockSpec((BLK, LANES), lambda b, j: (blk(b, j), 0)),
                pl.BlockSpec((LANES, BLK), lambda b, j: (0, blk(b, j))),
                pl.BlockSpec((1, LANES), const2), pl.BlockSpec((LANES, 1), const2),
                pl.BlockSpec((1, LANES), const2), pl.BlockSpec((LANES, 1), const2),
                pl.BlockSpec((LANES, D_SSM), const2),
                state_spec]
    args = [xbc, dt, dtT, prm["alog_row"], prm["alog_col"], prm["bias_row"], prm["bias_col"],
            prm["expand_rev" if reverse else "expand_fwd"], init]
    if mode == "state":
        out_specs = state_spec
        out_shape = jax.ShapeDtypeStruct((batch, SSM_GROUPS, SSM_STATE, GROUP_W), F32)
    else:
        out_specs = pl.BlockSpec((BLK, D_SSM), lambda b, j: (blk(b, j), 0))
        out_shape = jax.ShapeDtypeStruct((m, D_SSM), BF16)
    if mode == "final":
        z, y_other = extra
        in_specs += [pl.BlockSpec((BLK, D_SSM), lambda b, j: (blk(b, j), 0)),
                     pl.BlockSpec((BLK, D_SSM), lambda b, j: (blk(b, j), 0)),
                     pl.BlockSpec((1, D_SSM), const2), pl.BlockSpec((1, D_SSM), const2)]
        args += [z, y_other, prm["dskip"], prm["norm_w"]]
    kern = functools.partial(_ssd_kernel, reverse=reverse, mode=mode, nblk=nblk)
    return pl.pallas_call(
        kern,
        grid=(batch, nblk),
        in_specs=in_specs,
        out_specs=out_specs,
        out_shape=out_shape,
        scratch_shapes=[pltpu.VMEM((SSM_GROUPS, SSM_STATE, GROUP_W), F32)],
        compiler_params=_cparams(("arbitrary", "arbitrary")),
        name=f"ssd_{mode}_{'rev' if reverse else 'fwd'}",
    )(*args)


def _outproj_kernel(ys_ref, yc_ref, scw_ref, w_ref, x_ref, g1_ref, nw_ref, sh_ref, sc_ref, wr_ref,
                    fh_ref, aff_ref, *, tm, sub):
    for k in range(tm // sub):
        rows = slice(k * sub, (k + 1) * sub)
        yc = yc_ref[rows, :].astype(F32)
        ms = jnp.mean(yc * yc, axis=-1, keepdims=True)
        ycn = (yc * lax.rsqrt(ms + EPS) * scw_ref[...]).astype(BF16)
        mix = jnp.dot(ys_ref[rows, :], w_ref[0:D_SSM, :], preferred_element_type=F32)
        mix = mix + jnp.dot(ycn, w_ref[D_SSM:, :], preferred_element_type=F32)
        h = x_ref[rows, :] + g1_ref[0] * mix
        fh_ref[rows, D_MODEL:] = h
        f = _norm_modulate(h, nw_ref[...], sh_ref[0], sc_ref[0])
        fh_ref[rows, :D_MODEL] = f
        logits = _dot_nt(wr_ref[...], f.astype(BF16))
        mx = jnp.max(logits, axis=0, keepdims=True)
        ex = jnp.exp(logits - mx)
        aff_ref[0, :, rows] = ex / jnp.sum(ex, axis=0, keepdims=True)


def _outproj_call(y_ssm, y_sc, sc_norm_w, w_out, x2d, g1, norm_w, shift, scale, w_rT, *, batch, tm):
    m, d = x2d.shape
    n_tok = m // batch
    per = n_tok // tm
    vec = lambda i: (i // per, 0, 0)
    const2 = lambda i: (0, 0)
    sub = min(tm, 256)
    return pl.pallas_call(
        functools.partial(_outproj_kernel, tm=tm, sub=sub),
        grid=(m // tm,),
        in_specs=[pl.BlockSpec((tm, D_SSM), lambda i: (i, 0)),
                  pl.BlockSpec((tm, D_SC), lambda i: (i, 0)),
                  pl.BlockSpec((1, D_SC), const2),
                  pl.BlockSpec((D_SSM + D_SC, d), const2, pipeline_mode=pl.Buffered(1)),
                  pl.BlockSpec((tm, d), lambda i: (i, 0)),
                  pl.BlockSpec((1, 1, d), vec),
                  pl.BlockSpec((1, d), const2),
                  pl.BlockSpec((1, 1, d), vec),
                  pl.BlockSpec((1, 1, d), vec),
                  pl.BlockSpec((N_EXPERTS, d), const2)],
        out_specs=[pl.BlockSpec((tm, 2 * d), lambda i: (i, 0)),
                   pl.BlockSpec((1, N_EXPERTS, tm), lambda i: (i // per, 0, i % per))],
        out_shape=[jax.ShapeDtypeStruct((m, 2 * d), F32),
                   jax.ShapeDtypeStruct((batch, N_EXPERTS, n_tok), F32)],
        compiler_params=_cparams(("arbitrary",)),
        name="outproj",
    )(y_ssm, y_sc, sc_norm_w, w_out, x2d, g1, norm_w, shift, scale, w_rT)


def _prefix_lanes(m01):
    rows, n = m01.shape
    ki = lax.broadcasted_iota(I32, (LANES, LANES), 0)
    ci = lax.broadcasted_iota(I32, (LANES, LANES), 1)
    triu = jnp.where(ki <= ci, 1.0, 0.0).astype(BF16)
    off = jnp.zeros((rows, 1), F32)
    parts = []
    for k in range(n // LANES):
        p = jnp.dot(m01[:, k * LANES:(k + 1) * LANES].astype(BF16), triu,
                    preferred_element_type=F32)
        parts.append(p + off)
        off = off + p[:, LANES - 1:LANES]
    return jnp.concatenate(parts, axis=1)


def _route_kernel(aff_ref, idx_ref, gcol_ref, *, cap):
    a = aff_ref[0]
    n_e, n = a.shape
    thr = jnp.zeros((n_e, 1), I32)
    for bit in range(30, -1, -1):
        cand = thr | (1 << bit)
        cand_f = lax.bitcast_convert_type(cand, F32)
        cnt = jnp.sum(jnp.where(a >= cand_f, 1.0, 0.0), axis=1, keepdims=True)
        thr = jnp.where(cnt >= cap, cand, thr)
    thr_f = lax.bitcast_convert_type(thr, F32)
    gt = a > thr_f
    eq = a == thr_f
    need = cap - jnp.sum(jnp.where(gt, 1.0, 0.0), axis=1, keepdims=True)
    eqf = jnp.where(eq, 1.0, 0.0)
    ties_before = _prefix_lanes(eqf) - eqf
    sel = jnp.logical_or(gt, jnp.logical_and(eq, ties_before < need))
    self_ = jnp.where(sel, 1.0, 0.0)
    slot = jnp.where(sel, _prefix_lanes(self_) - 1.0, -1.0).astype(I32)

    tok = lax.broadcasted_iota(I32, (1, n), 1)
    t_hi = (tok // 64).astype(F32)
    t_lo = (tok % 64).astype(F32)
    a1 = a.astype(BF16).astype(F32)
    ra = a - a1
    a2 = ra.astype(BF16).astype(F32)
    a3 = ra - a2
    srow = lax.broadcasted_iota(I32, (cap, n), 0)
    r8 = lax.broadcasted_iota(I32, (8, n), 0)
    zpad = jnp.zeros((LANES - 8, n), F32)
    for e in range(n_e):
        onehot = jnp.where(slot[e:e + 1, :] == srow, 1.0, 0.0).astype(BF16)
        v8 = jnp.where(r8 == 0, t_hi, jnp.where(r8 == 1, t_lo, jnp.where(
            r8 == 2, a1[e:e + 1], jnp.where(r8 == 3, a2[e:e + 1], jnp.where(
                r8 == 4, a3[e:e + 1], 0.0)))))
        vals = jnp.concatenate([v8, zpad], axis=0).astype(BF16)
        o_col = lax.dot_general(onehot, vals, (((1,), (1,)), ((), ())),
                                preferred_element_type=F32)
        o_row = o_col.T
        idx_ref[0, e:e + 1, :] = (o_row[0:1] * 64.0 + o_row[1:2]).astype(I32)
        gcol_ref[0, e] = o_col


def _route_call(aff, *, cap):
    batch, n_e, n = aff.shape
    return pl.pallas_call(
        functools.partial(_route_kernel, cap=cap),
        grid=(batch,),
        in_specs=[pl.BlockSpec((1, n_e, n), lambda b: (b, 0, 0))],
        out_specs=[pl.BlockSpec((1, n_e, cap), lambda b: (b, 0, 0)),
                   pl.BlockSpec((1, n_e, cap, LANES), lambda b: (b, 0, 0, 0))],
        out_shape=[jax.ShapeDtypeStruct((batch, n_e, cap), I32),
                   jax.ShapeDtypeStruct((batch, n_e, cap, LANES), F32)],
        compiler_params=_cparams(("arbitrary",)),
        name="route",
    )(aff)


MOE_PARTS = 4
MOE_NCHUNK = 512


def _moe_kernel(idx_ref, gcol_ref, g2_ref, wg_ref, wu_ref, wd_ref, fh_in, fh_out,
                buf0, buf1, sem, *, cap, n_tok, n_pairs):
    e = pl.program_id(0)
    i = pl.program_id(1)
    first = jnp.logical_and(e == 0, i == 0)
    last = jnp.logical_and(e == pl.num_programs(0) - 1, i == n_pairs - 1)
    i_next = jnp.where(i == n_pairs - 1, 0, i + 1)
    e_next = jnp.where(i == n_pairs - 1, e + 1, e)
    bufs = (buf0, buf1)
    part = cap // MOE_PARTS
    hcols = pl.ds(D_MODEL, D_MODEL)

    def row_of(ex, b, jrow):
        return b * n_tok + idx_ref[(b * N_EXPERTS + ex) * cap + jrow]

    def gather_start(ex, b, s, jrow):
        pltpu.make_async_copy(fh_in.at[pl.ds(row_of(ex, b, jrow), 1)], bufs[s].at[pl.ds(jrow, 1)],
                              sem.at[s]).start()

    def scatter_start(ex, b, s, jrow):
        pltpu.make_async_copy(bufs[s].at[pl.ds(jrow, 1), hcols],
                              fh_out.at[pl.ds(row_of(ex, b, jrow), 1), hcols],
                              sem.at[2 + s]).start()

    def wait_gather(s):
        pltpu.make_async_copy(fh_in.at[pl.ds(0, cap)], bufs[s], sem.at[s]).wait()

    def wait_scatter(s):
        pltpu.make_async_copy(bufs[s].at[:, hcols], fh_out.at[pl.ds(0, cap), hcols],
                              sem.at[2 + s]).wait()

    def compute_part(s, b_local, q, between):
        rows = slice(q * part, (q + 1) * part)
        xq = bufs[s][rows, 0:D_MODEL].astype(BF16)
        gc = gcol_ref[b_local, 0, rows, :]
        gate = gc[:, 2:3] + gc[:, 3:4] + gc[:, 4:5]
        g2 = g2_ref[2 * i + b_local]
        nc = MOE_NCHUNK
        hid = []
        for c in range(EXPERT_FF // nc):
            cols = slice(c * nc, (c + 1) * nc)
            between[2 * c]()
            hg = jnp.dot(xq, wg_ref[0, :, cols], preferred_element_type=F32)
            between[2 * c + 1]()
            hu = jnp.dot(xq, wu_ref[0, :, cols], preferred_element_type=F32)
            hid.append((_silu(hg) * hu).astype(BF16))
        hid = jnp.concatenate(hid, axis=1)
        k0 = 2 * (EXPERT_FF // nc)
        for c in range(D_MODEL // nc):
            cols = slice(c * nc, (c + 1) * nc)
            hc = slice(D_MODEL + c * nc, D_MODEL + (c + 1) * nc)
            between[k0 + c]()
            ye = jnp.dot(hid, wd_ref[0, :, cols], preferred_element_type=F32)
            bufs[s][rows, hc] = bufs[s][rows, hc] + g2[:, cols] * (ye * gate)

    n_between = 2 * (EXPERT_FF // MOE_NCHUNK) + D_MODEL // MOE_NCHUNK

    def spread(jobs):
        n = len(jobs)
        cuts = [n * k // n_between for k in range(n_between + 1)]

        def group(k):
            def run():
                for job in jobs[cuts[k]:cuts[k + 1]]:
                    job()
            return run
        return [group(k) for k in range(n_between)]

    b_a, b_b = 2 * i, 2 * i + 1
    b_n = 2 * i_next

    @pl.when(first)
    def _():
        @pl.loop(0, cap)
        def _(jrow):
            gather_start(e, b_a, 0, jrow)

    wait_gather(0)
    for q in range(MOE_PARTS):
        jobs = [functools.partial(gather_start, e, b_b, 1, jrow)
                for jrow in range(q * part, (q + 1) * part)]
        if q > 0:
            jobs += [functools.partial(scatter_start, e, b_a, 0, jrow)
                     for jrow in range((q - 1) * part, q * part)]
        compute_part(0, 0, q, spread(jobs))

    wait_gather(1)
    rows_next = [cap * k // (MOE_PARTS - 1) for k in range(MOE_PARTS)]
    for q in range(MOE_PARTS):
        if q == 0:
            jobs = [functools.partial(scatter_start, e, b_a, 0, jrow)
                    for jrow in range((MOE_PARTS - 1) * part, cap)]
        else:
            jobs = [functools.partial(scatter_start, e, b_b, 1, jrow)
                    for jrow in range((q - 1) * part, q * part)]
            jobs += [functools.partial(gather_start, e_next, b_n, 0, jrow)
                     for jrow in range(rows_next[q - 1], rows_next[q])]
        if q == 1:
            wait_scatter(0)
        compute_part(1, 1, q, spread(jobs))

    for jrow in range((MOE_PARTS - 1) * part, cap):
        scatter_start(e, b_b, 1, jrow)
    wait_scatter(1)

    @pl.when(last)
    def _():
        wait_gather(0)


def _moe_call(idx_flat, gcol, g2, wg, wu, wd, fh, *, batch, cap):
    m, d = fh.shape[0], fh.shape[1] // 2
    n_tok = m // batch
    assert batch % 2 == 0 and cap % MOE_PARTS == 0
    n_pairs = batch // 2
    kern = functools.partial(_moe_kernel, cap=cap, n_tok=n_tok, n_pairs=n_pairs)
    grid_spec = pltpu.PrefetchScalarGridSpec(
        num_scalar_prefetch=1,
        grid=(N_EXPERTS, n_pairs),
        in_specs=[pl.BlockSpec((2, 1, cap, LANES), lambda e, i, idx: (i, e, 0, 0)),
                  pl.BlockSpec((batch, 1, d), lambda e, i, idx: (0, 0, 0)),
                  pl.BlockSpec((1, d, EXPERT_FF), lambda e, i, idx: (e, 0, 0)),
                  pl.BlockSpec((1, d, EXPERT_FF), lambda e, i, idx: (e, 0, 0)),
                  pl.BlockSpec((1, EXPERT_FF, d), lambda e, i, idx: (e, 0, 0)),
                  pl.BlockSpec(memory_space=pl.ANY)],
        out_specs=pl.BlockSpec(memory_space=pl.ANY),
        scratch_shapes=[pltpu.VMEM((cap, 2 * d), F32), pltpu.VMEM((cap, 2 * d), F32),
                        pltpu.SemaphoreType.DMA((4,))],
    )
    return pl.pallas_call(
        kern,
        grid_spec=grid_spec,
        out_shape=jax.ShapeDtypeStruct((m, 2 * d), F32),
        input_output_aliases={6: 0},
        compiler_params=_cparams(("arbitrary", "arbitrary")),
        name="moe",
    )(idx_flat, gcol, g2, wg, wu, wd, fh)


def _final_kernel(h_ref, w_ref, o_ref):
    h = h_ref[...]
    ms = jnp.mean(h * h, axis=-1, keepdims=True)
    o_ref[...] = h * lax.rsqrt(ms + EPS) * w_ref[...]


def _final_call(fh, w, *, tm):
    m, d = fh.shape[0], fh.shape[1] // 2
    return pl.pallas_call(
        _final_kernel,
        grid=(m // tm,),
        in_specs=[pl.BlockSpec((tm, d), lambda i: (i, 1)), pl.BlockSpec((1, d), lambda i: (0, 0))],
        out_specs=pl.BlockSpec((tm, d), lambda i: (i, 0)),
        out_shape=jax.ShapeDtypeStruct((m, d), F32),
        compiler_params=_cparams(("arbitrary",)),
        name="final_norm",
    )(fh, w)


def _pick_tile(n, pref):
    t = pref
    while n % t:
        t //= 2
    return t


def _ssd_params(a_log, dt_bias, d_skip, norm_w):
    padw = LANES - N_DT
    alog = jnp.pad(a_log.astype(F32), (0, padw))
    bias = jnp.pad(dt_bias.astype(F32), (0, padw))
    lane = jnp.arange(LANES)[:, None]
    head = jnp.arange(D_SSM)[None, :] // SSM_HEAD_DIM
    return {
        "alog_row": alog[None, :], "alog_col": alog[:, None],
        "bias_row": bias[None, :], "bias_col": bias[:, None],
        "expand_fwd": (lane == head).astype(BF16),
        "expand_rev": (lane == head + SSM_HEADS).astype(BF16),
        "dskip": jnp.repeat(d_skip.astype(F32), SSM_HEAD_DIM)[None, :],
        "norm_w": norm_w.astype(F32)[None, :],
    }


def kernel(x, c, ctx, c_ctx, w_mod, b_mod, norm_mix_w, w_in, ssm_conv_w, ssm_conv_b, ssm_a_log,
           ssm_dt_bias, ssm_d, ssm_norm_w, sc_conv_w, sc_norm_w, w_out, norm_ffn_w, w_router,
           w_gate, w_up, w_down, final_norm_w):
    batch, n_lat, d = x.shape
    n_ctx = ctx.shape[1]
    depth = w_mod.shape[0]
    assert depth == 1 and d == D_MODEL and n_ctx == BLK and n_lat % BLK == 0
    layer = 0
    cap = EC_CAPACITY_FACTOR * n_lat // N_EXPERTS

    pad_rows = -(batch + 1) % 8
    cvec = jnp.concatenate([c, c_ctx[None, :], jnp.zeros((pad_rows, d), F32)], axis=0)
    mod = _mod_call(cvec, w_mod[layer], b_mod[layer][None, :])
    sh1, sc1, g1, sh2, sc2, g2 = [mod[:batch, k * d:(k + 1) * d][:, None, :] for k in range(N_MOD)]
    sh1_c = mod[batch:batch + 1, 0:d][:, None, :]
    sc1_c = mod[batch:batch + 1, d:2 * d][:, None, :]

    w_inT = w_in[layer].T.astype(BF16)
    tc = 512
    w_out_b = w_out[layer].astype(BF16)
    w_rT = w_router[layer].T.astype(BF16)
    wg, wu, wd = w_gate[layer].astype(BF16), w_up[layer].astype(BF16), w_down[layer].astype(BF16)
    norm_mix = norm_mix_w[layer][None, :]
    conv_w, conv_b = ssm_conv_w[layer], ssm_conv_b[layer][None, :]
    prm = _ssd_params(ssm_a_log[layer], ssm_dt_bias[layer], ssm_d[layer], ssm_norm_w[layer])

    ctx2d = ctx.reshape(batch * n_ctx, d)
    xbc_c, dt_c, dtT_c = _inproj_zx_call(
        ctx2d, sh1_c, sc1_c, norm_mix, w_inT, conv_w, conv_b,
        with_z=False, period=n_ctx, tm=_pick_tile(batch * n_ctx, 1024), rows_per_mod=batch * n_ctx)
    zero_state = jnp.zeros((batch, SSM_GROUPS, SSM_STATE, GROUP_W), F32)
    st_f = _ssd_call(xbc_c, dt_c, dtT_c, prm, zero_state, batch=batch, nblk=1, reverse=False,
                     mode="state")
    st_r = _ssd_call(xbc_c, dt_c, dtT_c, prm, zero_state, batch=batch, nblk=1, reverse=True,
                     mode="state")

    x2d = x.reshape(batch * n_lat, d)
    tm = _pick_tile(n_lat, 1024)
    z, xbc, dt, dtT, a_lat = _inproj_zx_call(
        x2d, sh1, sc1, norm_mix, w_inT, conv_w, conv_b,
        with_z=True, period=GRID_W, tm=tm, rows_per_mod=n_lat)
    y_sc = _inproj_sc_call(a_lat, w_inT, sc_conv_w[layer], tm=tm, tc=tc)
    nblk = n_lat // BLK
    y_rev = _ssd_call(xbc, dt, dtT, prm, st_r, batch=batch, nblk=nblk, reverse=True, mode="partial")
    y_ssm = _ssd_call(xbc, dt, dtT, prm, st_f, batch=batch, nblk=nblk, reverse=False, mode="final",
                      extra=(z, y_rev))

    fh, aff = _outproj_call(
        y_ssm, y_sc, sc_norm_w[layer][None, :], w_out_b, x2d, g1, norm_ffn_w[layer][None, :],
        sh2, sc2, w_rT, batch=batch, tm=_pick_tile(n_lat, 512))
    idx, gcol = _route_call(aff, cap=cap)
    fh = _moe_call(idx.reshape(-1), gcol, g2, wg, wu, wd, fh, batch=batch, cap=cap)
    out = _final_call(fh, final_norm_w[None, :], tm=_pick_tile(batch * n_lat, 512))
    return out.reshape(batch, n_lat, d)
```

```python
import functools

import jax
import jax.numpy as jnp
from jax import lax
from jax.experimental import pallas as pl
from jax.experimental.pallas import tpu as pltpu

F32 = jnp.float32
BF16 = jnp.bfloat16
I32 = jnp.int32

D_MODEL = 2048
GRID_W = 64
D_SSM = 2048
SSM_HEAD_DIM = 64
SSM_HEADS = D_SSM // SSM_HEAD_DIM
SSM_GROUPS = 4
SSM_HPG = SSM_HEADS // SSM_GROUPS
SSM_STATE = 128
SSD_CHUNK = 128
GN = SSM_GROUPS * SSM_STATE
D_XBC = D_SSM + 2 * GN
D_ZX = D_SSM + D_XBC
N_DT = 2 * SSM_HEADS
D_SSM_IN = D_ZX + N_DT
D_SC = 2048
N_EXPERTS = 16
EC_CAPACITY_FACTOR = 2
EXPERT_FF = 1024
N_MOD = 6
EPS = 1e-6
LOG2E = 1.4426950408889634

LANES = 128
BLK = 2 * SSD_CHUNK
GROUP_W = SSM_HPG * SSM_HEAD_DIM
VMEM_LIMIT = 56 * 1024 * 1024


def _cparams(sem):
    return pltpu.CompilerParams(dimension_semantics=sem, vmem_limit_bytes=VMEM_LIMIT)


def _silu(v):
    return v * jax.nn.sigmoid(v)


def _softplus(v):
    return jnp.maximum(v, 0.0) + jnp.log1p(jnp.exp(-jnp.abs(v)))


def _mod_kernel(c_ref, w_ref, b_ref, o_ref):
    s = _silu(c_ref[...]).astype(BF16)
    o_ref[...] = jnp.dot(s, w_ref[...].astype(BF16), preferred_element_type=F32) + b_ref[...]


def _mod_call(cvec, w_mod, b_mod):
    rows, d = cvec.shape
    n = w_mod.shape[1]
    tn = 1024
    return pl.pallas_call(
        _mod_kernel,
        grid=(n // tn,),
        in_specs=[pl.BlockSpec((rows, d), lambda j: (0, 0)),
                  pl.BlockSpec((d, tn), lambda j: (0, j)),
                  pl.BlockSpec((1, tn), lambda j: (0, j))],
        out_specs=pl.BlockSpec((rows, tn), lambda j: (0, j)),
        out_shape=jax.ShapeDtypeStruct((rows, n), F32),
        compiler_params=_cparams(("arbitrary",)),
        name="mod",
    )(cvec, w_mod, b_mod)


def _dot_nt(a, bt):
    return lax.dot_general(a, bt, (((1,), (1,)), ((), ())), preferred_element_type=F32)


def _norm_modulate(h, nw, shift, scale):
    ms = jnp.mean(h * h, axis=-1, keepdims=True)
    hn = h * lax.rsqrt(ms + EPS) * nw
    return hn * (1.0 + scale) + shift


def _conv3(v, cw, period):
    rows = v.shape[0]
    r = lax.broadcasted_iota(I32, (rows, 1), 0) % period
    prev = jnp.where(r == 0, 0.0, pltpu.roll(v, 1, 0))
    nxt = jnp.where(r == period - 1, 0.0, pltpu.roll(v, rows - 1, 0))
    return prev * cw[0:1] + v * cw[1:2] + nxt * cw[2:3]


def _inproj_zx_kernel(h_ref, sh_ref, sc_ref, nw_ref, w_ref, wdt_ref, cw_ref, cb_ref,
                      *rest, with_z, period, tm, n_z):
    if with_z:
        z_ref, xbc_ref, dt_ref, dtT_ref, ao_ref, a_ref = rest
    else:
        xbc_ref, dt_ref, dtT_ref, a_ref = rest
    n = pl.program_id(1)

    @pl.when(n == 0)
    def _():
        a = _norm_modulate(h_ref[...], nw_ref[...], sh_ref[0], sc_ref[0]).astype(BF16)
        a_ref[...] = a
        if with_z:
            ao_ref[...] = a
        wdt = wdt_ref[...]
        dt_ref[...] = _dot_nt(a, wdt)
        dtT_ref[...] = _dot_nt(wdt, a)

    wb = w_ref[...]

    if with_z:
        @pl.when(n < n_z)
        def _():
            for k in range(tm // BLK):
                rows = slice(k * BLK, (k + 1) * BLK)
                z_ref[rows, :] = _dot_nt(a_ref[rows, :], wb).astype(BF16)

    @pl.when(n >= n_z)
    def _():
        for k in range(tm // BLK):
            rows = slice(k * BLK, (k + 1) * BLK)
            acc = _dot_nt(a_ref[rows, :], wb)
            y = _conv3(acc, cw_ref[...], period) + cb_ref[...]
            xbc_ref[rows, :] = _silu(y).astype(BF16)


def _inproj_zx_call(h2d, shift, scale, norm_w, w_inT, conv_w, conv_b, *,
                    with_z, period, tm, rows_per_mod):
    m, d = h2d.shape
    tn = 1024
    n_z = D_SSM // tn if with_z else 0
    n_x = D_XBC // tn
    n_off = 0 if with_z else D_SSM // tn
    per = rows_per_mod // tm
    kern = functools.partial(_inproj_zx_kernel, with_z=with_z, period=period, tm=tm, n_z=n_z)
    out_shape = [jax.ShapeDtypeStruct((m, D_XBC), BF16),
                 jax.ShapeDtypeStruct((m, LANES), F32),
                 jax.ShapeDtypeStruct((LANES, m), F32)]
    out_specs = [pl.BlockSpec((tm, tn), lambda i, n: (i, jnp.maximum(n - n_z, 0))),
                 pl.BlockSpec((tm, LANES), lambda i, n: (i, 0)),
                 pl.BlockSpec((LANES, tm), lambda i, n: (0, i))]
    if with_z:
        out_shape = ([jax.ShapeDtypeStruct((m, D_SSM), BF16)] + out_shape
                     + [jax.ShapeDtypeStruct((m, d), BF16)])
        out_specs = ([pl.BlockSpec((tm, tn), lambda i, n: (i, jnp.minimum(n, n_z - 1)))] + out_specs
                     + [pl.BlockSpec((tm, d), lambda i, n: (i, 0))])
    return pl.pallas_call(
        kern,
        grid=(m // tm, n_z + n_x),
        in_specs=[pl.BlockSpec((tm, d), lambda i, n: (i, 0)),
                  pl.BlockSpec((1, 1, d), lambda i, n: (i // per, 0, 0)),
                  pl.BlockSpec((1, 1, d), lambda i, n: (i // per, 0, 0)),
                  pl.BlockSpec((1, d), lambda i, n: (0, 0)),
                  pl.BlockSpec((tn, d), lambda i, n: (n + n_off, 0)),
                  pl.BlockSpec((LANES, d), lambda i, n: (D_ZX // LANES, 0)),
                  pl.BlockSpec((3, tn), lambda i, n: (0, jnp.maximum(n - n_z, 0))),
                  pl.BlockSpec((1, tn), lambda i, n: (0, jnp.maximum(n - n_z, 0)))],
        out_specs=out_specs,
        out_shape=out_shape,
        scratch_shapes=[pltpu.VMEM((tm, d), BF16)],
        compiler_params=_cparams(("arbitrary", "arbitrary")),
        name="inproj_zx" if with_z else "inproj_ctx",
    )(h2d, shift, scale, norm_w, w_inT, w_inT, conv_w, conv_b)


def _inproj_sc_kernel(a_ref, wb_ref, wc_ref, wv_ref, cw_ref, o_ref, *, tm):
    wb, wc, wv = wb_ref[...], wc_ref[...], wv_ref[...]
    for k in range(tm // BLK):
        rows = slice(k * BLK, (k + 1) * BLK)
        a = a_ref[rows, :]
        cg = _dot_nt(a, wc)
        hv = _dot_nt(a, wv)
        v = _conv3(cg * hv, cw_ref[...], GRID_W)
        bg = _dot_nt(a, wb)
        o_ref[rows, :] = (bg * v).astype(BF16)


def _inproj_sc_call(a2d, w_inT, conv_w, *, tm, tc):
    m, d = a2d.shape
    kern = functools.partial(_inproj_sc_kernel, tm=tm)

    def wspec(k):
        return pl.BlockSpec((pl.Element(tc), pl.Element(d)),
                            lambda i, n: (pl.multiple_of(D_SSM_IN + k * D_SC + n * tc, 16), 0))

    return pl.pallas_call(
        kern,
        grid=(m // tm, D_SC // tc),
        in_specs=[pl.BlockSpec((tm, d), lambda i, n: (i, 0)),
                  wspec(0), wspec(1), wspec(2),
                  pl.BlockSpec((3, tc), lambda i, n: (0, n))],
        out_specs=pl.BlockSpec((tm, tc), lambda i, n: (i, n)),
        out_shape=jax.ShapeDtypeStruct((m, D_SC), BF16),
        compiler_params=_cparams(("arbitrary", "arbitrary")),
        name="inproj_sc",
    )(a2d, w_inT, w_inT, w_inT, conv_w)


def _ssd_kernel(*refs, reverse, mode, nblk):
    xbc_ref, dt_ref, dtT_ref, alr_ref, alc_ref, br_ref, bc_ref, e_ref, init_ref = refs[:9]
    rest = refs[9:]
    if mode == "state":
        out_ref, s_ref = rest
    elif mode == "partial":
        out_ref, s_ref = rest
    else:
        z_ref, yo_ref, dsk_ref, nw_ref, out_ref, s_ref = rest
    j = pl.program_id(1)

    @pl.when(j == 0)
    def _():
        s_ref[...] = init_ref[0]

    ii = lax.broadcasted_iota(I32, (SSD_CHUNK, SSD_CHUNK), 0)
    jj = lax.broadcasted_iota(I32, (SSD_CHUNK, SSD_CHUNK), 1)
    tri = jnp.where(jj <= ii, 1.0, 0.0).astype(F32)
    triT = jnp.where(ii <= jj, 1.0, 0.0).astype(F32)
    mask = (jj >= ii) if reverse else (ii >= jj)
    lane4 = lax.broadcasted_iota(I32, (SSD_CHUNK, 4 * SSM_HEAD_DIM), 1) // SSM_HEAD_DIM
    aneg_r = -jnp.exp(alr_ref[...]) * LOG2E
    aneg_c = -jnp.exp(alc_ref[...]) * LOG2E
    lane0 = SSM_HEADS if reverse else 0
    hi = lax.Precision.HIGHEST

    def chunk(c):
        rows = slice(c * SSD_CHUNK, (c + 1) * SSD_CHUNK)
        dt = _softplus(dt_ref[rows, :] + br_ref[...])
        dtT = _softplus(dtT_ref[:, rows] + bc_ref[...])
        la = dt * aneg_r
        laT = dtT * aneg_c
        cs = jnp.dot(tri, la, precision=hi, preferred_element_type=F32)
        csT = jnp.dot(laT, triT, precision=hi, preferred_element_type=F32)
        total = cs[SSD_CHUNK - 1:SSD_CHUNK, :]
        log2_dtT = jnp.log(dtT) * LOG2E
        if reverse:
            ecs = cs - la
            col, rowm = -ecs, csT - laT + log2_dtT
            yscale = jnp.exp2(total - ecs)
            w = dt * jnp.exp2(ecs)
        else:
            col, rowm = cs, log2_dtT - csT
            yscale = jnp.exp2(cs)
            w = dt * jnp.exp2(total - cs)
        t1 = total.astype(BF16).astype(F32)
        r1 = total - t1
        t2 = r1.astype(BF16).astype(F32)
        t3 = r1 - t2
        r16 = lax.broadcasted_iota(I32, (16, LANES), 0)
        tail = jnp.where(r16 == 0, t1, jnp.where(r16 == 1, t2, jnp.where(r16 == 2, t3, 0.0)))
        stack = jnp.concatenate([yscale, w, tail], axis=0).astype(BF16)
        ex = jnp.dot(stack, e_ref[...], preferred_element_type=F32)
        yscale_x = ex[0:SSD_CHUNK]
        w_x = ex[SSD_CHUNK:2 * SSD_CHUNK]
        tb = 2 * SSD_CHUNK
        sdec_x = jnp.exp2(ex[tb:tb + 1] + ex[tb + 1:tb + 2] + ex[tb + 2:tb + 3])

        for g in range(SSM_GROUPS):
            gsl = slice(g * GROUP_W, (g + 1) * GROUP_W)
            xg = xbc_ref[rows, gsl]
            bg = xbc_ref[rows, D_SSM + g * SSM_STATE:D_SSM + (g + 1) * SSM_STATE]
            s_old = s_ref[g]
            if mode != "state":
                cg = xbc_ref[rows, D_SSM + GN + g * SSM_STATE:D_SSM + GN + (g + 1) * SSM_STATE]
                cb = lax.dot_general(cg, bg, (((1,), (1,)), ((), ())), preferred_element_type=F32)
                y_off = jnp.dot(cg, s_old.astype(BF16), preferred_element_type=F32)
                halves = []
                for q in range(2):
                    xq = xg[:, q * 256:(q + 1) * 256]
                    acc = jnp.zeros((SSD_CHUNK, 256), F32)
                    for r4 in range(4):
                        lane = lane0 + g * SSM_HPG + q * 4 + r4
                        seg = col[:, lane:lane + 1] + rowm[lane:lane + 1, :]
                        dec = jnp.exp2(jnp.where(mask, seg, -jnp.inf))
                        mh = (cb * dec).astype(BF16)
                        xm = jnp.where(lane4 == r4, xq, jnp.zeros_like(xq))
                        acc = acc + jnp.dot(mh, xm, preferred_element_type=F32)
                    halves.append(acc)
                y = jnp.concatenate(halves, axis=1) + yscale_x[:, gsl] * y_off
                if mode == "partial":
                    out_ref[rows, gsl] = y.astype(BF16)
                else:
                    y = y + yo_ref[rows, gsl].astype(F32) + dsk_ref[:, gsl] * xg.astype(F32)
                    y = y * _silu(z_ref[rows, gsl].astype(F32))
                    ms = jnp.mean(y * y, axis=-1, keepdims=True)
                    out_ref[rows, gsl] = (y * lax.rsqrt(ms + EPS) * nw_ref[:, gsl]).astype(BF16)
            xw = (xg.astype(F32) * w_x[:, gsl]).astype(BF16)
            upd = lax.dot_general(bg, xw, (((0,), (0,)), ((), ())), preferred_element_type=F32)
            s_ref[g] = sdec_x[:, gsl] * s_old + upd

    for c in ((1, 0) if reverse else (0, 1)):
        chunk(c)

    if mode == "state":
        @pl.when(j == nblk - 1)
        def _():
            out_ref[0] = s_ref[...]


def _ssd_call(xbc, dt, dtT, prm, init, *, batch, nblk, reverse, mode, extra=()):
    m = xbc.shape[0]

    def blk(b, j):
        return b * nblk + (nblk - 1 - j if reverse else j)

    state_spec = pl.BlockSpec((1, SSM_GROUPS, SSM_STATE, GROUP_W), lambda b, j: (b, 0, 0, 0))
    const2 = lambda b, j: (0, 0)
    in_specs = [pl.BlockSpec((BLK, D_XBC), lambda b, j: (blk(b, j), 0)),
                pl.BlockSpec((BLK, LANES), lambda b, j: (blk(b, j), 0)),
                pl.BlockSpec((LANES, BLK), lambda b, j: (0, blk(b, j))),
                pl.BlockSpec((1, LANES), const2), pl.BlockSpec((LANES, 1), const2),
                pl.BlockSpec((1, LANES), const2), pl.BlockSpec((LANES, 1), const2),
                pl.BlockSpec((LANES, D_SSM), const2),
                state_spec]
    args = [xbc, dt, dtT, prm["alog_row"], prm["alog_col"], prm["bias_row"], prm["bias_col"],
            prm["expand_rev" if reverse else "expand_fwd"], init]
    if mode == "state":
        out_specs = state_spec
        out_shape = jax.ShapeDtypeStruct((batch, SSM_GROUPS, SSM_STATE, GROUP_W), F32)
    else:
        out_specs = pl.BlockSpec((BLK, D_SSM), lambda b, j: (blk(b, j), 0))
        out_shape = jax.ShapeDtypeStruct((m, D_SSM), BF16)
    if mode == "final":
        z, y_other = extra
        in_specs += [pl.BlockSpec((BLK, D_SSM), lambda b, j: (blk(b, j), 0)),
                     pl.BlockSpec((BLK, D_SSM), lambda b, j: (blk(b, j), 0)),
                     pl.BlockSpec((1, D_SSM), const2), pl.BlockSpec((1, D_SSM), const2)]
        args += [z, y_other, prm["dskip"], prm["norm_w"]]
    kern = functools.partial(_ssd_kernel, reverse=reverse, mode=mode, nblk=nblk)
    return pl.pallas_call(
        kern,
        grid=(batch, nblk),
        in_specs=in_specs,
        out_specs=out_specs,
        out_shape=out_shape,
        scratch_shapes=[pltpu.VMEM((SSM_GROUPS, SSM_STATE, GROUP_W), F32)],
        compiler_params=_cparams(("arbitrary", "arbitrary")),
        name=f"ssd_{mode}_{'rev' if reverse else 'fwd'}",
    )(*args)


def _outproj_kernel(ys_ref, yc_ref, scw_ref, w_ref, x_ref, g1_ref, nw_ref, sh_ref, sc_ref, wr_ref,
                    fh_ref, aff_ref, *, tm, sub):
    for k in range(tm // sub):
        rows = slice(k * sub, (k + 1) * sub)
        yc = yc_ref[rows, :].astype(F32)
        ms = jnp.mean(yc * yc, axis=-1, keepdims=True)
        ycn = (yc * lax.rsqrt(ms + EPS) * scw_ref[...]).astype(BF16)
        mix = jnp.dot(ys_ref[rows, :], w_ref[0:D_SSM, :], preferred_element_type=F32)
        mix = mix + jnp.dot(ycn, w_ref[D_SSM:, :], preferred_element_type=F32)
        h = x_ref[rows, :] + g1_ref[0] * mix
        fh_ref[rows, D_MODEL:] = h
        f = _norm_modulate(h, nw_ref[...], sh_ref[0], sc_ref[0])
        fh_ref[rows, :D_MODEL] = f
        logits = _dot_nt(wr_ref[...], f.astype(BF16))
        mx = jnp.max(logits, axis=0, keepdims=True)
        ex = jnp.exp(logits - mx)
        aff_ref[0, :, rows] = ex / jnp.sum(ex, axis=0, keepdims=True)


def _outproj_call(y_ssm, y_sc, sc_norm_w, w_out, x2d, g1, norm_w, shift, scale, w_rT, *, batch, tm):
    m, d = x2d.shape
    n_tok = m // batch
    per = n_tok // tm
    vec = lambda i: (i // per, 0, 0)
    const2 = lambda i: (0, 0)
    sub = min(tm, 256)
    return pl.pallas_call(
        functools.partial(_outproj_kernel, tm=tm, sub=sub),
        grid=(m // tm,),
        in_specs=[pl.BlockSpec((tm, D_SSM), lambda i: (i, 0)),
                  pl.BlockSpec((tm, D_SC), lambda i: (i, 0)),
                  pl.BlockSpec((1, D_SC), const2),
                  pl.BlockSpec((D_SSM + D_SC, d), const2, pipeline_mode=pl.Buffered(1)),
                  pl.BlockSpec((tm, d), lambda i: (i, 0)),
                  pl.BlockSpec((1, 1, d), vec),
                  pl.BlockSpec((1, d), const2),
                  pl.BlockSpec((1, 1, d), vec),
                  pl.BlockSpec((1, 1, d), vec),
                  pl.BlockSpec((N_EXPERTS, d), const2)],
        out_specs=[pl.BlockSpec((tm, 2 * d), lambda i: (i, 0)),
                   pl.BlockSpec((1, N_EXPERTS, tm), lambda i: (i // per, 0, i % per))],
        out_shape=[jax.ShapeDtypeStruct((m, 2 * d), F32),
                   jax.ShapeDtypeStruct((batch, N_EXPERTS, n_tok), F32)],
        compiler_params=_cparams(("arbitrary",)),
        name="outproj",
    )(y_ssm, y_sc, sc_norm_w, w_out, x2d, g1, norm_w, shift, scale, w_rT)


def _prefix_lanes(m01):
    rows, n = m01.shape
    ki = lax.broadcasted_iota(I32, (LANES, LANES), 0)
    ci = lax.broadcasted_iota(I32, (LANES, LANES), 1)
    triu = jnp.where(ki <= ci, 1.0, 0.0).astype(BF16)
    off = jnp.zeros((rows, 1), F32)
    parts = []
    for k in range(n // LANES):
        p = jnp.dot(m01[:, k * LANES:(k + 1) * LANES].astype(BF16), triu,
                    preferred_element_type=F32)
        parts.append(p + off)
        off = off + p[:, LANES - 1:LANES]
    return jnp.concatenate(parts, axis=1)


def _route_kernel(aff_ref, idx_ref, gcol_ref, *, cap):
    a = aff_ref[0]
    n_e, n = a.shape
    thr = jnp.zeros((n_e, 1), I32)
    for bit in range(30, -1, -1):
        cand = thr | (1 << bit)
        cand_f = lax.bitcast_convert_type(cand, F32)
        cnt = jnp.sum(jnp.where(a >= cand_f, 1.0, 0.0), axis=1, keepdims=True)
        thr = jnp.where(cnt >= cap, cand, thr)
    thr_f = lax.bitcast_convert_type(thr, F32)
    gt = a > thr_f
    eq = a == thr_f
    need = cap - jnp.sum(jnp.where(gt, 1.0, 0.0), axis=1, keepdims=True)
    eqf = jnp.where(eq, 1.0, 0.0)
    ties_before = _prefix_lanes(eqf) - eqf
    sel = jnp.logical_or(gt, jnp.logical_and(eq, ties_before < need))
    self_ = jnp.where(sel, 1.0, 0.0)
    slot = jnp.where(sel, _prefix_lanes(self_) - 1.0, -1.0).astype(I32)

    tok = lax.broadcasted_iota(I32, (1, n), 1)
    t_hi = (tok // 64).astype(F32)
    t_lo = (tok % 64).astype(F32)
    a1 = a.astype(BF16).astype(F32)
    ra = a - a1
    a2 = ra.astype(BF16).astype(F32)
    a3 = ra - a2
    srow = lax.broadcasted_iota(I32, (cap, n), 0)
    r8 = lax.broadcasted_iota(I32, (8, n), 0)
    zpad = jnp.zeros((LANES - 8, n), F32)
    for e in range(n_e):
        onehot = jnp.where(slot[e:e + 1, :] == srow, 1.0, 0.0).astype(BF16)
        v8 = jnp.where(r8 == 0, t_hi, jnp.where(r8 == 1, t_lo, jnp.where(
            r8 == 2, a1[e:e + 1], jnp.where(r8 == 3, a2[e:e + 1], jnp.where(
                r8 == 4, a3[e:e + 1], 0.0)))))
        vals = jnp.concatenate([v8, zpad], axis=0).astype(BF16)
        o_col = lax.dot_general(onehot, vals, (((1,), (1,)), ((), ())),
                                preferred_element_type=F32)
        o_row = o_col.T
        idx_ref[0, e:e + 1, :] = (o_row[0:1] * 64.0 + o_row[1:2]).astype(I32)
        gcol_ref[0, e] = o_col


def _route_call(aff, *, cap):
    batch, n_e, n = aff.shape
    return pl.pallas_call(
        functools.partial(_route_kernel, cap=cap),
        grid=(batch,),
        in_specs=[pl.BlockSpec((1, n_e, n), lambda b: (b, 0, 0))],
        out_specs=[pl.BlockSpec((1, n_e, cap), lambda b: (b, 0, 0)),
                   pl.BlockSpec((1, n_e, cap, LANES), lambda b: (b, 0, 0, 0))],
        out_shape=[jax.ShapeDtypeStruct((batch, n_e, cap), I32),
                   jax.ShapeDtypeStruct((batch, n_e, cap, LANES), F32)],
        compiler_params=_cparams(("arbitrary",)),
        name="route",
    )(aff)


MOE_PARTS = 4
MOE_NCHUNK = 512
MOE_WROWS_A = 512
MOE_WROWS_D = 128
MOE_WPIECES = 8


def _moe_kernel(idx_ref, gcol_ref, g2_ref, wg_hbm, wu_hbm, wd_hbm, fh_in, fh_out,
                buf0, buf1, wg_s, wu_s, wd_s, stg_a, stg_d, sem, *, cap, n_tok, n_pairs):
    e = pl.program_id(0)
    i = pl.program_id(1)
    n_e = pl.num_programs(0)
    first = jnp.logical_and(e == 0, i == 0)
    last = jnp.logical_and(e == n_e - 1, i == n_pairs - 1)
    i_next = jnp.where(i == n_pairs - 1, 0, i + 1)
    e_next = jnp.where(i == n_pairs - 1, e + 1, e)
    bufs = (buf0, buf1)
    part = cap // MOE_PARTS
    hcols = pl.ds(D_MODEL, D_MODEL)
    cur = e % 2

    def row_of(ex, b, jrow):
        return b * n_tok + idx_ref[(b * N_EXPERTS + ex) * cap + jrow]

    def gather_start(ex, b, s, jrow):
        pltpu.make_async_copy(fh_in.at[pl.ds(row_of(ex, b, jrow), 1)], bufs[s].at[pl.ds(jrow, 1)],
                              sem.at[s]).start()

    def scatter_start(ex, b, s, jrow):
        pltpu.make_async_copy(bufs[s].at[pl.ds(jrow, 1), hcols],
                              fh_out.at[pl.ds(row_of(ex, b, jrow), 1), hcols],
                              sem.at[2 + s]).start()

    def wait_gather(s):
        pltpu.make_async_copy(fh_in.at[pl.ds(0, cap)], bufs[s], sem.at[s]).wait()

    def wait_scatter(s):
        pltpu.make_async_copy(bufs[s].at[:, hcols], fh_out.at[pl.ds(0, cap), hcols],
                              sem.at[2 + s]).wait()

    g_units = (D_MODEL // MOE_WROWS_A) // n_pairs
    d_units = (EXPERT_FF // MOE_WROWS_D) // n_pairs
    units = [("g", j) for j in range(g_units)] + [("u", j) for j in range(g_units)] \
        + [("d", j) for j in range(d_units)]
    n_parts = 2 * MOE_PARTS
    upp = len(units) // n_parts
    pieces = MOE_WPIECES // upp

    def unit_refs(kind, j, tgt, step, slot):
        if kind == "d":
            row0 = pl.multiple_of((step * d_units + j) * MOE_WROWS_D, MOE_WROWS_D)
            return (wd_hbm.at[tgt, pl.ds(row0, MOE_WROWS_D), :], stg_d.at[slot], wd_s, row0,
                    MOE_WROWS_D)
        src = wg_hbm if kind == "g" else wu_hbm
        row0 = pl.multiple_of((step * g_units + j) * MOE_WROWS_A, MOE_WROWS_A)
        return (src.at[tgt, pl.ds(row0, MOE_WROWS_A), :], stg_a.at[slot],
                wg_s if kind == "g" else wu_s, row0, MOE_WROWS_A)

    def unit_copy(t, tgt, step):
        kind, j = units[t]
        src, dst, _, _, _ = unit_refs(kind, j, tgt, step, t % 2)
        return pltpu.make_async_copy(src, dst, sem.at[4 + t % 2])

    def unit_cast(t, step, wslot, piece, n_pieces):
        kind, j = units[t]
        _, stg, dst, row0, rows = unit_refs(kind, j, 0, step, t % 2)
        pr = rows // n_pieces
        dst[wslot, pl.ds(row0 + piece * pr, pr), :] = stg[piece * pr:(piece + 1) * pr, :].astype(BF16)

    tgt_now = jnp.minimum(e + 1, n_e - 1)
    tgt_next = jnp.minimum(e_next + 1, n_e - 1)

    def weight_jobs(p):
        jobs = []
        for c in range(upp):
            t = p * upp + c
            for piece in range(pieces):
                def job(t=t, piece=piece):
                    if piece == 0:
                        unit_copy(t, tgt_now, i).wait()
                        if t + 1 < len(units):
                            unit_copy(t + 1, tgt_now, i).start()
                        else:
                            unit_copy(0, tgt_next, i_next).start()
                    unit_cast(t, i, 1 - cur, piece, pieces)
                jobs.append(job)
        return jobs

    def compute_part(s, b_local, q, between, wjobs):
        rows = slice(q * part, (q + 1) * part)
        xq = bufs[s][rows, 0:D_MODEL].astype(BF16)
        gc = gcol_ref[b_local, 0, rows, :]
        gate = gc[:, 2:3] + gc[:, 3:4] + gc[:, 4:5]
        g2 = g2_ref[2 * i + b_local]
        nc = MOE_NCHUNK
        hid = []
        slot_no = 0

        def tick():
            nonlocal slot_no
            between[slot_no]()
            if slot_no < len(wjobs):
                wjobs[slot_no]()
            slot_no += 1

        for c in range(EXPERT_FF // nc):
            cols = slice(c * nc, (c + 1) * nc)
            tick()
            hg = jnp.dot(xq, wg_s[cur, :, cols], preferred_element_type=F32)
            tick()
            hu = jnp.dot(xq, wu_s[cur, :, cols], preferred_element_type=F32)
            hid.append((_silu(hg) * hu).astype(BF16))
        hid = jnp.concatenate(hid, axis=1)
        for c in range(D_MODEL // nc):
            cols = slice(c * nc, (c + 1) * nc)
            hc = slice(D_MODEL + c * nc, D_MODEL + (c + 1) * nc)
            tick()
            ye = jnp.dot(hid, wd_s[cur, :, cols], preferred_element_type=F32)
            bufs[s][rows, hc] = bufs[s][rows, hc] + g2[:, cols] * (ye * gate)

    n_between = 2 * (EXPERT_FF // MOE_NCHUNK) + D_MODEL // MOE_NCHUNK

    def spread(jobs):
        n = len(jobs)
        cuts = [n * k // n_between for k in range(n_between + 1)]

        def group(k):
            def run():
                for job in jobs[cuts[k]:cuts[k + 1]]:
                    job()
            return run
        return [group(k) for k in range(n_between)]

    b_a, b_b = 2 * i, 2 * i + 1
    b_n = 2 * i_next

    @pl.when(first)
    def _():
        @pl.loop(0, cap)
        def _(jrow):
            gather_start(e, b_a, 0, jrow)
        for step in range(n_pairs):
            for t in range(len(units)):
                kind, j = units[t]
                src, _, _, _, _ = unit_refs(kind, j, 0, step, 0)
                stg0 = stg_d.at[0] if kind == "d" else stg_a.at[0]
                cp = pltpu.make_async_copy(src, stg0, sem.at[4])
                cp.start()
                cp.wait()
                _, _, dst, row0, rows = unit_refs(kind, j, 0, step, 0)
                dst[0, pl.ds(row0, rows), :] = (stg_d[0] if kind == "d" else stg_a[0]).astype(BF16)
        unit_copy(0, tgt_now, i).start()

    wait_gather(0)
    for q in range(MOE_PARTS):
        jobs = [functools.partial(gather_start, e, b_b, 1, jrow)
                for jrow in range(q * part, (q + 1) * part)]
        if q > 0:
            jobs += [functools.partial(scatter_start, e, b_a, 0, jrow)
                     for jrow in range((q - 1) * part, q * part)]
        compute_part(0, 0, q, spread(jobs), weight_jobs(q))

    wait_gather(1)
    rows_next = [cap * k // (MOE_PARTS - 1) for k in range(MOE_PARTS)]
    for q in range(MOE_PARTS):
        if q == 0:
            jobs = [functools.partial(scatter_start, e, b_a, 0, jrow)
                    for jrow in range((MOE_PARTS - 1) * part, cap)]
        else:
            jobs = [functools.partial(scatter_start, e, b_b, 1, jrow)
                    for jrow in range((q - 1) * part, q * part)]
            jobs += [functools.partial(gather_start, e_next, b_n, 0, jrow)
                     for jrow in range(rows_next[q - 1], rows_next[q])]
        if q == 1:
            wait_scatter(0)
        compute_part(1, 1, q, spread(jobs), weight_jobs(MOE_PARTS + q))

    for jrow in range((MOE_PARTS - 1) * part, cap):
        scatter_start(e, b_b, 1, jrow)
    wait_scatter(1)

    @pl.when(last)
    def _():
        wait_gather(0)
        unit_copy(0, tgt_next, i_next).wait()


def _moe_call(idx_flat, gcol, g2, wg, wu, wd, fh, *, batch, cap):
    m, d = fh.shape[0], fh.shape[1] // 2
    n_tok = m // batch
    assert batch % 2 == 0 and cap % MOE_PARTS == 0
    n_pairs = batch // 2
    assert n_pairs in (1, 2)
    kern = functools.partial(_moe_kernel, cap=cap, n_tok=n_tok, n_pairs=n_pairs)
    grid_spec = pltpu.PrefetchScalarGridSpec(
        num_scalar_prefetch=1,
        grid=(N_EXPERTS, n_pairs),
        in_specs=[pl.BlockSpec((2, 1, cap, LANES), lambda e, i, idx: (i, e, 0, 0)),
                  pl.BlockSpec((batch, 1, d), lambda e, i, idx: (0, 0, 0)),
                  pl.BlockSpec(memory_space=pl.ANY), pl.BlockSpec(memory_space=pl.ANY),
                  pl.BlockSpec(memory_space=pl.ANY), pl.BlockSpec(memory_space=pl.ANY)],
        out_specs=pl.BlockSpec(memory_space=pl.ANY),
        scratch_shapes=[pltpu.VMEM((cap, 2 * d), F32), pltpu.VMEM((cap, 2 * d), F32),
                        pltpu.VMEM((2, d, EXPERT_FF), BF16), pltpu.VMEM((2, d, EXPERT_FF), BF16),
                        pltpu.VMEM((2, EXPERT_FF, d), BF16),
                        pltpu.VMEM((2, MOE_WROWS_A, EXPERT_FF), F32),
                        pltpu.VMEM((2, MOE_WROWS_D, d), F32),
                        pltpu.SemaphoreType.DMA((6,))],
    )
    return pl.pallas_call(
        kern,
        grid_spec=grid_spec,
        out_shape=jax.ShapeDtypeStruct((m, 2 * d), F32),
        input_output_aliases={6: 0},
        compiler_params=_cparams(("arbitrary", "arbitrary")),
        name="moe",
    )(idx_flat, gcol, g2, wg, wu, wd, fh)


def _final_kernel(h_ref, w_ref, o_ref):
    h = h_ref[...]
    ms = jnp.mean(h * h, axis=-1, keepdims=True)
    o_ref[...] = h * lax.rsqrt(ms + EPS) * w_ref[...]


def _final_call(fh, w, *, tm):
    m, d = fh.shape[0], fh.shape[1] // 2
    return pl.pallas_call(
        _final_kernel,
        grid=(m // tm,),
        in_specs=[pl.BlockSpec((tm, d), lambda i: (i, 1)), pl.BlockSpec((1, d), lambda i: (0, 0))],
        out_specs=pl.BlockSpec((tm, d), lambda i: (i, 0)),
        out_shape=jax.ShapeDtypeStruct((m, d), F32),
        compiler_params=_cparams(("arbitrary",)),
        name="final_norm",
    )(fh, w)


def _pick_tile(n, pref):
    t = pref
    while n % t:
        t //= 2
    return t


def _ssd_params(a_log, dt_bias, d_skip, norm_w):
    padw = LANES - N_DT
    alog = jnp.pad(a_log.astype(F32), (0, padw))
    bias = jnp.pad(dt_bias.astype(F32), (0, padw))
    lane = jnp.arange(LANES)[:, None]
    head = jnp.arange(D_SSM)[None, :] // SSM_HEAD_DIM
    return {
        "alog_row": alog[None, :], "alog_col": alog[:, None],
        "bias_row": bias[None, :], "bias_col": bias[:, None],
        "expand_fwd": (lane == head).astype(BF16),
        "expand_rev": (lane == head + SSM_HEADS).astype(BF16),
        "dskip": jnp.repeat(d_skip.astype(F32), SSM_HEAD_DIM)[None, :],
        "norm_w": norm_w.astype(F32)[None, :],
    }


def kernel(x, c, ctx, c_ctx, w_mod, b_mod, norm_mix_w, w_in, ssm_conv_w, ssm_conv_b, ssm_a_log,
           ssm_dt_bias, ssm_d, ssm_norm_w, sc_conv_w, sc_norm_w, w_out, norm_ffn_w, w_router,
           w_gate, w_up, w_down, final_norm_w):
    batch, n_lat, d = x.shape
    n_ctx = ctx.shape[1]
    depth = w_mod.shape[0]
    assert depth == 1 and d == D_MODEL and n_ctx == BLK and n_lat % BLK == 0
    layer = 0
    cap = EC_CAPACITY_FACTOR * n_lat // N_EXPERTS

    pad_rows = -(batch + 1) % 8
    cvec = jnp.concatenate([c, c_ctx[None, :], jnp.zeros((pad_rows, d), F32)], axis=0)
    mod = _mod_call(cvec, w_mod[layer], b_mod[layer][None, :])
    sh1, sc1, g1, sh2, sc2, g2 = [mod[:batch, k * d:(k + 1) * d][:, None, :] for k in range(N_MOD)]
    sh1_c = mod[batch:batch + 1, 0:d][:, None, :]
    sc1_c = mod[batch:batch + 1, d:2 * d][:, None, :]

    w_inT = w_in[layer].T.astype(BF16)
    tc = 512
    w_out_b = w_out[layer].astype(BF16)
    w_rT = w_router[layer].T.astype(BF16)
    wg, wu, wd = w_gate[layer], w_up[layer], w_down[layer]
    norm_mix = norm_mix_w[layer][None, :]
    conv_w, conv_b = ssm_conv_w[layer], ssm_conv_b[layer][None, :]
    prm = _ssd_params(ssm_a_log[layer], ssm_dt_bias[layer], ssm_d[layer], ssm_norm_w[layer])

    ctx2d = ctx.reshape(batch * n_ctx, d)
    xbc_c, dt_c, dtT_c = _inproj_zx_call(
        ctx2d, sh1_c, sc1_c, norm_mix, w_inT, conv_w, conv_b,
        with_z=False, period=n_ctx, tm=_pick_tile(batch * n_ctx, 1024), rows_per_mod=batch * n_ctx)
    zero_state = jnp.zeros((batch, SSM_GROUPS, SSM_STATE, GROUP_W), F32)
    st_f = _ssd_call(xbc_c, dt_c, dtT_c, prm, zero_state, batch=batch, nblk=1, reverse=False,
                     mode="state")
    st_r = _ssd_call(xbc_c, dt_c, dtT_c, prm, zero_state, batch=batch, nblk=1, reverse=True,
                     mode="state")

    x2d = x.reshape(batch * n_lat, d)
    tm = _pick_tile(n_lat, 1024)
    z, xbc, dt, dtT, a_lat = _inproj_zx_call(
        x2d, sh1, sc1, norm_mix, w_inT, conv_w, conv_b,
        with_z=True, period=GRID_W, tm=tm, rows_per_mod=n_lat)
    y_sc = _inproj_sc_call(a_lat, w_inT, sc_conv_w[layer], tm=tm, tc=tc)
    nblk = n_lat // BLK
    y_rev = _ssd_call(xbc, dt, dtT, prm, st_r, batch=batch, nblk=nblk, reverse=True, mode="partial")
    y_ssm = _ssd_call(xbc, dt, dtT, prm, st_f, batch=batch, nblk=nblk, reverse=False, mode="final",
                      extra=(z, y_rev))

    fh, aff = _outproj_call(
        y_ssm, y_sc, sc_norm_w[layer][None, :], w_out_b, x2d, g1, norm_ffn_w[layer][None, :],
        sh2, sc2, w_rT, batch=batch, tm=_pick_tile(n_lat, 512))
    idx, gcol = _route_call(aff, cap=cap)
    fh = _moe_call(idx.reshape(-1), gcol, g2, wg, wu, wd, fh, batch=batch, cap=cap)
    out = _final_call(fh, final_norm_w[None, :], tm=_pick_tile(batch * n_lat, 512))
    return out.reshape(batch, n_lat, d)
```

```python
import functools

import jax
import jax.numpy as jnp
from jax import lax
from jax.experimental import pallas as pl
from jax.experimental.pallas import tpu as pltpu

F32 = jnp.float32
BF16 = jnp.bfloat16
I32 = jnp.int32

D_MODEL = 2048
GRID_W = 64
D_SSM = 2048
SSM_HEAD_DIM = 64
SSM_HEADS = D_SSM // SSM_HEAD_DIM
SSM_GROUPS = 4
SSM_HPG = SSM_HEADS // SSM_GROUPS
SSM_STATE = 128
SSD_CHUNK = 128
GN = SSM_GROUPS * SSM_STATE
D_XBC = D_SSM + 2 * GN
D_ZX = D_SSM + D_XBC
N_DT = 2 * SSM_HEADS
D_SSM_IN = D_ZX + N_DT
D_SC = 2048
N_EXPERTS = 16
EC_CAPACITY_FACTOR = 2
EXPERT_FF = 1024
N_MOD = 6
EPS = 1e-6
LOG2E = 1.4426950408889634

LANES = 128
BLK = 2 * SSD_CHUNK
GROUP_W = SSM_HPG * SSM_HEAD_DIM
VMEM_LIMIT = 56 * 1024 * 1024


def _cparams(sem):
    return pltpu.CompilerParams(dimension_semantics=sem, vmem_limit_bytes=VMEM_LIMIT)


def _silu(v):
    return v * jax.nn.sigmoid(v)


def _softplus(v):
    return jnp.maximum(v, 0.0) + jnp.log1p(jnp.exp(-jnp.abs(v)))


def _mod_kernel(c_ref, w_ref, b_ref, o_ref):
    s = _silu(c_ref[...]).astype(BF16)
    o_ref[...] = jnp.dot(s, w_ref[...].astype(BF16), preferred_element_type=F32) + b_ref[...]


def _mod_call(cvec, w_mod, b_mod):
    rows, d = cvec.shape
    n = w_mod.shape[1]
    tn = 1024
    return pl.pallas_call(
        _mod_kernel,
        grid=(n // tn,),
        in_specs=[pl.BlockSpec((rows, d), lambda j: (0, 0)),
                  pl.BlockSpec((d, tn), lambda j: (0, j)),
                  pl.BlockSpec((1, tn), lambda j: (0, j))],
        out_specs=pl.BlockSpec((rows, tn), lambda j: (0, j)),
        out_shape=jax.ShapeDtypeStruct((rows, n), F32),
        compiler_params=_cparams(("arbitrary",)),
        name="mod",
    )(cvec, w_mod, b_mod)


def _dot_nt(a, bt):
    return lax.dot_general(a, bt, (((1,), (1,)), ((), ())), preferred_element_type=F32)


def _norm_modulate(h, nw, shift, scale):
    ms = jnp.mean(h * h, axis=-1, keepdims=True)
    hn = h * lax.rsqrt(ms + EPS) * nw
    return hn * (1.0 + scale) + shift


def _conv3(v, cw, period):
    rows = v.shape[0]
    r = lax.broadcasted_iota(I32, (rows, 1), 0) % period
    prev = jnp.where(r == 0, 0.0, pltpu.roll(v, 1, 0))
    nxt = jnp.where(r == period - 1, 0.0, pltpu.roll(v, rows - 1, 0))
    return prev * cw[0:1] + v * cw[1:2] + nxt * cw[2:3]


def _inproj_zx_kernel(h_ref, sh_ref, sc_ref, nw_ref, w_ref, wdt_ref, cw_ref, cb_ref,
                      *rest, with_z, period, tm, n_z):
    if with_z:
        z_ref, xbc_ref, dt_ref, dtT_ref, ao_ref, a_ref = rest
    else:
        xbc_ref, dt_ref, dtT_ref, a_ref = rest
    n = pl.program_id(1)

    @pl.when(n == 0)
    def _():
        a = _norm_modulate(h_ref[...], nw_ref[...], sh_ref[0], sc_ref[0]).astype(BF16)
        a_ref[...] = a
        if with_z:
            ao_ref[...] = a
        wdt = wdt_ref[...]
        dt_ref[...] = _dot_nt(a, wdt)
        dtT_ref[...] = _dot_nt(wdt, a)

    wb = w_ref[...]

    if with_z:
        @pl.when(n < n_z)
        def _():
            for k in range(tm // BLK):
                rows = slice(k * BLK, (k + 1) * BLK)
                z_ref[rows, :] = _dot_nt(a_ref[rows, :], wb).astype(BF16)

    @pl.when(n >= n_z)
    def _():
        for k in range(tm // BLK):
            rows = slice(k * BLK, (k + 1) * BLK)
            acc = _dot_nt(a_ref[rows, :], wb)
            y = _conv3(acc, cw_ref[...], period) + cb_ref[...]
            xbc_ref[rows, :] = _silu(y).astype(BF16)


def _inproj_zx_call(h2d, shift, scale, norm_w, w_inT, conv_w, conv_b, *,
                    with_z, period, tm, rows_per_mod):
    m, d = h2d.shape
    tn = 1024
    n_z = D_SSM // tn if with_z else 0
    n_x = D_XBC // tn
    n_off = 0 if with_z else D_SSM // tn
    per = rows_per_mod // tm
    kern = functools.partial(_inproj_zx_kernel, with_z=with_z, period=period, tm=tm, n_z=n_z)
    out_shape = [jax.ShapeDtypeStruct((m, D_XBC), BF16),
                 jax.ShapeDtypeStruct((m, LANES), F32),
                 jax.ShapeDtypeStruct((LANES, m), F32)]
    out_specs = [pl.BlockSpec((tm, tn), lambda i, n: (i, jnp.maximum(n - n_z, 0))),
                 pl.BlockSpec((tm, LANES), lambda i, n: (i, 0)),
                 pl.BlockSpec((LANES, tm), lambda i, n: (0, i))]
    if with_z:
        out_shape = ([jax.ShapeDtypeStruct((m, D_SSM), BF16)] + out_shape
                     + [jax.ShapeDtypeStruct((m, d), BF16)])
        out_specs = ([pl.BlockSpec((tm, tn), lambda i, n: (i, jnp.minimum(n, n_z - 1)))] + out_specs
                     + [pl.BlockSpec((tm, d), lambda i, n: (i, 0))])
    return pl.pallas_call(
        kern,
        grid=(m // tm, n_z + n_x),
        in_specs=[pl.BlockSpec((tm, d), lambda i, n: (i, 0)),
                  pl.BlockSpec((1, 1, d), lambda i, n: (i // per, 0, 0)),
                  pl.BlockSpec((1, 1, d), lambda i, n: (i // per, 0, 0)),
                  pl.BlockSpec((1, d), lambda i, n: (0, 0)),
                  pl.BlockSpec((tn, d), lambda i, n: (n + n_off, 0)),
                  pl.BlockSpec((LANES, d), lambda i, n: (D_ZX // LANES, 0)),
                  pl.BlockSpec((3, tn), lambda i, n: (0, jnp.maximum(n - n_z, 0))),
                  pl.BlockSpec((1, tn), lambda i, n: (0, jnp.maximum(n - n_z, 0)))],
        out_specs=out_specs,
        out_shape=out_shape,
        scratch_shapes=[pltpu.VMEM((tm, d), BF16)],
        compiler_params=_cparams(("arbitrary", "arbitrary")),
        name="inproj_zx" if with_z else "inproj_ctx",
    )(h2d, shift, scale, norm_w, w_inT, w_inT, conv_w, conv_b)


def _inproj_sc_kernel(a_ref, wb_ref, wc_ref, wv_ref, cw_ref, o_ref, *, tm):
    wb, wc, wv = wb_ref[...], wc_ref[...], wv_ref[...]
    for k in range(tm // BLK):
        rows = slice(k * BLK, (k + 1) * BLK)
        a = a_ref[rows, :]
        cg = _dot_nt(a, wc)
        hv = _dot_nt(a, wv)
        v = _conv3(cg * hv, cw_ref[...], GRID_W)
        bg = _dot_nt(a, wb)
        o_ref[rows, :] = (bg * v).astype(BF16)


def _inproj_sc_call(a2d, w_inT, conv_w, *, tm, tc):
    m, d = a2d.shape
    kern = functools.partial(_inproj_sc_kernel, tm=tm)

    def wspec(k):
        return pl.BlockSpec((pl.Element(tc), pl.Element(d)),
                            lambda i, n: (pl.multiple_of(D_SSM_IN + k * D_SC + n * tc, 16), 0))

    return pl.pallas_call(
        kern,
        grid=(m // tm, D_SC // tc),
        in_specs=[pl.BlockSpec((tm, d), lambda i, n: (i, 0)),
                  wspec(0), wspec(1), wspec(2),
                  pl.BlockSpec((3, tc), lambda i, n: (0, n))],
        out_specs=pl.BlockSpec((tm, tc), lambda i, n: (i, n)),
        out_shape=jax.ShapeDtypeStruct((m, D_SC), BF16),
        compiler_params=_cparams(("arbitrary", "arbitrary")),
        name="inproj_sc",
    )(a2d, w_inT, w_inT, w_inT, conv_w)


def _ssd_kernel(*refs, reverse, mode, nblk):
    xbc_ref, dt_ref, dtT_ref, alr_ref, alc_ref, br_ref, bc_ref, e_ref, init_ref = refs[:9]
    rest = refs[9:]
    if mode == "state":
        out_ref, s_ref = rest
    elif mode == "partial":
        out_ref, s_ref = rest
    else:
        z_ref, yo_ref, dsk_ref, nw_ref, out_ref, s_ref = rest
    j = pl.program_id(1)

    @pl.when(j == 0)
    def _():
        s_ref[...] = init_ref[0]

    ii = lax.broadcasted_iota(I32, (SSD_CHUNK, SSD_CHUNK), 0)
    jj = lax.broadcasted_iota(I32, (SSD_CHUNK, SSD_CHUNK), 1)
    tri = jnp.where(jj <= ii, 1.0, 0.0).astype(F32)
    triT = jnp.where(ii <= jj, 1.0, 0.0).astype(F32)
    mask = (jj >= ii) if reverse else (ii >= jj)
    lane4 = lax.broadcasted_iota(I32, (SSD_CHUNK, 4 * SSM_HEAD_DIM), 1) // SSM_HEAD_DIM
    aneg_r = -jnp.exp(alr_ref[...]) * LOG2E
    aneg_c = -jnp.exp(alc_ref[...]) * LOG2E
    lane0 = SSM_HEADS if reverse else 0
    hi = lax.Precision.HIGHEST

    def chunk(c):
        rows = slice(c * SSD_CHUNK, (c + 1) * SSD_CHUNK)
        dt = _softplus(dt_ref[rows, :] + br_ref[...])
        dtT = _softplus(dtT_ref[:, rows] + bc_ref[...])
        la = dt * aneg_r
        laT = dtT * aneg_c
        cs = jnp.dot(tri, la, precision=hi, preferred_element_type=F32)
        csT = jnp.dot(laT, triT, precision=hi, preferred_element_type=F32)
        total = cs[SSD_CHUNK - 1:SSD_CHUNK, :]
        log2_dtT = jnp.log(dtT) * LOG2E
        if reverse:
            ecs = cs - la
            col, rowm = -ecs, csT - laT + log2_dtT
            yscale = jnp.exp2(total - ecs)
            w = dt * jnp.exp2(ecs)
        else:
            col, rowm = cs, log2_dtT - csT
            yscale = jnp.exp2(cs)
            w = dt * jnp.exp2(total - cs)
        t1 = total.astype(BF16).astype(F32)
        r1 = total - t1
        t2 = r1.astype(BF16).astype(F32)
        t3 = r1 - t2
        r16 = lax.broadcasted_iota(I32, (16, LANES), 0)
        tail = jnp.where(r16 == 0, t1, jnp.where(r16 == 1, t2, jnp.where(r16 == 2, t3, 0.0)))
        stack = jnp.concatenate([yscale, w, tail], axis=0).astype(BF16)
        ex = jnp.dot(stack, e_ref[...], preferred_element_type=F32)
        yscale_x = ex[0:SSD_CHUNK]
        w_x = ex[SSD_CHUNK:2 * SSD_CHUNK]
        tb = 2 * SSD_CHUNK
        sdec_x = jnp.exp2(ex[tb:tb + 1] + ex[tb + 1:tb + 2] + ex[tb + 2:tb + 3])

        for g in range(SSM_GROUPS):
            gsl = slice(g * GROUP_W, (g + 1) * GROUP_W)
            xg = xbc_ref[rows, gsl]
            bg = xbc_ref[rows, D_SSM + g * SSM_STATE:D_SSM + (g + 1) * SSM_STATE]
            s_old = s_ref[g]
            if mode != "state":
                cg = xbc_ref[rows, D_SSM + GN + g * SSM_STATE:D_SSM + GN + (g + 1) * SSM_STATE]
                cb = lax.dot_general(cg, bg, (((1,), (1,)), ((), ())), preferred_element_type=F32)
                y_off = jnp.dot(cg, s_old.astype(BF16), preferred_element_type=F32)
                halves = []
                for q in range(2):
                    xq = xg[:, q * 256:(q + 1) * 256]
                    acc = jnp.zeros((SSD_CHUNK, 256), F32)
                    for r4 in range(4):
                        lane = lane0 + g * SSM_HPG + q * 4 + r4
                        seg = col[:, lane:lane + 1] + rowm[lane:lane + 1, :]
                        dec = jnp.exp2(jnp.where(mask, seg, -jnp.inf))
                        mh = (cb * dec).astype(BF16)
                        xm = jnp.where(lane4 == r4, xq, jnp.zeros_like(xq))
                        acc = acc + jnp.dot(mh, xm, preferred_element_type=F32)
                    halves.append(acc)
                y = jnp.concatenate(halves, axis=1) + yscale_x[:, gsl] * y_off
                if mode == "partial":
                    out_ref[rows, gsl] = y.astype(BF16)
                else:
                    y = y + yo_ref[rows, gsl].astype(F32) + dsk_ref[:, gsl] * xg.astype(F32)
                    y = y * _silu(z_ref[rows, gsl].astype(F32))
                    ms = jnp.mean(y * y, axis=-1, keepdims=True)
                    out_ref[rows, gsl] = (y * lax.rsqrt(ms + EPS) * nw_ref[:, gsl]).astype(BF16)
            xw = (xg.astype(F32) * w_x[:, gsl]).astype(BF16)
            upd = lax.dot_general(bg, xw, (((0,), (0,)), ((), ())), preferred_element_type=F32)
            s_ref[g] = sdec_x[:, gsl] * s_old + upd

    for c in ((1, 0) if reverse else (0, 1)):
        chunk(c)

    if mode == "state":
        @pl.when(j == nblk - 1)
        def _():
            out_ref[0] = s_ref[...]


def _ssd_call(xbc, dt, dtT, prm, init, *, batch, nblk, reverse, mode, extra=()):
    m = xbc.shape[0]

    def blk(b, j):
        return b * nblk + (nblk - 1 - j if reverse else j)

    state_spec = pl.BlockSpec((1, SSM_GROUPS, SSM_STATE, GROUP_W), lambda b, j: (b, 0, 0, 0))
    const2 = lambda b, j: (0, 0)
    in_specs = [pl.BlockSpec((BLK, D_XBC), lambda b, j: (blk(b, j), 0)),
                pl.BlockSpec((BLK, LANES), lambda b, j: (blk(b, j), 0)),
                pl.BlockSpec((LANES, BLK), lambda b, j: (0, blk(b, j))),
                pl.BlockSpec((1, LANES), const2), pl.BlockSpec((LANES, 1), const2),
                pl.BlockSpec((1, LANES), const2), pl.BlockSpec((LANES, 1), const2),
                pl.BlockSpec((LANES, D_SSM), const2),
                state_spec]
    args = [xbc, dt, dtT, prm["alog_row"], prm["alog_col"], prm["bias_row"], prm["bias_col"],
            prm["expand_rev" if reverse else "expand_fwd"], init]
    if mode == "state":
        out_specs = state_spec
        out_shape = jax.ShapeDtypeStruct((batch, SSM_GROUPS, SSM_STATE, GROUP_W), F32)
    else:
        out_specs = pl.BlockSpec((BLK, D_SSM), lambda b, j: (blk(b, j), 0))
        out_shape = jax.ShapeDtypeStruct((m, D_SSM), BF16)
    if mode == "final":
        z, y_other = extra
        in_specs += [pl.BlockSpec((BLK, D_SSM), lambda b, j: (blk(b, j), 0)),
                     pl.BlockSpec((BLK, D_SSM), lambda b, j: (blk(b, j), 0)),
                     pl.BlockSpec((1, D_SSM), const2), pl.BlockSpec((1, D_SSM), const2)]
        args += [z, y_other, prm["dskip"], prm["norm_w"]]
    kern = functools.partial(_ssd_kernel, reverse=reverse, mode=mode, nblk=nblk)
    return pl.pallas_call(
        kern,
        grid=(batch, nblk),
        in_specs=in_specs,
        out_specs=out_specs,
        out_shape=out_shape,
        scratch_shapes=[pltpu.VMEM((SSM_GROUPS, SSM_STATE, GROUP_W), F32)],
        compiler_params=_cparams(("arbitrary", "arbitrary")),
        name=f"ssd_{mode}_{'rev' if reverse else 'fwd'}",
    )(*args)


def _outproj_kernel(ys_ref, yc_ref, scw_ref, w_ref, x_ref, g1_ref, nw_ref, sh_ref, sc_ref, wr_ref,
                    fh_ref, aff_ref, *, tm, sub):
    for k in range(tm // sub):
        rows = slice(k * sub, (k + 1) * sub)
        yc = yc_ref[rows, :].astype(F32)
        ms = jnp.mean(yc * yc, axis=-1, keepdims=True)
        ycn = (yc * lax.rsqrt(ms + EPS) * scw_ref[...]).astype(BF16)
        mix = jnp.dot(ys_ref[rows, :], w_ref[0:D_SSM, :], preferred_element_type=F32)
        mix = mix + jnp.dot(ycn, w_ref[D_SSM:, :], preferred_element_type=F32)
        h = x_ref[rows, :] + g1_ref[0] * mix
        fh_ref[rows, D_MODEL:] = h
        f = _norm_modulate(h, nw_ref[...], sh_ref[0], sc_ref[0])
        fh_ref[rows, :D_MODEL] = f
        logits = _dot_nt(wr_ref[...], f.astype(BF16))
        mx = jnp.max(logits, axis=0, keepdims=True)
        ex = jnp.exp(logits - mx)
        aff_ref[0, :, rows] = ex / jnp.sum(ex, axis=0, keepdims=True)


def _outproj_call(y_ssm, y_sc, sc_norm_w, w_out, x2d, g1, norm_w, shift, scale, w_rT, *, batch, tm):
    m, d = x2d.shape
    n_tok = m // batch
    per = n_tok // tm
    vec = lambda i: (i // per, 0, 0)
    const2 = lambda i: (0, 0)
    sub = min(tm, 256)
    return pl.pallas_call(
        functools.partial(_outproj_kernel, tm=tm, sub=sub),
        grid=(m // tm,),
        in_specs=[pl.BlockSpec((tm, D_SSM), lambda i: (i, 0)),
                  pl.BlockSpec((tm, D_SC), lambda i: (i, 0)),
                  pl.BlockSpec((1, D_SC), const2),
                  pl.BlockSpec((D_SSM + D_SC, d), const2, pipeline_mode=pl.Buffered(1)),
                  pl.BlockSpec((tm, d), lambda i: (i, 0)),
                  pl.BlockSpec((1, 1, d), vec),
                  pl.BlockSpec((1, d), const2),
                  pl.BlockSpec((1, 1, d), vec),
                  pl.BlockSpec((1, 1, d), vec),
                  pl.BlockSpec((N_EXPERTS, d), const2)],
        out_specs=[pl.BlockSpec((tm, 2 * d), lambda i: (i, 0)),
                   pl.BlockSpec((1, N_EXPERTS, tm), lambda i: (i // per, 0, i % per))],
        out_shape=[jax.ShapeDtypeStruct((m, 2 * d), F32),
                   jax.ShapeDtypeStruct((batch, N_EXPERTS, n_tok), F32)],
        compiler_params=_cparams(("arbitrary",)),
        name="outproj",
    )(y_ssm, y_sc, sc_norm_w, w_out, x2d, g1, norm_w, shift, scale, w_rT)


def _prefix_lanes(m01):
    rows, n = m01.shape
    ki = lax.broadcasted_iota(I32, (LANES, LANES), 0)
    ci = lax.broadcasted_iota(I32, (LANES, LANES), 1)
    triu = jnp.where(ki <= ci, 1.0, 0.0).astype(BF16)
    off = jnp.zeros((rows, 1), F32)
    parts = []
    for k in range(n // LANES):
        p = jnp.dot(m01[:, k * LANES:(k + 1) * LANES].astype(BF16), triu,
                    preferred_element_type=F32)
        parts.append(p + off)
        off = off + p[:, LANES - 1:LANES]
    return jnp.concatenate(parts, axis=1)


def _route_kernel(aff_ref, idx_ref, gcol_ref, *, cap):
    a = aff_ref[0]
    n_e, n = a.shape
    thr = jnp.zeros((n_e, 1), I32)
    for bit in range(30, -1, -1):
        cand = thr | (1 << bit)
        cand_f = lax.bitcast_convert_type(cand, F32)
        cnt = jnp.sum(jnp.where(a >= cand_f, 1.0, 0.0), axis=1, keepdims=True)
        thr = jnp.where(cnt >= cap, cand, thr)
    thr_f = lax.bitcast_convert_type(thr, F32)
    gt = a > thr_f
    eq = a == thr_f
    need = cap - jnp.sum(jnp.where(gt, 1.0, 0.0), axis=1, keepdims=True)
    eqf = jnp.where(eq, 1.0, 0.0)
    ties_before = _prefix_lanes(eqf) - eqf
    sel = jnp.logical_or(gt, jnp.logical_and(eq, ties_before < need))
    self_ = jnp.where(sel, 1.0, 0.0)
    slot = jnp.where(sel, _prefix_lanes(self_) - 1.0, -1.0).astype(I32)

    tok = lax.broadcasted_iota(I32, (1, n), 1)
    t_hi = (tok // 64).astype(F32)
    t_lo = (tok % 64).astype(F32)
    a1 = a.astype(BF16).astype(F32)
    ra = a - a1
    a2 = ra.astype(BF16).astype(F32)
    a3 = ra - a2
    srow = lax.broadcasted_iota(I32, (cap, n), 0)
    r8 = lax.broadcasted_iota(I32, (8, n), 0)
    zpad = jnp.zeros((LANES - 8, n), F32)
    for e in range(n_e):
        onehot = jnp.where(slot[e:e + 1, :] == srow, 1.0, 0.0).astype(BF16)
        v8 = jnp.where(r8 == 0, t_hi, jnp.where(r8 == 1, t_lo, jnp.where(
            r8 == 2, a1[e:e + 1], jnp.where(r8 == 3, a2[e:e + 1], jnp.where(
                r8 == 4, a3[e:e + 1], 0.0)))))
        vals = jnp.concatenate([v8, zpad], axis=0).astype(BF16)
        o_col = lax.dot_general(onehot, vals, (((1,), (1,)), ((), ())),
                                preferred_element_type=F32)
        o_row = o_col.T
        idx_ref[0, e:e + 1, :] = (o_row[0:1] * 64.0 + o_row[1:2]).astype(I32)
        gcol_ref[0, e] = o_col


def _route_call(aff, *, cap):
    batch, n_e, n = aff.shape
    return pl.pallas_call(
        functools.partial(_route_kernel, cap=cap),
        grid=(batch,),
        in_specs=[pl.BlockSpec((1, n_e, n), lambda b: (b, 0, 0))],
        out_specs=[pl.BlockSpec((1, n_e, cap), lambda b: (b, 0, 0)),
                   pl.BlockSpec((1, n_e, cap, LANES), lambda b: (b, 0, 0, 0))],
        out_shape=[jax.ShapeDtypeStruct((batch, n_e, cap), I32),
                   jax.ShapeDtypeStruct((batch, n_e, cap, LANES), F32)],
        compiler_params=_cparams(("arbitrary",)),
        name="route",
    )(aff)


MOE_PARTS = 4
MOE_NCHUNK = 512
MOE_WROWS_A = 512
MOE_WROWS_D = 128
MOE_WPIECES = 8
MOE_WPRIO = 1


def _moe_kernel(idx_ref, gcol_ref, g2_ref, wg_hbm, wu_hbm, wd_hbm, fh_in, fh_out,
                buf0, buf1, wg_s, wu_s, wd_s, stg_a, stg_d, sem, *, cap, n_tok, n_pairs):
    e = pl.program_id(0)
    i = pl.program_id(1)
    n_e = pl.num_programs(0)
    first = jnp.logical_and(e == 0, i == 0)
    last = jnp.logical_and(e == n_e - 1, i == n_pairs - 1)
    i_next = jnp.where(i == n_pairs - 1, 0, i + 1)
    e_next = jnp.where(i == n_pairs - 1, e + 1, e)
    bufs = (buf0, buf1)
    part = cap // MOE_PARTS
    hcols = pl.ds(D_MODEL, D_MODEL)
    cur = e % 2

    def row_of(ex, b, jrow):
        return b * n_tok + idx_ref[(b * N_EXPERTS + ex) * cap + jrow]

    def gather_start(ex, b, s, jrow):
        pltpu.make_async_copy(fh_in.at[pl.ds(row_of(ex, b, jrow), 1)], bufs[s].at[pl.ds(jrow, 1)],
                              sem.at[s]).start()

    def scatter_start(ex, b, s, jrow):
        pltpu.make_async_copy(bufs[s].at[pl.ds(jrow, 1), hcols],
                              fh_out.at[pl.ds(row_of(ex, b, jrow), 1), hcols],
                              sem.at[2 + s]).start()

    def wait_gather(s):
        pltpu.make_async_copy(fh_in.at[pl.ds(0, cap)], bufs[s], sem.at[s]).wait()

    def wait_scatter(s):
        pltpu.make_async_copy(bufs[s].at[:, hcols], fh_out.at[pl.ds(0, cap), hcols],
                              sem.at[2 + s]).wait()

    g_units = (D_MODEL // MOE_WROWS_A) // n_pairs
    d_units = (EXPERT_FF // MOE_WROWS_D) // n_pairs
    units = [("g", j) for j in range(g_units)] + [("u", j) for j in range(g_units)] \
        + [("d", j) for j in range(d_units)]
    n_parts = 2 * MOE_PARTS
    upp = len(units) // n_parts
    pieces = MOE_WPIECES // upp

    def unit_refs(kind, j, tgt, step, slot):
        if kind == "d":
            row0 = pl.multiple_of((step * d_units + j) * MOE_WROWS_D, MOE_WROWS_D)
            return (wd_hbm.at[tgt, pl.ds(row0, MOE_WROWS_D), :], stg_d.at[slot], wd_s, row0,
                    MOE_WROWS_D)
        src = wg_hbm if kind == "g" else wu_hbm
        row0 = pl.multiple_of((step * g_units + j) * MOE_WROWS_A, MOE_WROWS_A)
        return (src.at[tgt, pl.ds(row0, MOE_WROWS_A), :], stg_a.at[slot],
                wg_s if kind == "g" else wu_s, row0, MOE_WROWS_A)

    def unit_copy(t, tgt, step):
        kind, j = units[t]
        src, dst, _, _, _ = unit_refs(kind, j, tgt, step, t % 2)
        return pltpu.make_async_copy(src, dst, sem.at[4 + t % 2])

    def unit_cast(t, step, wslot, piece, n_pieces):
        kind, j = units[t]
        _, stg, dst, row0, rows = unit_refs(kind, j, 0, step, t % 2)
        pr = rows // n_pieces
        dst[wslot, pl.ds(row0 + piece * pr, pr), :] = stg[piece * pr:(piece + 1) * pr, :].astype(BF16)

    tgt_now = jnp.minimum(e + 1, n_e - 1)
    tgt_next = jnp.minimum(e_next + 1, n_e - 1)

    def weight_jobs(p):
        jobs = []
        for c in range(upp):
            t = p * upp + c
            for piece in range(pieces):
                def job(t=t, piece=piece):
                    if piece == 0:
                        unit_copy(t, tgt_now, i).wait()
                        if t + 1 < len(units):
                            unit_copy(t + 1, tgt_now, i).start(priority=MOE_WPRIO)
                        else:
                            unit_copy(0, tgt_next, i_next).start(priority=MOE_WPRIO)
                    unit_cast(t, i, 1 - cur, piece, pieces)
                jobs.append(job)
        return jobs

    def compute_part(s, b_local, q, between, wjobs):
        rows = slice(q * part, (q + 1) * part)
        xq = bufs[s][rows, 0:D_MODEL].astype(BF16)
        gc = gcol_ref[b_local, 0, rows, :]
        gate = gc[:, 2:3] + gc[:, 3:4] + gc[:, 4:5]
        g2 = g2_ref[2 * i + b_local]
        nc = MOE_NCHUNK
        hid = []
        slot_no = 0

        def tick():
            nonlocal slot_no
            between[slot_no]()
            if slot_no < len(wjobs):
                wjobs[slot_no]()
            slot_no += 1

        for c in range(EXPERT_FF // nc):
            cols = slice(c * nc, (c + 1) * nc)
            tick()
            hg = jnp.dot(xq, wg_s[cur, :, cols], preferred_element_type=F32)
            tick()
            hu = jnp.dot(xq, wu_s[cur, :, cols], preferred_element_type=F32)
            hid.append((_silu(hg) * hu).astype(BF16))
        hid = jnp.concatenate(hid, axis=1)
        for c in range(D_MODEL // nc):
            cols = slice(c * nc, (c + 1) * nc)
            hc = slice(D_MODEL + c * nc, D_MODEL + (c + 1) * nc)
            tick()
            ye = jnp.dot(hid, wd_s[cur, :, cols], preferred_element_type=F32)
            bufs[s][rows, hc] = bufs[s][rows, hc] + g2[:, cols] * (ye * gate)

    n_between = 2 * (EXPERT_FF // MOE_NCHUNK) + D_MODEL // MOE_NCHUNK

    def spread(jobs):
        n = len(jobs)
        cuts = [n * k // n_between for k in range(n_between + 1)]

        def group(k):
            def run():
                for job in jobs[cuts[k]:cuts[k + 1]]:
                    job()
            return run
        return [group(k) for k in range(n_between)]

    b_a, b_b = 2 * i, 2 * i + 1
    b_n = 2 * i_next

    @pl.when(first)
    def _():
        @pl.loop(0, cap)
        def _(jrow):
            gather_start(e, b_a, 0, jrow)
        for step in range(n_pairs):
            for t in range(len(units)):
                kind, j = units[t]
                src, _, _, _, _ = unit_refs(kind, j, 0, step, 0)
                stg0 = stg_d.at[0] if kind == "d" else stg_a.at[0]
                cp = pltpu.make_async_copy(src, stg0, sem.at[4])
                cp.start()
                cp.wait()
                _, _, dst, row0, rows = unit_refs(kind, j, 0, step, 0)
                dst[0, pl.ds(row0, rows), :] = (stg_d[0] if kind == "d" else stg_a[0]).astype(BF16)
        unit_copy(0, tgt_now, i).start(priority=MOE_WPRIO)

    wait_gather(0)
    fill = {1: range(0, cap // 2), 2: range(cap // 2, cap)}
    for q in range(MOE_PARTS):
        jobs = []
        if q > 0:
            jobs += [functools.partial(scatter_start, e, b_a, 0, jrow)
                     for jrow in range((q - 1) * part, q * part)]
        jobs += [functools.partial(gather_start, e, b_b, 1, jrow) for jrow in fill.get(q, ())]
        if q == 1:
            @pl.when(jnp.logical_not(first))
            def _():
                wait_scatter(1)
        compute_part(0, 0, q, spread(jobs), weight_jobs(q))

    wait_gather(1)
    for q in range(MOE_PARTS):
        if q == 0:
            jobs = [functools.partial(scatter_start, e, b_a, 0, jrow)
                    for jrow in range((MOE_PARTS - 1) * part, cap)]
        else:
            jobs = [functools.partial(scatter_start, e, b_b, 1, jrow)
                    for jrow in range((q - 1) * part, q * part)]
        jobs += [functools.partial(gather_start, e_next, b_n, 0, jrow) for jrow in fill.get(q, ())]
        if q == 1:
            wait_scatter(0)
        compute_part(1, 1, q, spread(jobs), weight_jobs(MOE_PARTS + q))

    for jrow in range((MOE_PARTS - 1) * part, cap):
        scatter_start(e, b_b, 1, jrow)

    @pl.when(last)
    def _():
        wait_scatter(1)
        wait_gather(0)
        unit_copy(0, tgt_next, i_next).wait()


def _moe_call(idx_flat, gcol, g2, wg, wu, wd, fh, *, batch, cap):
    m, d = fh.shape[0], fh.shape[1] // 2
    n_tok = m // batch
    assert batch % 2 == 0 and cap % MOE_PARTS == 0
    n_pairs = batch // 2
    assert n_pairs in (1, 2)
    kern = functools.partial(_moe_kernel, cap=cap, n_tok=n_tok, n_pairs=n_pairs)
    grid_spec = pltpu.PrefetchScalarGridSpec(
        num_scalar_prefetch=1,
        grid=(N_EXPERTS, n_pairs),
        in_specs=[pl.BlockSpec((2, 1, cap, LANES), lambda e, i, idx: (i, e, 0, 0)),
                  pl.BlockSpec((batch, 1, d), lambda e, i, idx: (0, 0, 0)),
                  pl.BlockSpec(memory_space=pl.ANY), pl.BlockSpec(memory_space=pl.ANY),
                  pl.BlockSpec(memory_space=pl.ANY), pl.BlockSpec(memory_space=pl.ANY)],
        out_specs=pl.BlockSpec(memory_space=pl.ANY),
        scratch_shapes=[pltpu.VMEM((cap, 2 * d), F32), pltpu.VMEM((cap, 2 * d), F32),
                        pltpu.VMEM((2, d, EXPERT_FF), BF16), pltpu.VMEM((2, d, EXPERT_FF), BF16),
                        pltpu.VMEM((2, EXPERT_FF, d), BF16),
                        pltpu.VMEM((2, MOE_WROWS_A, EXPERT_FF), F32),
                        pltpu.VMEM((2, MOE_WROWS_D, d), F32),
                        pltpu.SemaphoreType.DMA((6,))],
    )
    return pl.pallas_call(
        kern,
        grid_spec=grid_spec,
        out_shape=jax.ShapeDtypeStruct((m, 2 * d), F32),
        input_output_aliases={6: 0},
        compiler_params=_cparams(("arbitrary", "arbitrary")),
        name="moe",
    )(idx_flat, gcol, g2, wg, wu, wd, fh)


def _final_kernel(h_ref, w_ref, o_ref):
    h = h_ref[...]
    ms = jnp.mean(h * h, axis=-1, keepdims=True)
    o_ref[...] = h * lax.rsqrt(ms + EPS) * w_ref[...]


def _final_call(fh, w, *, tm):
    m, d = fh.shape[0], fh.shape[1] // 2
    return pl.pallas_call(
        _final_kernel,
        grid=(m // tm,),
        in_specs=[pl.BlockSpec((tm, d), lambda i: (i, 1)), pl.BlockSpec((1, d), lambda i: (0, 0))],
        out_specs=pl.BlockSpec((tm, d), lambda i: (i, 0)),
        out_shape=jax.ShapeDtypeStruct((m, d), F32),
        compiler_params=_cparams(("arbitrary",)),
        name="final_norm",
    )(fh, w)


def _pick_tile(n, pref):
    t = pref
    while n % t:
        t //= 2
    return t


def _ssd_params(a_log, dt_bias, d_skip, norm_w):
    padw = LANES - N_DT
    alog = jnp.pad(a_log.astype(F32), (0, padw))
    bias = jnp.pad(dt_bias.astype(F32), (0, padw))
    lane = jnp.arange(LANES)[:, None]
    head = jnp.arange(D_SSM)[None, :] // SSM_HEAD_DIM
    return {
        "alog_row": alog[None, :], "alog_col": alog[:, None],
        "bias_row": bias[None, :], "bias_col": bias[:, None],
        "expand_fwd": (lane == head).astype(BF16),
        "expand_rev": (lane == head + SSM_HEADS).astype(BF16),
        "dskip": jnp.repeat(d_skip.astype(F32), SSM_HEAD_DIM)[None, :],
        "norm_w": norm_w.astype(F32)[None, :],
    }


def kernel(x, c, ctx, c_ctx, w_mod, b_mod, norm_mix_w, w_in, ssm_conv_w, ssm_conv_b, ssm_a_log,
           ssm_dt_bias, ssm_d, ssm_norm_w, sc_conv_w, sc_norm_w, w_out, norm_ffn_w, w_router,
           w_gate, w_up, w_down, final_norm_w):
    batch, n_lat, d = x.shape
    n_ctx = ctx.shape[1]
    depth = w_mod.shape[0]
    assert depth == 1 and d == D_MODEL and n_ctx == BLK and n_lat % BLK == 0
    layer = 0
    cap = EC_CAPACITY_FACTOR * n_lat // N_EXPERTS

    pad_rows = -(batch + 1) % 8
    cvec = jnp.concatenate([c, c_ctx[None, :], jnp.zeros((pad_rows, d), F32)], axis=0)
    mod = _mod_call(cvec, w_mod[layer], b_mod[layer][None, :])
    sh1, sc1, g1, sh2, sc2, g2 = [mod[:batch, k * d:(k + 1) * d][:, None, :] for k in range(N_MOD)]
    sh1_c = mod[batch:batch + 1, 0:d][:, None, :]
    sc1_c = mod[batch:batch + 1, d:2 * d][:, None, :]

    w_inT = w_in[layer].T.astype(BF16)
    tc = 512
    w_out_b = w_out[layer].astype(BF16)
    w_rT = w_router[layer].T.astype(BF16)
    wg, wu, wd = w_gate[layer], w_up[layer], w_down[layer]
    norm_mix = norm_mix_w[layer][None, :]
    conv_w, conv_b = ssm_conv_w[layer], ssm_conv_b[layer][None, :]
    prm = _ssd_params(ssm_a_log[layer], ssm_dt_bias[layer], ssm_d[layer], ssm_norm_w[layer])

    ctx2d = ctx.reshape(batch * n_ctx, d)
    xbc_c, dt_c, dtT_c = _inproj_zx_call(
        ctx2d, sh1_c, sc1_c, norm_mix, w_inT, conv_w, conv_b,
        with_z=False, period=n_ctx, tm=_pick_tile(batch * n_ctx, 1024), rows_per_mod=batch * n_ctx)
    zero_state = jnp.zeros((batch, SSM_GROUPS, SSM_STATE, GROUP_W), F32)
    st_f = _ssd_call(xbc_c, dt_c, dtT_c, prm, zero_state, batch=batch, nblk=1, reverse=False,
                     mode="state")
    st_r = _ssd_call(xbc_c, dt_c, dtT_c, prm, zero_state, batch=batch, nblk=1, reverse=True,
                     mode="state")

    x2d = x.reshape(batch * n_lat, d)
    tm = _pick_tile(n_lat, 1024)
    z, xbc, dt, dtT, a_lat = _inproj_zx_call(
        x2d, sh1, sc1, norm_mix, w_inT, conv_w, conv_b,
        with_z=True, period=GRID_W, tm=tm, rows_per_mod=n_lat)
    y_sc = _inproj_sc_call(a_lat, w_inT, sc_conv_w[layer], tm=tm, tc=tc)
    nblk = n_lat // BLK
    y_rev = _ssd_call(xbc, dt, dtT, prm, st_r, batch=batch, nblk=nblk, reverse=True, mode="partial")
    y_ssm = _ssd_call(xbc, dt, dtT, prm, st_f, batch=batch, nblk=nblk, reverse=False, mode="final",
                      extra=(z, y_rev))

    fh, aff = _outproj_call(
        y_ssm, y_sc, sc_norm_w[layer][None, :], w_out_b, x2d, g1, norm_ffn_w[layer][None, :],
        sh2, sc2, w_rT, batch=batch, tm=_pick_tile(n_lat, 512))
    idx, gcol = _route_call(aff, cap=cap)
    fh = _moe_call(idx.reshape(-1), gcol, g2, wg, wu, wd, fh, batch=batch, cap=cap)
    out = _final_call(fh, final_norm_w[None, :], tm=_pick_tile(batch * n_lat, 512))
    return out.reshape(batch, n_lat, d)
```

```python
import functools

import jax
import jax.numpy as jnp
from jax import lax
from jax.experimental import pallas as pl
from jax.experimental.pallas import tpu as pltpu

F32 = jnp.float32
BF16 = jnp.bfloat16
I32 = jnp.int32

D_MODEL = 2048
GRID_W = 64
D_SSM = 2048
SSM_HEAD_DIM = 64
SSM_HEADS = D_SSM // SSM_HEAD_DIM
SSM_GROUPS = 4
SSM_HPG = SSM_HEADS // SSM_GROUPS
SSM_STATE = 128
SSD_CHUNK = 128
GN = SSM_GROUPS * SSM_STATE
D_XBC = D_SSM + 2 * GN
D_ZX = D_SSM + D_XBC
N_DT = 2 * SSM_HEADS
D_SSM_IN = D_ZX + N_DT
D_SC = 2048
N_EXPERTS = 16
EC_CAPACITY_FACTOR = 2
EXPERT_FF = 1024
N_MOD = 6
EPS = 1e-6
LOG2E = 1.4426950408889634

LANES = 128
BLK = 2 * SSD_CHUNK
GROUP_W = SSM_HPG * SSM_HEAD_DIM
VMEM_LIMIT = 56 * 1024 * 1024


def _cparams(sem):
    return pltpu.CompilerParams(dimension_semantics=sem, vmem_limit_bytes=VMEM_LIMIT)


def _silu(v):
    return v * jax.nn.sigmoid(v)


def _softplus(v):
    return jnp.maximum(v, 0.0) + jnp.log1p(jnp.exp(-jnp.abs(v)))


def _mod_kernel(c_ref, w_ref, b_ref, o_ref):
    s = _silu(c_ref[...]).astype(BF16)
    o_ref[...] = jnp.dot(s, w_ref[...].astype(BF16), preferred_element_type=F32) + b_ref[...]


def _mod_call(cvec, w_mod, b_mod):
    rows, d = cvec.shape
    n = w_mod.shape[1]
    tn = 1024
    return pl.pallas_call(
        _mod_kernel,
        grid=(n // tn,),
        in_specs=[pl.BlockSpec((rows, d), lambda j: (0, 0)),
                  pl.BlockSpec((d, tn), lambda j: (0, j)),
                  pl.BlockSpec((1, tn), lambda j: (0, j))],
        out_specs=pl.BlockSpec((rows, tn), lambda j: (0, j)),
        out_shape=jax.ShapeDtypeStruct((rows, n), F32),
        compiler_params=_cparams(("arbitrary",)),
        name="mod",
    )(cvec, w_mod, b_mod)


def _dot_nt(a, bt):
    return lax.dot_general(a, bt, (((1,), (1,)), ((), ())), preferred_element_type=F32)


def _norm_modulate(h, nw, shift, scale):
    ms = jnp.mean(h * h, axis=-1, keepdims=True)
    hn = h * lax.rsqrt(ms + EPS) * nw
    return hn * (1.0 + scale) + shift


def _conv3(v, cw, period):
    rows = v.shape[0]
    r = lax.broadcasted_iota(I32, (rows, 1), 0) % period
    prev = jnp.where(r == 0, 0.0, pltpu.roll(v, 1, 0))
    nxt = jnp.where(r == period - 1, 0.0, pltpu.roll(v, rows - 1, 0))
    return prev * cw[0:1] + v * cw[1:2] + nxt * cw[2:3]


def _inproj_zx_kernel(h_ref, sh_ref, sc_ref, nw_ref, w_ref, wdt_ref, cw_ref, cb_ref, br_ref, bc_ref,
                      *rest, with_z, period, tm, n_z):
    if with_z:
        z_ref, xbc_ref, dt_ref, dtT_ref, ao_ref, a_ref = rest
    else:
        xbc_ref, dt_ref, dtT_ref, a_ref = rest
    n = pl.program_id(1)

    @pl.when(n == 0)
    def _():
        a = _norm_modulate(h_ref[...], nw_ref[...], sh_ref[0], sc_ref[0]).astype(BF16)
        a_ref[...] = a
        if with_z:
            ao_ref[...] = a
        wdt = wdt_ref[...]
        dt_ref[...] = _softplus(_dot_nt(a, wdt) + br_ref[...])
        dtT_ref[...] = _softplus(_dot_nt(wdt, a) + bc_ref[...])

    wb = w_ref[...]

    if with_z:
        @pl.when(n < n_z)
        def _():
            for k in range(tm // BLK):
                rows = slice(k * BLK, (k + 1) * BLK)
                z_ref[rows, :] = _dot_nt(a_ref[rows, :], wb).astype(BF16)

    @pl.when(n >= n_z)
    def _():
        for k in range(tm // BLK):
            rows = slice(k * BLK, (k + 1) * BLK)
            acc = _dot_nt(a_ref[rows, :], wb)
            y = _conv3(acc, cw_ref[...], period) + cb_ref[...]
            xbc_ref[rows, :] = _silu(y).astype(BF16)


def _inproj_zx_call(h2d, shift, scale, norm_w, w_inT, conv_w, conv_b, bias_row, bias_col, *,
                    with_z, period, tm, rows_per_mod):
    m, d = h2d.shape
    tn = 1024
    n_z = D_SSM // tn if with_z else 0
    n_x = D_XBC // tn
    n_off = 0 if with_z else D_SSM // tn
    per = rows_per_mod // tm
    kern = functools.partial(_inproj_zx_kernel, with_z=with_z, period=period, tm=tm, n_z=n_z)
    out_shape = [jax.ShapeDtypeStruct((m, D_XBC), BF16),
                 jax.ShapeDtypeStruct((m, LANES), F32),
                 jax.ShapeDtypeStruct((LANES, m), F32)]
    out_specs = [pl.BlockSpec((tm, tn), lambda i, n: (i, jnp.maximum(n - n_z, 0))),
                 pl.BlockSpec((tm, LANES), lambda i, n: (i, 0)),
                 pl.BlockSpec((LANES, tm), lambda i, n: (0, i))]
    if with_z:
        out_shape = ([jax.ShapeDtypeStruct((m, D_SSM), BF16)] + out_shape
                     + [jax.ShapeDtypeStruct((m, d), BF16)])
        out_specs = ([pl.BlockSpec((tm, tn), lambda i, n: (i, jnp.minimum(n, n_z - 1)))] + out_specs
                     + [pl.BlockSpec((tm, d), lambda i, n: (i, 0))])
    return pl.pallas_call(
        kern,
        grid=(m // tm, n_z + n_x),
        in_specs=[pl.BlockSpec((tm, d), lambda i, n: (i, 0)),
                  pl.BlockSpec((1, 1, d), lambda i, n: (i // per, 0, 0)),
                  pl.BlockSpec((1, 1, d), lambda i, n: (i // per, 0, 0)),
                  pl.BlockSpec((1, d), lambda i, n: (0, 0)),
                  pl.BlockSpec((tn, d), lambda i, n: (n + n_off, 0)),
                  pl.BlockSpec((LANES, d), lambda i, n: (D_ZX // LANES, 0)),
                  pl.BlockSpec((3, tn), lambda i, n: (0, jnp.maximum(n - n_z, 0))),
                  pl.BlockSpec((1, tn), lambda i, n: (0, jnp.maximum(n - n_z, 0))),
                  pl.BlockSpec((1, LANES), lambda i, n: (0, 0)),
                  pl.BlockSpec((LANES, 1), lambda i, n: (0, 0))],
        out_specs=out_specs,
        out_shape=out_shape,
        scratch_shapes=[pltpu.VMEM((tm, d), BF16)],
        compiler_params=_cparams(("arbitrary", "arbitrary")),
        name="inproj_zx" if with_z else "inproj_ctx",
    )(h2d, shift, scale, norm_w, w_inT, w_inT, conv_w, conv_b, bias_row, bias_col)


def _inproj_sc_kernel(a_ref, wb_ref, wc_ref, wv_ref, cw_ref, o_ref, *, tm):
    wb, wc, wv = wb_ref[...], wc_ref[...], wv_ref[...]
    for k in range(tm // BLK):
        rows = slice(k * BLK, (k + 1) * BLK)
        a = a_ref[rows, :]
        cg = _dot_nt(a, wc)
        hv = _dot_nt(a, wv)
        v = _conv3(cg * hv, cw_ref[...], GRID_W)
        bg = _dot_nt(a, wb)
        o_ref[rows, :] = (bg * v).astype(BF16)


def _inproj_sc_call(a2d, w_inT, conv_w, *, tm, tc):
    m, d = a2d.shape
    kern = functools.partial(_inproj_sc_kernel, tm=tm)

    def wspec(k):
        return pl.BlockSpec((pl.Element(tc), pl.Element(d)),
                            lambda i, n: (pl.multiple_of(D_SSM_IN + k * D_SC + n * tc, 16), 0))

    return pl.pallas_call(
        kern,
        grid=(m // tm, D_SC // tc),
        in_specs=[pl.BlockSpec((tm, d), lambda i, n: (i, 0)),
                  wspec(0), wspec(1), wspec(2),
                  pl.BlockSpec((3, tc), lambda i, n: (0, n))],
        out_specs=pl.BlockSpec((tm, tc), lambda i, n: (i, n)),
        out_shape=jax.ShapeDtypeStruct((m, D_SC), BF16),
        compiler_params=_cparams(("arbitrary", "arbitrary")),
        name="inproj_sc",
    )(a2d, w_inT, w_inT, w_inT, conv_w)


def _ssd_kernel(*refs, reverse, mode, nblk):
    xbc_ref, dt_ref, dtT_ref, alr_ref, alc_ref, e_ref, init_ref = refs[:7]
    rest = refs[7:]
    if mode == "state":
        out_ref, s_ref = rest
    elif mode == "partial":
        out_ref, s_ref = rest
    else:
        z_ref, yo_ref, dsk_ref, nw_ref, out_ref, s_ref = rest
    j = pl.program_id(1)

    @pl.when(j == 0)
    def _():
        s_ref[...] = init_ref[0]

    ii = lax.broadcasted_iota(I32, (SSD_CHUNK, SSD_CHUNK), 0)
    jj = lax.broadcasted_iota(I32, (SSD_CHUNK, SSD_CHUNK), 1)
    tri = jnp.where(jj <= ii, 1.0, 0.0).astype(F32)
    triT = jnp.where(ii <= jj, 1.0, 0.0).astype(F32)
    mask = (jj >= ii) if reverse else (ii >= jj)
    lane4 = lax.broadcasted_iota(I32, (SSD_CHUNK, 4 * SSM_HEAD_DIM), 1) // SSM_HEAD_DIM
    aneg_r = -jnp.exp(alr_ref[...]) * LOG2E
    aneg_c = -jnp.exp(alc_ref[...]) * LOG2E
    lane0 = SSM_HEADS if reverse else 0
    hi = lax.Precision.HIGHEST

    def chunk(c):
        rows = slice(c * SSD_CHUNK, (c + 1) * SSD_CHUNK)
        dt = dt_ref[rows, :]
        dtT = dtT_ref[:, rows]
        la = dt * aneg_r
        laT = dtT * aneg_c
        cs = jnp.dot(tri, la, precision=hi, preferred_element_type=F32)
        csT = jnp.dot(laT, triT, precision=hi, preferred_element_type=F32)
        total = cs[SSD_CHUNK - 1:SSD_CHUNK, :]
        log2_dtT = jnp.log(dtT) * LOG2E
        if reverse:
            ecs = cs - la
            col, rowm = -ecs, csT - laT + log2_dtT
            yscale = jnp.exp2(total - ecs)
            w = dt * jnp.exp2(ecs)
        else:
            col, rowm = cs, log2_dtT - csT
            yscale = jnp.exp2(cs)
            w = dt * jnp.exp2(total - cs)
        t1 = total.astype(BF16).astype(F32)
        r1 = total - t1
        t2 = r1.astype(BF16).astype(F32)
        t3 = r1 - t2
        r16 = lax.broadcasted_iota(I32, (16, LANES), 0)
        tail = jnp.where(r16 == 0, t1, jnp.where(r16 == 1, t2, jnp.where(r16 == 2, t3, 0.0)))
        stack = jnp.concatenate([yscale, w, tail], axis=0).astype(BF16)
        ex = jnp.dot(stack, e_ref[...], preferred_element_type=F32)
        yscale_x = ex[0:SSD_CHUNK]
        w_x = ex[SSD_CHUNK:2 * SSD_CHUNK]
        tb = 2 * SSD_CHUNK
        sdec_x = jnp.exp2(ex[tb:tb + 1] + ex[tb + 1:tb + 2] + ex[tb + 2:tb + 3])

        for g in range(SSM_GROUPS):
            gsl = slice(g * GROUP_W, (g + 1) * GROUP_W)
            xg = xbc_ref[rows, gsl]
            bg = xbc_ref[rows, D_SSM + g * SSM_STATE:D_SSM + (g + 1) * SSM_STATE]
            s_old = s_ref[g]
            if mode != "state":
                cg = xbc_ref[rows, D_SSM + GN + g * SSM_STATE:D_SSM + GN + (g + 1) * SSM_STATE]
                cb = lax.dot_general(cg, bg, (((1,), (1,)), ((), ())), preferred_element_type=F32)
                y_off = jnp.dot(cg, s_old.astype(BF16), preferred_element_type=F32)
                halves = []
                for q in range(2):
                    xq = xg[:, q * 256:(q + 1) * 256]
                    mhs, xms = [], []
                    for r4 in range(4):
                        lane = lane0 + g * SSM_HPG + q * 4 + r4
                        seg = col[:, lane:lane + 1] + rowm[lane:lane + 1, :]
                        dec = jnp.exp2(jnp.where(mask, seg, -jnp.inf))
                        mhs.append((cb * dec).astype(BF16))
                        xms.append(jnp.where(lane4 == r4, xq, jnp.zeros_like(xq)))
                    halves.append(jnp.dot(jnp.concatenate(mhs, axis=1), jnp.concatenate(xms, axis=0),
                                          preferred_element_type=F32))
                y = jnp.concatenate(halves, axis=1) + yscale_x[:, gsl] * y_off
                if mode == "partial":
                    out_ref[rows, gsl] = y.astype(BF16)
                else:
                    y = y + yo_ref[rows, gsl].astype(F32) + dsk_ref[:, gsl] * xg.astype(F32)
                    y = y * _silu(z_ref[rows, gsl].astype(F32))
                    ms = jnp.mean(y * y, axis=-1, keepdims=True)
                    out_ref[rows, gsl] = (y * lax.rsqrt(ms + EPS) * nw_ref[:, gsl]).astype(BF16)
            xw = (xg.astype(F32) * w_x[:, gsl]).astype(BF16)
            upd = lax.dot_general(bg, xw, (((0,), (0,)), ((), ())), preferred_element_type=F32)
            s_ref[g] = sdec_x[:, gsl] * s_old + upd

    for c in ((1, 0) if reverse else (0, 1)):
        chunk(c)

    if mode == "state":
        @pl.when(j == nblk - 1)
        def _():
            out_ref[0] = s_ref[...]


def _ssd_call(xbc, dt, dtT, prm, init, *, batch, nblk, reverse, mode, extra=()):
    m = xbc.shape[0]

    def blk(b, j):
        return b * nblk + (nblk - 1 - j if reverse else j)

    state_spec = pl.BlockSpec((1, SSM_GROUPS, SSM_STATE, GROUP_W), lambda b, j: (b, 0, 0, 0))
    const2 = lambda b, j: (0, 0)
    in_specs = [pl.BlockSpec((BLK, D_XBC), lambda b, j: (blk(b, j), 0)),
                pl.BlockSpec((BLK, LANES), lambda b, j: (blk(b, j), 0)),
                pl.BlockSpec((LANES, BLK), lambda b, j: (0, blk(b, j))),
                pl.BlockSpec((1, LANES), const2), pl.BlockSpec((LANES, 1), const2),
                pl.BlockSpec((LANES, D_SSM), const2),
                state_spec]
    args = [xbc, dt, dtT, prm["alog_row"], prm["alog_col"],
            prm["expand_rev" if reverse else "expand_fwd"], init]
    if mode == "state":
        out_specs = state_spec
        out_shape = jax.ShapeDtypeStruct((batch, SSM_GROUPS, SSM_STATE, GROUP_W), F32)
    else:
        out_specs = pl.BlockSpec((BLK, D_SSM), lambda b, j: (blk(b, j), 0))
        out_shape = jax.ShapeDtypeStruct((m, D_SSM), BF16)
    if mode == "final":
        z, y_other = extra
        in_specs += [pl.BlockSpec((BLK, D_SSM), lambda b, j: (blk(b, j), 0)),
                     pl.BlockSpec((BLK, D_SSM), lambda b, j: (blk(b, j), 0)),
                     pl.BlockSpec((1, D_SSM), const2), pl.BlockSpec((1, D_SSM), const2)]
        args += [z, y_other, prm["dskip"], prm["norm_w"]]
    kern = functools.partial(_ssd_kernel, reverse=reverse, mode=mode, nblk=nblk)
    return pl.pallas_call(
        kern,
        grid=(batch, nblk),
        in_specs=in_specs,
        out_specs=out_specs,
        out_shape=out_shape,
        scratch_shapes=[pltpu.VMEM((SSM_GROUPS, SSM_STATE, GROUP_W), F32)],
        compiler_params=_cparams(("arbitrary", "arbitrary")),
        name=f"ssd_{mode}_{'rev' if reverse else 'fwd'}",
    )(*args)


def _outproj_kernel(ys_ref, yc_ref, scw_ref, w_ref, x_ref, g1_ref, nw_ref, sh_ref, sc_ref, wr_ref,
                    fh_ref, aff_ref, *, tm, sub):
    for k in range(tm // sub):
        rows = slice(k * sub, (k + 1) * sub)
        yc = yc_ref[rows, :].astype(F32)
        ms = jnp.mean(yc * yc, axis=-1, keepdims=True)
        ycn = (yc * lax.rsqrt(ms + EPS) * scw_ref[...]).astype(BF16)
        mix = jnp.dot(ys_ref[rows, :], w_ref[0:D_SSM, :], preferred_element_type=F32)
        mix = mix + jnp.dot(ycn, w_ref[D_SSM:, :], preferred_element_type=F32)
        h = x_ref[rows, :] + g1_ref[0] * mix
        fh_ref[rows, D_MODEL:] = h
        f = _norm_modulate(h, nw_ref[...], sh_ref[0], sc_ref[0])
        fh_ref[rows, :D_MODEL] = f
        logits = _dot_nt(wr_ref[...], f.astype(BF16))
        mx = jnp.max(logits, axis=0, keepdims=True)
        ex = jnp.exp(logits - mx)
        aff_ref[0, :, rows] = ex / jnp.sum(ex, axis=0, keepdims=True)


def _outproj_call(y_ssm, y_sc, sc_norm_w, w_out, x2d, g1, norm_w, shift, scale, w_rT, *, batch, tm):
    m, d = x2d.shape
    n_tok = m // batch
    per = n_tok // tm
    vec = lambda i: (i // per, 0, 0)
    const2 = lambda i: (0, 0)
    sub = min(tm, 256)
    return pl.pallas_call(
        functools.partial(_outproj_kernel, tm=tm, sub=sub),
        grid=(m // tm,),
        in_specs=[pl.BlockSpec((tm, D_SSM), lambda i: (i, 0)),
                  pl.BlockSpec((tm, D_SC), lambda i: (i, 0)),
                  pl.BlockSpec((1, D_SC), const2),
                  pl.BlockSpec((D_SSM + D_SC, d), const2, pipeline_mode=pl.Buffered(1)),
                  pl.BlockSpec((tm, d), lambda i: (i, 0)),
                  pl.BlockSpec((1, 1, d), vec),
                  pl.BlockSpec((1, d), const2),
                  pl.BlockSpec((1, 1, d), vec),
                  pl.BlockSpec((1, 1, d), vec),
                  pl.BlockSpec((N_EXPERTS, d), const2)],
        out_specs=[pl.BlockSpec((tm, 2 * d), lambda i: (i, 0)),
                   pl.BlockSpec((1, N_EXPERTS, tm), lambda i: (i // per, 0, i % per))],
        out_shape=[jax.ShapeDtypeStruct((m, 2 * d), F32),
                   jax.ShapeDtypeStruct((batch, N_EXPERTS, n_tok), F32)],
        compiler_params=_cparams(("arbitrary",)),
        name="outproj",
    )(y_ssm, y_sc, sc_norm_w, w_out, x2d, g1, norm_w, shift, scale, w_rT)


def _prefix_lanes(m01):
    rows, n = m01.shape
    ki = lax.broadcasted_iota(I32, (LANES, LANES), 0)
    ci = lax.broadcasted_iota(I32, (LANES, LANES), 1)
    triu = jnp.where(ki <= ci, 1.0, 0.0).astype(BF16)
    off = jnp.zeros((rows, 1), F32)
    parts = []
    for k in range(n // LANES):
        p = jnp.dot(m01[:, k * LANES:(k + 1) * LANES].astype(BF16), triu,
                    preferred_element_type=F32)
        parts.append(p + off)
        off = off + p[:, LANES - 1:LANES]
    return jnp.concatenate(parts, axis=1)


def _route_kernel(aff_ref, idx_ref, gcol_ref, *, cap):
    a = aff_ref[0]
    n_e, n = a.shape
    thr = jnp.zeros((n_e, 1), I32)
    for bit in range(30, -1, -1):
        cand = thr | (1 << bit)
        cand_f = lax.bitcast_convert_type(cand, F32)
        cnt = jnp.sum(jnp.where(a >= cand_f, 1.0, 0.0), axis=1, keepdims=True)
        thr = jnp.where(cnt >= cap, cand, thr)
    thr_f = lax.bitcast_convert_type(thr, F32)
    gt = a > thr_f
    eq = a == thr_f
    need = cap - jnp.sum(jnp.where(gt, 1.0, 0.0), axis=1, keepdims=True)
    eqf = jnp.where(eq, 1.0, 0.0)
    ties_before = _prefix_lanes(eqf) - eqf
    sel = jnp.logical_or(gt, jnp.logical_and(eq, ties_before < need))
    self_ = jnp.where(sel, 1.0, 0.0)
    slot = jnp.where(sel, _prefix_lanes(self_) - 1.0, -1.0).astype(I32)

    tok = lax.broadcasted_iota(I32, (1, n), 1)
    t_hi = (tok // 64).astype(F32)
    t_lo = (tok % 64).astype(F32)
    a1 = a.astype(BF16).astype(F32)
    ra = a - a1
    a2 = ra.astype(BF16).astype(F32)
    a3 = ra - a2
    srow = lax.broadcasted_iota(I32, (cap, n), 0)
    r8 = lax.broadcasted_iota(I32, (8, n), 0)
    zpad = jnp.zeros((LANES - 8, n), F32)
    for e in range(n_e):
        onehot = jnp.where(slot[e:e + 1, :] == srow, 1.0, 0.0).astype(BF16)
        v8 = jnp.where(r8 == 0, t_hi, jnp.where(r8 == 1, t_lo, jnp.where(
            r8 == 2, a1[e:e + 1], jnp.where(r8 == 3, a2[e:e + 1], jnp.where(
                r8 == 4, a3[e:e + 1], 0.0)))))
        vals = jnp.concatenate([v8, zpad], axis=0).astype(BF16)
        o_col = lax.dot_general(onehot, vals, (((1,), (1,)), ((), ())),
                                preferred_element_type=F32)
        o_row = o_col.T
        idx_ref[0, e:e + 1, :] = (o_row[0:1] * 64.0 + o_row[1:2]).astype(I32)
        gcol_ref[0, e] = o_col


def _route_call(aff, *, cap):
    batch, n_e, n = aff.shape
    return pl.pallas_call(
        functools.partial(_route_kernel, cap=cap),
        grid=(batch,),
        in_specs=[pl.BlockSpec((1, n_e, n), lambda b: (b, 0, 0))],
        out_specs=[pl.BlockSpec((1, n_e, cap), lambda b: (b, 0, 0)),
                   pl.BlockSpec((1, n_e, cap, LANES), lambda b: (b, 0, 0, 0))],
        out_shape=[jax.ShapeDtypeStruct((batch, n_e, cap), I32),
                   jax.ShapeDtypeStruct((batch, n_e, cap, LANES), F32)],
        compiler_params=_cparams(("arbitrary",)),
        name="route",
    )(aff)


MOE_PARTS = 4
MOE_NCHUNK = 512
MOE_WROWS_A = 512
MOE_WROWS_D = 128
MOE_WPIECES = 4
MOE_WPRIO = 1


def _moe_kernel(idx_ref, gcol_ref, g2_ref, wg_hbm, wu_hbm, wd_hbm, fh_in, fh_out,
                buf0, buf1, wg_s, wu_s, wd_s, stg_a, stg_d, sem, *, cap, n_tok, n_pairs):
    e = pl.program_id(0)
    i = pl.program_id(1)
    n_e = pl.num_programs(0)
    first = jnp.logical_and(e == 0, i == 0)
    last = jnp.logical_and(e == n_e - 1, i == n_pairs - 1)
    i_next = jnp.where(i == n_pairs - 1, 0, i + 1)
    e_next = jnp.where(i == n_pairs - 1, e + 1, e)
    bufs = (buf0, buf1)
    part = cap // MOE_PARTS
    hcols = pl.ds(D_MODEL, D_MODEL)
    cur = e % 2

    def row_of(ex, b, jrow):
        return b * n_tok + idx_ref[(b * N_EXPERTS + ex) * cap + jrow]

    def gather_start(ex, b, s, jrow):
        pltpu.make_async_copy(fh_in.at[pl.ds(row_of(ex, b, jrow), 1)], bufs[s].at[pl.ds(jrow, 1)],
                              sem.at[s]).start()

    def scatter_start(ex, b, s, jrow):
        pltpu.make_async_copy(bufs[s].at[pl.ds(jrow, 1), hcols],
                              fh_out.at[pl.ds(row_of(ex, b, jrow), 1), hcols],
                              sem.at[2 + s]).start()

    def wait_gather(s):
        pltpu.make_async_copy(fh_in.at[pl.ds(0, cap)], bufs[s], sem.at[s]).wait()

    def wait_scatter(s):
        pltpu.make_async_copy(bufs[s].at[:, hcols], fh_out.at[pl.ds(0, cap), hcols],
                              sem.at[2 + s]).wait()

    g_units = (D_MODEL // MOE_WROWS_A) // n_pairs
    d_units = (EXPERT_FF // MOE_WROWS_D) // n_pairs
    units = [("g", j) for j in range(g_units)] + [("u", j) for j in range(g_units)] \
        + [("d", j) for j in range(d_units)]
    n_parts = 2 * MOE_PARTS
    upp = len(units) // n_parts
    pieces = MOE_WPIECES // upp

    def unit_refs(kind, j, tgt, step, slot):
        if kind == "d":
            row0 = pl.multiple_of((step * d_units + j) * MOE_WROWS_D, MOE_WROWS_D)
            return (wd_hbm.at[tgt, pl.ds(row0, MOE_WROWS_D), :], stg_d.at[slot], wd_s, row0,
                    MOE_WROWS_D)
        src = wg_hbm if kind == "g" else wu_hbm
        row0 = pl.multiple_of((step * g_units + j) * MOE_WROWS_A, MOE_WROWS_A)
        return (src.at[tgt, pl.ds(row0, MOE_WROWS_A), :], stg_a.at[slot],
                wg_s if kind == "g" else wu_s, row0, MOE_WROWS_A)

    def unit_copy(t, tgt, step):
        kind, j = units[t]
        src, dst, _, _, _ = unit_refs(kind, j, tgt, step, t % 2)
        return pltpu.make_async_copy(src, dst, sem.at[4 + t % 2])

    def unit_cast(t, step, wslot, piece, n_pieces):
        kind, j = units[t]
        _, stg, dst, row0, rows = unit_refs(kind, j, 0, step, t % 2)
        pr = rows // n_pieces
        dst[wslot, pl.ds(row0 + piece * pr, pr), :] = stg[piece * pr:(piece + 1) * pr, :].astype(BF16)

    tgt_now = jnp.minimum(e + 1, n_e - 1)
    tgt_next = jnp.minimum(e_next + 1, n_e - 1)

    def unit_start_ahead(t):
        t2 = t + 2
        if t2 < len(units):
            unit_copy(t2, tgt_now, i).start(priority=MOE_WPRIO)
        else:
            unit_copy(t2 - len(units), tgt_next, i_next).start(priority=MOE_WPRIO)

    def weight_jobs(p):
        jobs = []
        for c in range(upp):
            t = p * upp + c
            for piece in range(pieces):
                def job(t=t, piece=piece):
                    if piece == 0:
                        unit_copy(t, tgt_now, i).wait()
                    unit_cast(t, i, 1 - cur, piece, pieces)
                jobs.append(job)
            jobs.append(functools.partial(unit_start_ahead, t))
        return jobs

    def compute_part(s, b_local, q, between, wjobs):
        rows = slice(q * part, (q + 1) * part)
        xq = bufs[s][rows, 0:D_MODEL].astype(BF16)
        gc = gcol_ref[b_local, 0, rows, :]
        gate = gc[:, 2:3] + gc[:, 3:4] + gc[:, 4:5]
        g2 = g2_ref[2 * i + b_local]
        nc = MOE_NCHUNK
        hid = []
        slot_no = 0

        def tick():
            nonlocal slot_no
            between[slot_no]()
            if slot_no < len(wjobs):
                wjobs[slot_no]()
            slot_no += 1

        for c in range(EXPERT_FF // nc):
            cols = slice(c * nc, (c + 1) * nc)
            tick()
            hg = jnp.dot(xq, wg_s[cur, :, cols], preferred_element_type=F32)
            tick()
            hu = jnp.dot(xq, wu_s[cur, :, cols], preferred_element_type=F32)
            hid.append((_silu(hg) * hu).astype(BF16))
        hid = jnp.concatenate(hid, axis=1)
        for c in range(D_MODEL // nc):
            cols = slice(c * nc, (c + 1) * nc)
            hc = slice(D_MODEL + c * nc, D_MODEL + (c + 1) * nc)
            tick()
            ye = jnp.dot(hid, wd_s[cur, :, cols], preferred_element_type=F32)
            bufs[s][rows, hc] = bufs[s][rows, hc] + g2[:, cols] * (ye * gate)

    n_between = 2 * (EXPERT_FF // MOE_NCHUNK) + D_MODEL // MOE_NCHUNK

    def spread(jobs):
        n = len(jobs)
        cuts = [n * k // n_between for k in range(n_between + 1)]

        def group(k):
            def run():
                for job in jobs[cuts[k]:cuts[k + 1]]:
                    job()
            return run
        return [group(k) for k in range(n_between)]

    b_a, b_b = 2 * i, 2 * i + 1
    b_n = 2 * i_next

    @pl.when(first)
    def _():
        @pl.loop(0, cap)
        def _(jrow):
            gather_start(e, b_a, 0, jrow)
        for step in range(n_pairs):
            for t in range(len(units)):
                kind, j = units[t]
                src, _, _, _, _ = unit_refs(kind, j, 0, step, 0)
                stg0 = stg_d.at[0] if kind == "d" else stg_a.at[0]
                cp = pltpu.make_async_copy(src, stg0, sem.at[4])
                cp.start()
                cp.wait()
                _, _, dst, row0, rows = unit_refs(kind, j, 0, step, 0)
                dst[0, pl.ds(row0, rows), :] = (stg_d[0] if kind == "d" else stg_a[0]).astype(BF16)
        unit_copy(0, tgt_now, i).start(priority=MOE_WPRIO)
        unit_copy(1, tgt_now, i).start(priority=MOE_WPRIO)

    wait_gather(0)
    fill = {1: range(0, cap // 2), 2: range(cap // 2, cap)}
    for q in range(MOE_PARTS):
        jobs = []
        if q > 0:
            jobs += [functools.partial(scatter_start, e, b_a, 0, jrow)
                     for jrow in range((q - 1) * part, q * part)]
        jobs += [functools.partial(gather_start, e, b_b, 1, jrow) for jrow in fill.get(q, ())]
        if q == 1:
            @pl.when(jnp.logical_not(first))
            def _():
                wait_scatter(1)
        compute_part(0, 0, q, spread(jobs), weight_jobs(q))

    wait_gather(1)
    for q in range(MOE_PARTS):
        if q == 0:
            jobs = [functools.partial(scatter_start, e, b_a, 0, jrow)
                    for jrow in range((MOE_PARTS - 1) * part, cap)]
        else:
            jobs = [functools.partial(scatter_start, e, b_b, 1, jrow)
                    for jrow in range((q - 1) * part, q * part)]
        jobs += [functools.partial(gather_start, e_next, b_n, 0, jrow) for jrow in fill.get(q, ())]
        if q == 1:
            wait_scatter(0)
        compute_part(1, 1, q, spread(jobs), weight_jobs(MOE_PARTS + q))

    for jrow in range((MOE_PARTS - 1) * part, cap):
        scatter_start(e, b_b, 1, jrow)

    @pl.when(last)
    def _():
        wait_scatter(1)
        wait_gather(0)
        unit_copy(0, tgt_next, i_next).wait()
        unit_copy(1, tgt_next, i_next).wait()


def _moe_call(idx_flat, gcol, g2, wg, wu, wd, fh, *, batch, cap):
    m, d = fh.shape[0], fh.shape[1] // 2
    n_tok = m // batch
    assert batch % 2 == 0 and cap % MOE_PARTS == 0
    n_pairs = batch // 2
    assert n_pairs in (1, 2)
    kern = functools.partial(_moe_kernel, cap=cap, n_tok=n_tok, n_pairs=n_pairs)
    grid_spec = pltpu.PrefetchScalarGridSpec(
        num_scalar_prefetch=1,
        grid=(N_EXPERTS, n_pairs),
        in_specs=[pl.BlockSpec((2, 1, cap, LANES), lambda e, i, idx: (i, e, 0, 0)),
                  pl.BlockSpec((batch, 1, d), lambda e, i, idx: (0, 0, 0)),
                  pl.BlockSpec(memory_space=pl.ANY), pl.BlockSpec(memory_space=pl.ANY),
                  pl.BlockSpec(memory_space=pl.ANY), pl.BlockSpec(memory_space=pl.ANY)],
        out_specs=pl.BlockSpec(memory_space=pl.ANY),
        scratch_shapes=[pltpu.VMEM((cap, 2 * d), F32), pltpu.VMEM((cap, 2 * d), F32),
                        pltpu.VMEM((2, d, EXPERT_FF), BF16), pltpu.VMEM((2, d, EXPERT_FF), BF16),
                        pltpu.VMEM((2, EXPERT_FF, d), BF16),
                        pltpu.VMEM((2, MOE_WROWS_A, EXPERT_FF), F32),
                        pltpu.VMEM((2, MOE_WROWS_D, d), F32),
                        pltpu.SemaphoreType.DMA((6,))],
    )
    return pl.pallas_call(
        kern,
        grid_spec=grid_spec,
        out_shape=jax.ShapeDtypeStruct((m, 2 * d), F32),
        input_output_aliases={6: 0},
        compiler_params=_cparams(("arbitrary", "arbitrary")),
        name="moe",
    )(idx_flat, gcol, g2, wg, wu, wd, fh)


def _final_kernel(h_ref, w_ref, o_ref):
    h = h_ref[...]
    ms = jnp.mean(h * h, axis=-1, keepdims=True)
    o_ref[...] = h * lax.rsqrt(ms + EPS) * w_ref[...]


def _final_call(fh, w, *, tm):
    m, d = fh.shape[0], fh.shape[1] // 2
    return pl.pallas_call(
        _final_kernel,
        grid=(m // tm,),
        in_specs=[pl.BlockSpec((tm, d), lambda i: (i, 1)), pl.BlockSpec((1, d), lambda i: (0, 0))],
        out_specs=pl.BlockSpec((tm, d), lambda i: (i, 0)),
        out_shape=jax.ShapeDtypeStruct((m, d), F32),
        compiler_params=_cparams(("arbitrary",)),
        name="final_norm",
    )(fh, w)


def _pick_tile(n, pref):
    t = pref
    while n % t:
        t //= 2
    return t


def _ssd_params(a_log, dt_bias, d_skip, norm_w):
    padw = LANES - N_DT
    alog = jnp.pad(a_log.astype(F32), (0, padw))
    bias = jnp.pad(dt_bias.astype(F32), (0, padw))
    lane = jnp.arange(LANES)[:, None]
    head = jnp.arange(D_SSM)[None, :] // SSM_HEAD_DIM
    return {
        "alog_row": alog[None, :], "alog_col": alog[:, None],
        "bias_row": bias[None, :], "bias_col": bias[:, None],
        "expand_fwd": (lane == head).astype(BF16),
        "expand_rev": (lane == head + SSM_HEADS).astype(BF16),
        "dskip": jnp.repeat(d_skip.astype(F32), SSM_HEAD_DIM)[None, :],
        "norm_w": norm_w.astype(F32)[None, :],
    }


def kernel(x, c, ctx, c_ctx, w_mod, b_mod, norm_mix_w, w_in, ssm_conv_w, ssm_conv_b, ssm_a_log,
           ssm_dt_bias, ssm_d, ssm_norm_w, sc_conv_w, sc_norm_w, w_out, norm_ffn_w, w_router,
           w_gate, w_up, w_down, final_norm_w):
    batch, n_lat, d = x.shape
    n_ctx = ctx.shape[1]
    depth = w_mod.shape[0]
    assert depth == 1 and d == D_MODEL and n_ctx == BLK and n_lat % BLK == 0
    layer = 0
    cap = EC_CAPACITY_FACTOR * n_lat // N_EXPERTS

    pad_rows = -(batch + 1) % 8
    cvec = jnp.concatenate([c, c_ctx[None, :], jnp.zeros((pad_rows, d), F32)], axis=0)
    mod = _mod_call(cvec, w_mod[layer], b_mod[layer][None, :])
    sh1, sc1, g1, sh2, sc2, g2 = [mod[:batch, k * d:(k + 1) * d][:, None, :] for k in range(N_MOD)]
    sh1_c = mod[batch:batch + 1, 0:d][:, None, :]
    sc1_c = mod[batch:batch + 1, d:2 * d][:, None, :]

    w_inT = w_in[layer].T.astype(BF16)
    tc = 512
    w_out_b = w_out[layer].astype(BF16)
    w_rT = w_router[layer].T.astype(BF16)
    wg, wu, wd = w_gate[layer], w_up[layer], w_down[layer]
    norm_mix = norm_mix_w[layer][None, :]
    conv_w, conv_b = ssm_conv_w[layer], ssm_conv_b[layer][None, :]
    prm = _ssd_params(ssm_a_log[layer], ssm_dt_bias[layer], ssm_d[layer], ssm_norm_w[layer])

    ctx2d = ctx.reshape(batch * n_ctx, d)
    xbc_c, dt_c, dtT_c = _inproj_zx_call(
        ctx2d, sh1_c, sc1_c, norm_mix, w_inT, conv_w, conv_b, prm["bias_row"], prm["bias_col"],
        with_z=False, period=n_ctx, tm=_pick_tile(batch * n_ctx, 1024), rows_per_mod=batch * n_ctx)
    zero_state = jnp.zeros((batch, SSM_GROUPS, SSM_STATE, GROUP_W), F32)
    st_f = _ssd_call(xbc_c, dt_c, dtT_c, prm, zero_state, batch=batch, nblk=1, reverse=False,
                     mode="state")
    st_r = _ssd_call(xbc_c, dt_c, dtT_c, prm, zero_state, batch=batch, nblk=1, reverse=True,
                     mode="state")

    x2d = x.reshape(batch * n_lat, d)
    tm = _pick_tile(n_lat, 1024)
    z, xbc, dt, dtT, a_lat = _inproj_zx_call(
        x2d, sh1, sc1, norm_mix, w_inT, conv_w, conv_b, prm["bias_row"], prm["bias_col"],
        with_z=True, period=GRID_W, tm=tm, rows_per_mod=n_lat)
    y_sc = _inproj_sc_call(a_lat, w_inT, sc_conv_w[layer], tm=tm, tc=tc)
    nblk = n_lat // BLK
    y_rev = _ssd_call(xbc, dt, dtT, prm, st_r, batch=batch, nblk=nblk, reverse=True, mode="partial")
    y_ssm = _ssd_call(xbc, dt, dtT, prm, st_f, batch=batch, nblk=nblk, reverse=False, mode="final",
                      extra=(z, y_rev))

    fh, aff = _outproj_call(
        y_ssm, y_sc, sc_norm_w[layer][None, :], w_out_b, x2d, g1, norm_ffn_w[layer][None, :],
        sh2, sc2, w_rT, batch=batch, tm=_pick_tile(n_lat, 512))
    idx, gcol = _route_call(aff, cap=cap)
    fh = _moe_call(idx.reshape(-1), gcol, g2, wg, wu, wd, fh, batch=batch, cap=cap)
    out = _final_call(fh, final_norm_w[None, :], tm=_pick_tile(batch * n_lat, 512))
    return out.reshape(batch, n_lat, d)
```

```python
import functools

import jax
import jax.numpy as jnp
from jax import lax
from jax.experimental import pallas as pl
from jax.experimental.pallas import tpu as pltpu

F32 = jnp.float32
BF16 = jnp.bfloat16
I32 = jnp.int32

D_MODEL = 2048
GRID_W = 64
D_SSM = 2048
SSM_HEAD_DIM = 64
SSM_HEADS = D_SSM // SSM_HEAD_DIM
SSM_GROUPS = 4
SSM_HPG = SSM_HEADS // SSM_GROUPS
SSM_STATE = 128
SSD_CHUNK = 128
GN = SSM_GROUPS * SSM_STATE
D_XBC = D_SSM + 2 * GN
D_ZX = D_SSM + D_XBC
N_DT = 2 * SSM_HEADS
D_SSM_IN = D_ZX + N_DT
D_SC = 2048
N_EXPERTS = 16
EC_CAPACITY_FACTOR = 2
EXPERT_FF = 1024
N_MOD = 6
EPS = 1e-6
LOG2E = 1.4426950408889634

LANES = 128
BLK = 2 * SSD_CHUNK
GROUP_W = SSM_HPG * SSM_HEAD_DIM
VMEM_LIMIT = 56 * 1024 * 1024


def _cparams(sem):
    return pltpu.CompilerParams(dimension_semantics=sem, vmem_limit_bytes=VMEM_LIMIT)


def _silu(v):
    return v * jax.nn.sigmoid(v)


def _softplus(v):
    return jnp.maximum(v, 0.0) + jnp.log1p(jnp.exp(-jnp.abs(v)))


def _mod_kernel(c_ref, w_ref, b_ref, o_ref):
    s = _silu(c_ref[...]).astype(BF16)
    o_ref[...] = jnp.dot(s, w_ref[...].astype(BF16), preferred_element_type=F32) + b_ref[...]


def _mod_call(cvec, w_mod, b_mod):
    rows, d = cvec.shape
    n = w_mod.shape[1]
    tn = 1024
    return pl.pallas_call(
        _mod_kernel,
        grid=(n // tn,),
        in_specs=[pl.BlockSpec((rows, d), lambda j: (0, 0)),
                  pl.BlockSpec((d, tn), lambda j: (0, j)),
                  pl.BlockSpec((1, tn), lambda j: (0, j))],
        out_specs=pl.BlockSpec((rows, tn), lambda j: (0, j)),
        out_shape=jax.ShapeDtypeStruct((rows, n), F32),
        compiler_params=_cparams(("arbitrary",)),
        name="mod",
    )(cvec, w_mod, b_mod)


def _dot_nt(a, bt):
    return lax.dot_general(a, bt, (((1,), (1,)), ((), ())), preferred_element_type=F32)


def _norm_modulate(h, nw, shift, scale):
    ms = jnp.mean(h * h, axis=-1, keepdims=True)
    hn = h * lax.rsqrt(ms + EPS) * nw
    return hn * (1.0 + scale) + shift


def _conv3(v, cw, period):
    rows = v.shape[0]
    r = lax.broadcasted_iota(I32, (rows, 1), 0) % period
    prev = jnp.where(r == 0, 0.0, pltpu.roll(v, 1, 0))
    nxt = jnp.where(r == period - 1, 0.0, pltpu.roll(v, rows - 1, 0))
    return prev * cw[0:1] + v * cw[1:2] + nxt * cw[2:3]


def _inproj_zx_kernel(h_ref, sh_ref, sc_ref, nw_ref, w_ref, wdt_ref, cw_ref, cb_ref, br_ref, bc_ref,
                      *rest, with_z, period, tm, n_z):
    if with_z:
        z_ref, xbc_ref, dt_ref, dtT_ref, ao_ref, a_ref = rest
    else:
        xbc_ref, dt_ref, dtT_ref, a_ref = rest
    n = pl.program_id(1)

    @pl.when(n == 0)
    def _():
        a = _norm_modulate(h_ref[...], nw_ref[...], sh_ref[0], sc_ref[0]).astype(BF16)
        a_ref[...] = a
        if with_z:
            ao_ref[...] = a
        wdt = wdt_ref[...]
        dt_ref[...] = _softplus(_dot_nt(a, wdt) + br_ref[...])
        dtT_ref[...] = _softplus(_dot_nt(wdt, a) + bc_ref[...])

    wb = w_ref[...]

    if with_z:
        @pl.when(n < n_z)
        def _():
            for k in range(tm // BLK):
                rows = slice(k * BLK, (k + 1) * BLK)
                z_ref[rows, :] = _dot_nt(a_ref[rows, :], wb).astype(BF16)

    @pl.when(n >= n_z)
    def _():
        for k in range(tm // BLK):
            rows = slice(k * BLK, (k + 1) * BLK)
            acc = _dot_nt(a_ref[rows, :], wb)
            y = _conv3(acc, cw_ref[...], period) + cb_ref[...]
            xbc_ref[rows, :] = _silu(y).astype(BF16)


def _inproj_zx_call(h2d, shift, scale, norm_w, w_inT, conv_w, conv_b, bias_row, bias_col, *,
                    with_z, period, tm, rows_per_mod):
    m, d = h2d.shape
    tn = 1024
    n_z = D_SSM // tn if with_z else 0
    n_x = D_XBC // tn
    n_off = 0 if with_z else D_SSM // tn
    per = rows_per_mod // tm
    kern = functools.partial(_inproj_zx_kernel, with_z=with_z, period=period, tm=tm, n_z=n_z)
    out_shape = [jax.ShapeDtypeStruct((m, D_XBC), BF16),
                 jax.ShapeDtypeStruct((m, LANES), F32),
                 jax.ShapeDtypeStruct((LANES, m), F32)]
    out_specs = [pl.BlockSpec((tm, tn), lambda i, n: (i, jnp.maximum(n - n_z, 0))),
                 pl.BlockSpec((tm, LANES), lambda i, n: (i, 0)),
                 pl.BlockSpec((LANES, tm), lambda i, n: (0, i))]
    if with_z:
        out_shape = ([jax.ShapeDtypeStruct((m, D_SSM), BF16)] + out_shape
                     + [jax.ShapeDtypeStruct((m, d), BF16)])
        out_specs = ([pl.BlockSpec((tm, tn), lambda i, n: (i, jnp.minimum(n, n_z - 1)))] + out_specs
                     + [pl.BlockSpec((tm, d), lambda i, n: (i, 0))])
    return pl.pallas_call(
        kern,
        grid=(m // tm, n_z + n_x),
        in_specs=[pl.BlockSpec((tm, d), lambda i, n: (i, 0)),
                  pl.BlockSpec((1, 1, d), lambda i, n: (i // per, 0, 0)),
                  pl.BlockSpec((1, 1, d), lambda i, n: (i // per, 0, 0)),
                  pl.BlockSpec((1, d), lambda i, n: (0, 0)),
                  pl.BlockSpec((tn, d), lambda i, n: (n + n_off, 0)),
                  pl.BlockSpec((LANES, d), lambda i, n: (D_ZX // LANES, 0)),
                  pl.BlockSpec((3, tn), lambda i, n: (0, jnp.maximum(n - n_z, 0))),
                  pl.BlockSpec((1, tn), lambda i, n: (0, jnp.maximum(n - n_z, 0))),
                  pl.BlockSpec((1, LANES), lambda i, n: (0, 0)),
                  pl.BlockSpec((LANES, 1), lambda i, n: (0, 0))],
        out_specs=out_specs,
        out_shape=out_shape,
        scratch_shapes=[pltpu.VMEM((tm, d), BF16)],
        compiler_params=_cparams(("arbitrary", "arbitrary")),
        name="inproj_zx" if with_z else "inproj_ctx",
    )(h2d, shift, scale, norm_w, w_inT, w_inT, conv_w, conv_b, bias_row, bias_col)


def _inproj_sc_kernel(a_ref, wb_ref, wc_ref, wv_ref, cw_ref, o_ref, *, tm):
    wb, wc, wv = wb_ref[...], wc_ref[...], wv_ref[...]
    for k in range(tm // BLK):
        rows = slice(k * BLK, (k + 1) * BLK)
        a = a_ref[rows, :]
        cg = _dot_nt(a, wc)
        hv = _dot_nt(a, wv)
        v = _conv3(cg * hv, cw_ref[...], GRID_W)
        bg = _dot_nt(a, wb)
        o_ref[rows, :] = (bg * v).astype(BF16)


def _inproj_sc_call(a2d, w_inT, conv_w, *, tm, tc):
    m, d = a2d.shape
    kern = functools.partial(_inproj_sc_kernel, tm=tm)

    def wspec(k):
        return pl.BlockSpec((pl.Element(tc), pl.Element(d)),
                            lambda i, n: (pl.multiple_of(D_SSM_IN + k * D_SC + n * tc, 16), 0))

    return pl.pallas_call(
        kern,
        grid=(m // tm, D_SC // tc),
        in_specs=[pl.BlockSpec((tm, d), lambda i, n: (i, 0)),
                  wspec(0), wspec(1), wspec(2),
                  pl.BlockSpec((3, tc), lambda i, n: (0, n))],
        out_specs=pl.BlockSpec((tm, tc), lambda i, n: (i, n)),
        out_shape=jax.ShapeDtypeStruct((m, D_SC), BF16),
        compiler_params=_cparams(("arbitrary", "arbitrary")),
        name="inproj_sc",
    )(a2d, w_inT, w_inT, w_inT, conv_w)


def _ssd_kernel(*refs, reverse, mode, nblk):
    xbc_ref, dt_ref, dtT_ref, alr_ref, alc_ref, e_ref, init_ref = refs[:7]
    rest = refs[7:]
    if mode == "state":
        out_ref, s_ref = rest
    elif mode == "partial":
        out_ref, s_ref = rest
    else:
        z_ref, yo_ref, dsk_ref, nw_ref, out_ref, s_ref = rest
    j = pl.program_id(1)

    @pl.when(j == 0)
    def _():
        s_ref[...] = init_ref[0]

    ii = lax.broadcasted_iota(I32, (SSD_CHUNK, SSD_CHUNK), 0)
    jj = lax.broadcasted_iota(I32, (SSD_CHUNK, SSD_CHUNK), 1)
    tri = jnp.where(jj <= ii, 1.0, 0.0).astype(F32)
    triT = jnp.where(ii <= jj, 1.0, 0.0).astype(F32)
    mask = (jj >= ii) if reverse else (ii >= jj)
    lane4 = lax.broadcasted_iota(I32, (SSD_CHUNK, 4 * SSM_HEAD_DIM), 1) // SSM_HEAD_DIM
    aneg_r = -jnp.exp(alr_ref[...]) * LOG2E
    aneg_c = -jnp.exp(alc_ref[...]) * LOG2E
    lane0 = SSM_HEADS if reverse else 0
    hi = lax.Precision.HIGHEST

    def chunk(c):
        rows = slice(c * SSD_CHUNK, (c + 1) * SSD_CHUNK)
        dt = dt_ref[rows, :]
        dtT = dtT_ref[:, rows]
        la = dt * aneg_r
        laT = dtT * aneg_c
        cs = jnp.dot(tri, la, precision=hi, preferred_element_type=F32)
        csT = jnp.dot(laT, triT, precision=hi, preferred_element_type=F32)
        total = cs[SSD_CHUNK - 1:SSD_CHUNK, :]
        log2_dtT = jnp.log(dtT) * LOG2E
        if reverse:
            ecs = cs - la
            col, rowm = -ecs, csT - laT + log2_dtT
            yscale = jnp.exp2(total - ecs)
            w = dt * jnp.exp2(ecs)
        else:
            col, rowm = cs, log2_dtT - csT
            yscale = jnp.exp2(cs)
            w = dt * jnp.exp2(total - cs)
        t1 = total.astype(BF16).astype(F32)
        r1 = total - t1
        t2 = r1.astype(BF16).astype(F32)
        t3 = r1 - t2
        r16 = lax.broadcasted_iota(I32, (16, LANES), 0)
        tail = jnp.where(r16 == 0, t1, jnp.where(r16 == 1, t2, jnp.where(r16 == 2, t3, 0.0)))
        stack = jnp.concatenate([yscale, w, tail], axis=0).astype(BF16)
        ex = jnp.dot(stack, e_ref[...], preferred_element_type=F32)
        yscale_x = ex[0:SSD_CHUNK]
        w_x = ex[SSD_CHUNK:2 * SSD_CHUNK]
        tb = 2 * SSD_CHUNK
        sdec_x = jnp.exp2(ex[tb:tb + 1] + ex[tb + 1:tb + 2] + ex[tb + 2:tb + 3])

        for g in range(SSM_GROUPS):
            gsl = slice(g * GROUP_W, (g + 1) * GROUP_W)
            xg = xbc_ref[rows, gsl]
            bg = xbc_ref[rows, D_SSM + g * SSM_STATE:D_SSM + (g + 1) * SSM_STATE]
            s_old = s_ref[g]
            if mode != "state":
                cg = xbc_ref[rows, D_SSM + GN + g * SSM_STATE:D_SSM + GN + (g + 1) * SSM_STATE]
                cb = lax.dot_general(cg, bg, (((1,), (1,)), ((), ())), preferred_element_type=F32)
                y_off = jnp.dot(cg, s_old.astype(BF16), preferred_element_type=F32)
                halves = []
                for q in range(2):
                    xq = xg[:, q * 256:(q + 1) * 256]
                    mhs, xms = [], []
                    for r4 in range(4):
                        lane = lane0 + g * SSM_HPG + q * 4 + r4
                        seg = col[:, lane:lane + 1] + rowm[lane:lane + 1, :]
                        dec = jnp.exp2(jnp.where(mask, seg, -jnp.inf))
                        mhs.append((cb * dec).astype(BF16))
                        xms.append(jnp.where(lane4 == r4, xq, jnp.zeros_like(xq)))
                    halves.append(jnp.dot(jnp.concatenate(mhs, axis=1), jnp.concatenate(xms, axis=0),
                                          preferred_element_type=F32))
                y = jnp.concatenate(halves, axis=1) + yscale_x[:, gsl] * y_off
                if mode == "partial":
                    out_ref[rows, gsl] = y.astype(BF16)
                else:
                    y = y + yo_ref[rows, gsl].astype(F32) + dsk_ref[:, gsl] * xg.astype(F32)
                    y = y * _silu(z_ref[rows, gsl].astype(F32))
                    ms = jnp.mean(y * y, axis=-1, keepdims=True)
                    out_ref[rows, gsl] = (y * lax.rsqrt(ms + EPS) * nw_ref[:, gsl]).astype(BF16)
            xw = (xg.astype(F32) * w_x[:, gsl]).astype(BF16)
            upd = lax.dot_general(bg, xw, (((0,), (0,)), ((), ())), preferred_element_type=F32)
            s_ref[g] = sdec_x[:, gsl] * s_old + upd

    for c in ((1, 0) if reverse else (0, 1)):
        chunk(c)

    if mode == "state":
        @pl.when(j == nblk - 1)
        def _():
            out_ref[0] = s_ref[...]


def _ssd_call(xbc, dt, dtT, prm, init, *, batch, nblk, reverse, mode, extra=()):
    m = xbc.shape[0]

    def blk(b, j):
        return b * nblk + (nblk - 1 - j if reverse else j)

    state_spec = pl.BlockSpec((1, SSM_GROUPS, SSM_STATE, GROUP_W), lambda b, j: (b, 0, 0, 0))
    const2 = lambda b, j: (0, 0)
    in_specs = [pl.BlockSpec((BLK, D_XBC), lambda b, j: (blk(b, j), 0)),
                pl.BlockSpec((BLK, LANES), lambda b, j: (blk(b, j), 0)),
                pl.BlockSpec((LANES, BLK), lambda b, j: (0, blk(b, j))),
                pl.BlockSpec((1, LANES), const2), pl.BlockSpec((LANES, 1), const2),
                pl.BlockSpec((LANES, D_SSM), const2),
                state_spec]
    args = [xbc, dt, dtT, prm["alog_row"], prm["alog_col"],
            prm["expand_rev" if reverse else "expand_fwd"], init]
    if mode == "state":
        out_specs = state_spec
        out_shape = jax.ShapeDtypeStruct((batch, SSM_GROUPS, SSM_STATE, GROUP_W), F32)
    else:
        out_specs = pl.BlockSpec((BLK, D_SSM), lambda b, j: (blk(b, j), 0))
        out_shape = jax.ShapeDtypeStruct((m, D_SSM), BF16)
    if mode == "final":
        z, y_other = extra
        in_specs += [pl.BlockSpec((BLK, D_SSM), lambda b, j: (blk(b, j), 0)),
                     pl.BlockSpec((BLK, D_SSM), lambda b, j: (blk(b, j), 0)),
                     pl.BlockSpec((1, D_SSM), const2), pl.BlockSpec((1, D_SSM), const2)]
        args += [z, y_other, prm["dskip"], prm["norm_w"]]
    kern = functools.partial(_ssd_kernel, reverse=reverse, mode=mode, nblk=nblk)
    return pl.pallas_call(
        kern,
        grid=(batch, nblk),
        in_specs=in_specs,
        out_specs=out_specs,
        out_shape=out_shape,
        scratch_shapes=[pltpu.VMEM((SSM_GROUPS, SSM_STATE, GROUP_W), F32)],
        compiler_params=_cparams(("arbitrary", "arbitrary")),
        name=f"ssd_{mode}_{'rev' if reverse else 'fwd'}",
    )(*args)


OUT_NCHUNK = 512


def _outproj_kernel(ys_ref, yc_ref, scw_ref, w_ref, x_ref, g1_ref, nw_ref, sh_ref, sc_ref, wr_ref,
                    fh_ref, aff_ref, *, tm, sub):
    nc = OUT_NCHUNK
    chunks = [slice(c * nc, (c + 1) * nc) for c in range(D_MODEL // nc)]
    g1, nw, sh, sc = g1_ref[0], nw_ref[...], sh_ref[0], sc_ref[0]

    def rows_of(k):
        return slice(k * sub, (k + 1) * sub)

    def prologue(k):
        yc = yc_ref[rows_of(k), :].astype(F32)
        ms = jnp.mean(yc * yc, axis=-1, keepdims=True)
        return ys_ref[rows_of(k), :], (yc * lax.rsqrt(ms + EPS) * scw_ref[...]).astype(BF16)

    def pass1(k, ys, ycn, cols):
        mix = jnp.dot(ys, w_ref[0:D_SSM, cols], preferred_element_type=F32)
        mix = mix + jnp.dot(ycn, w_ref[D_SSM:, cols], preferred_element_type=F32)
        h = x_ref[rows_of(k), cols] + g1[:, cols] * mix
        fh_ref[rows_of(k), slice(D_MODEL + cols.start, D_MODEL + cols.stop)] = h
        return jnp.sum(h * h, axis=-1, keepdims=True)

    def pass2(k, rstd, cols):
        h = fh_ref[rows_of(k), slice(D_MODEL + cols.start, D_MODEL + cols.stop)]
        f = h * rstd * nw[:, cols] * (1.0 + sc[:, cols]) + sh[:, cols]
        fh_ref[rows_of(k), cols] = f
        return _dot_nt(wr_ref[:, cols], f.astype(BF16))

    def finish(k, logits):
        mx = jnp.max(logits, axis=0, keepdims=True)
        ex = jnp.exp(logits - mx)
        aff_ref[0, :, rows_of(k)] = ex / jnp.sum(ex, axis=0, keepdims=True)

    prev = None
    for k in range(tm // sub):
        ys, ycn = prologue(k)
        ssq, logits = None, None
        for cols in chunks:
            part = pass1(k, ys, ycn, cols)
            ssq = part if ssq is None else ssq + part
            if prev is not None:
                lg = pass2(prev[0], prev[1], cols)
                logits = lg if logits is None else logits + lg
        if prev is not None:
            finish(prev[0], logits)
        prev = (k, lax.rsqrt(ssq * (1.0 / D_MODEL) + EPS))
    logits = None
    for cols in chunks:
        lg = pass2(prev[0], prev[1], cols)
        logits = lg if logits is None else logits + lg
    finish(prev[0], logits)


def _outproj_call(y_ssm, y_sc, sc_norm_w, w_out, x2d, g1, norm_w, shift, scale, w_rT, *, batch, tm):
    m, d = x2d.shape
    n_tok = m // batch
    per = n_tok // tm
    vec = lambda i: (i // per, 0, 0)
    const2 = lambda i: (0, 0)
    sub = min(tm, 256)
    return pl.pallas_call(
        functools.partial(_outproj_kernel, tm=tm, sub=sub),
        grid=(m // tm,),
        in_specs=[pl.BlockSpec((tm, D_SSM), lambda i: (i, 0)),
                  pl.BlockSpec((tm, D_SC), lambda i: (i, 0)),
                  pl.BlockSpec((1, D_SC), const2),
                  pl.BlockSpec((D_SSM + D_SC, d), const2, pipeline_mode=pl.Buffered(1)),
                  pl.BlockSpec((tm, d), lambda i: (i, 0)),
                  pl.BlockSpec((1, 1, d), vec),
                  pl.BlockSpec((1, d), const2),
                  pl.BlockSpec((1, 1, d), vec),
                  pl.BlockSpec((1, 1, d), vec),
                  pl.BlockSpec((N_EXPERTS, d), const2)],
        out_specs=[pl.BlockSpec((tm, 2 * d), lambda i: (i, 0)),
                   pl.BlockSpec((1, N_EXPERTS, tm), lambda i: (i // per, 0, i % per))],
        out_shape=[jax.ShapeDtypeStruct((m, 2 * d), F32),
                   jax.ShapeDtypeStruct((batch, N_EXPERTS, n_tok), F32)],
        compiler_params=_cparams(("arbitrary",)),
        name="outproj",
    )(y_ssm, y_sc, sc_norm_w, w_out, x2d, g1, norm_w, shift, scale, w_rT)


def _prefix_lanes(m01):
    rows, n = m01.shape
    ki = lax.broadcasted_iota(I32, (LANES, LANES), 0)
    ci = lax.broadcasted_iota(I32, (LANES, LANES), 1)
    triu = jnp.where(ki <= ci, 1.0, 0.0).astype(BF16)
    off = jnp.zeros((rows, 1), F32)
    parts = []
    for k in range(n // LANES):
        p = jnp.dot(m01[:, k * LANES:(k + 1) * LANES].astype(BF16), triu,
                    preferred_element_type=F32)
        parts.append(p + off)
        off = off + p[:, LANES - 1:LANES]
    return jnp.concatenate(parts, axis=1)


def _route_kernel(aff_ref, idx_ref, gcol_ref, *, cap):
    a = aff_ref[0]
    n_e, n = a.shape
    thr = jnp.zeros((n_e, 1), I32)
    for bit in range(30, -1, -1):
        cand = thr | (1 << bit)
        cand_f = lax.bitcast_convert_type(cand, F32)
        cnt = jnp.sum(jnp.where(a >= cand_f, 1.0, 0.0), axis=1, keepdims=True)
        thr = jnp.where(cnt >= cap, cand, thr)
    thr_f = lax.bitcast_convert_type(thr, F32)
    gt = a > thr_f
    eq = a == thr_f
    need = cap - jnp.sum(jnp.where(gt, 1.0, 0.0), axis=1, keepdims=True)
    eqf = jnp.where(eq, 1.0, 0.0)
    ties_before = _prefix_lanes(eqf) - eqf
    sel = jnp.logical_or(gt, jnp.logical_and(eq, ties_before < need))
    self_ = jnp.where(sel, 1.0, 0.0)
    slot = jnp.where(sel, _prefix_lanes(self_) - 1.0, -1.0).astype(I32)

    tok = lax.broadcasted_iota(I32, (1, n), 1)
    t_hi = (tok // 64).astype(F32)
    t_lo = (tok % 64).astype(F32)
    a1 = a.astype(BF16).astype(F32)
    ra = a - a1
    a2 = ra.astype(BF16).astype(F32)
    a3 = ra - a2
    srow = lax.broadcasted_iota(I32, (cap, n), 0)
    r8 = lax.broadcasted_iota(I32, (8, n), 0)
    zpad = jnp.zeros((LANES - 8, n), F32)
    for e in range(n_e):
        onehot = jnp.where(slot[e:e + 1, :] == srow, 1.0, 0.0).astype(BF16)
        v8 = jnp.where(r8 == 0, t_hi, jnp.where(r8 == 1, t_lo, jnp.where(
            r8 == 2, a1[e:e + 1], jnp.where(r8 == 3, a2[e:e + 1], jnp.where(
                r8 == 4, a3[e:e + 1], 0.0)))))
        vals = jnp.concatenate([v8, zpad], axis=0).astype(BF16)
        o_col = lax.dot_general(onehot, vals, (((1,), (1,)), ((), ())),
                                preferred_element_type=F32)
        o_row = o_col.T
        idx_ref[0, e:e + 1, :] = (o_row[0:1] * 64.0 + o_row[1:2]).astype(I32)
        gcol_ref[0, e] = o_col


def _route_call(aff, *, cap):
    batch, n_e, n = aff.shape
    return pl.pallas_call(
        functools.partial(_route_kernel, cap=cap),
        grid=(batch,),
        in_specs=[pl.BlockSpec((1, n_e, n), lambda b: (b, 0, 0))],
        out_specs=[pl.BlockSpec((1, n_e, cap), lambda b: (b, 0, 0)),
                   pl.BlockSpec((1, n_e, cap, LANES), lambda b: (b, 0, 0, 0))],
        out_shape=[jax.ShapeDtypeStruct((batch, n_e, cap), I32),
                   jax.ShapeDtypeStruct((batch, n_e, cap, LANES), F32)],
        compiler_params=_cparams(("arbitrary",)),
        name="route",
    )(aff)


MOE_PARTS = 4
MOE_NCHUNK = 512
MOE_WROWS_A = 512
MOE_WROWS_D = 128
MOE_WPIECES = 4
MOE_WPRIO = 1


def _moe_kernel(idx_ref, gcol_ref, g2_ref, wg_hbm, wu_hbm, wd_hbm, fh_in, fh_out,
                buf0, buf1, wg_s, wu_s, wd_s, stg_a, stg_d, sem, *, cap, n_tok, n_pairs):
    e = pl.program_id(0)
    i = pl.program_id(1)
    n_e = pl.num_programs(0)
    first = jnp.logical_and(e == 0, i == 0)
    last = jnp.logical_and(e == n_e - 1, i == n_pairs - 1)
    i_next = jnp.where(i == n_pairs - 1, 0, i + 1)
    e_next = jnp.where(i == n_pairs - 1, e + 1, e)
    bufs = (buf0, buf1)
    part = cap // MOE_PARTS
    hcols = pl.ds(D_MODEL, D_MODEL)
    cur = e % 2

    def row_of(ex, b, jrow):
        return b * n_tok + idx_ref[(b * N_EXPERTS + ex) * cap + jrow]

    def gather_start(ex, b, s, jrow):
        pltpu.make_async_copy(fh_in.at[pl.ds(row_of(ex, b, jrow), 1)], bufs[s].at[pl.ds(jrow, 1)],
                              sem.at[s]).start()

    def scatter_start(ex, b, s, jrow):
        pltpu.make_async_copy(bufs[s].at[pl.ds(jrow, 1), hcols],
                              fh_out.at[pl.ds(row_of(ex, b, jrow), 1), hcols],
                              sem.at[2 + s]).start()

    def wait_gather(s):
        pltpu.make_async_copy(fh_in.at[pl.ds(0, cap)], bufs[s], sem.at[s]).wait()

    def wait_scatter(s):
        pltpu.make_async_copy(bufs[s].at[:, hcols], fh_out.at[pl.ds(0, cap), hcols],
                              sem.at[2 + s]).wait()

    g_units = (D_MODEL // MOE_WROWS_A) // n_pairs
    d_units = (EXPERT_FF // MOE_WROWS_D) // n_pairs
    units = [("g", j) for j in range(g_units)] + [("u", j) for j in range(g_units)] \
        + [("d", j) for j in range(d_units)]
    n_parts = 2 * MOE_PARTS
    upp = len(units) // n_parts
    pieces = MOE_WPIECES // upp

    def unit_refs(kind, j, tgt, step, slot):
        if kind == "d":
            row0 = pl.multiple_of((step * d_units + j) * MOE_WROWS_D, MOE_WROWS_D)
            return (wd_hbm.at[tgt, pl.ds(row0, MOE_WROWS_D), :], stg_d.at[slot], wd_s, row0,
                    MOE_WROWS_D)
        src = wg_hbm if kind == "g" else wu_hbm
        row0 = pl.multiple_of((step * g_units + j) * MOE_WROWS_A, MOE_WROWS_A)
        return (src.at[tgt, pl.ds(row0, MOE_WROWS_A), :], stg_a.at[slot],
                wg_s if kind == "g" else wu_s, row0, MOE_WROWS_A)

    def unit_copy(t, tgt, step):
        kind, j = units[t]
        src, dst, _, _, _ = unit_refs(kind, j, tgt, step, t % 2)
        return pltpu.make_async_copy(src, dst, sem.at[4 + t % 2])

    def unit_cast(t, step, wslot, piece, n_pieces):
        kind, j = units[t]
        _, stg, dst, row0, rows = unit_refs(kind, j, 0, step, t % 2)
        pr = rows // n_pieces
        dst[wslot, pl.ds(row0 + piece * pr, pr), :] = stg[piece * pr:(piece + 1) * pr, :].astype(BF16)

    tgt_now = jnp.minimum(e + 1, n_e - 1)
    tgt_next = jnp.minimum(e_next + 1, n_e - 1)

    def unit_start_ahead(t):
        t2 = t + 2
        if t2 < len(units):
            unit_copy(t2, tgt_now, i).start(priority=MOE_WPRIO)
        else:
            unit_copy(t2 - len(units), tgt_next, i_next).start(priority=MOE_WPRIO)

    def weight_jobs(p):
        jobs = []
        for c in range(upp):
            t = p * upp + c
            for piece in range(pieces):
                def job(t=t, piece=piece):
                    if piece == 0:
                        unit_copy(t, tgt_now, i).wait()
                    unit_cast(t, i, 1 - cur, piece, pieces)
                jobs.append(job)
            jobs.append(functools.partial(unit_start_ahead, t))
        return jobs

    def compute_part(s, b_local, q, between, wjobs):
        rows = slice(q * part, (q + 1) * part)
        xq = bufs[s][rows, 0:D_MODEL].astype(BF16)
        gc = gcol_ref[b_local, 0, rows, :]
        gate = gc[:, 2:3] + gc[:, 3:4] + gc[:, 4:5]
        g2 = g2_ref[2 * i + b_local]
        nc = MOE_NCHUNK
        hid = []
        slot_no = 0

        def tick():
            nonlocal slot_no
            between[slot_no]()
            if slot_no < len(wjobs):
                wjobs[slot_no]()
            slot_no += 1

        for c in range(EXPERT_FF // nc):
            cols = slice(c * nc, (c + 1) * nc)
            tick()
            hg = jnp.dot(xq, wg_s[cur, :, cols], preferred_element_type=F32)
            tick()
            hu = jnp.dot(xq, wu_s[cur, :, cols], preferred_element_type=F32)
            hid.append((_silu(hg) * hu).astype(BF16))
        hid = jnp.concatenate(hid, axis=1)
        for c in range(D_MODEL // nc):
            cols = slice(c * nc, (c + 1) * nc)
            hc = slice(D_MODEL + c * nc, D_MODEL + (c + 1) * nc)
            tick()
            ye = jnp.dot(hid, wd_s[cur, :, cols], preferred_element_type=F32)
            bufs[s][rows, hc] = bufs[s][rows, hc] + g2[:, cols] * (ye * gate)

    n_between = 2 * (EXPERT_FF // MOE_NCHUNK) + D_MODEL // MOE_NCHUNK

    def spread(jobs):
        n = len(jobs)
        cuts = [n * k // n_between for k in range(n_between + 1)]

        def group(k):
            def run():
                for job in jobs[cuts[k]:cuts[k + 1]]:
                    job()
            return run
        return [group(k) for k in range(n_between)]

    b_a, b_b = 2 * i, 2 * i + 1
    b_n = 2 * i_next

    @pl.when(first)
    def _():
        @pl.loop(0, cap)
        def _(jrow):
            gather_start(e, b_a, 0, jrow)
        for step in range(n_pairs):
            for t in range(len(units)):
                kind, j = units[t]
                src, _, _, _, _ = unit_refs(kind, j, 0, step, 0)
                stg0 = stg_d.at[0] if kind == "d" else stg_a.at[0]
                cp = pltpu.make_async_copy(src, stg0, sem.at[4])
                cp.start()
                cp.wait()
                _, _, dst, row0, rows = unit_refs(kind, j, 0, step, 0)
                dst[0, pl.ds(row0, rows), :] = (stg_d[0] if kind == "d" else stg_a[0]).astype(BF16)
        unit_copy(0, tgt_now, i).start(priority=MOE_WPRIO)
        unit_copy(1, tgt_now, i).start(priority=MOE_WPRIO)

    wait_gather(0)
    fill = {1: range(0, cap // 2), 2: range(cap // 2, cap)}
    for q in range(MOE_PARTS):
        jobs = []
        if q > 0:
            jobs += [functools.partial(scatter_start, e, b_a, 0, jrow)
                     for jrow in range((q - 1) * part, q * part)]
        jobs += [functools.partial(gather_start, e, b_b, 1, jrow) for jrow in fill.get(q, ())]
        if q == 1:
            @pl.when(jnp.logical_not(first))
            def _():
                wait_scatter(1)
        compute_part(0, 0, q, spread(jobs), weight_jobs(q))

    wait_gather(1)
    for q in range(MOE_PARTS):
        if q == 0:
            jobs = [functools.partial(scatter_start, e, b_a, 0, jrow)
                    for jrow in range((MOE_PARTS - 1) * part, cap)]
        else:
            jobs = [functools.partial(scatter_start, e, b_b, 1, jrow)
                    for jrow in range((q - 1) * part, q * part)]
        jobs += [functools.partial(gather_start, e_next, b_n, 0, jrow) for jrow in fill.get(q, ())]
        if q == 1:
            wait_scatter(0)
        compute_part(1, 1, q, spread(jobs), weight_jobs(MOE_PARTS + q))

    for jrow in range((MOE_PARTS - 1) * part, cap):
        scatter_start(e, b_b, 1, jrow)

    @pl.when(last)
    def _():
        wait_scatter(1)
        wait_gather(0)
        unit_copy(0, tgt_next, i_next).wait()
        unit_copy(1, tgt_next, i_next).wait()


def _moe_call(idx_flat, gcol, g2, wg, wu, wd, fh, *, batch, cap):
    m, d = fh.shape[0], fh.shape[1] // 2
    n_tok = m // batch
    assert batch % 2 == 0 and cap % MOE_PARTS == 0
    n_pairs = batch // 2
    assert n_pairs in (1, 2)
    kern = functools.partial(_moe_kernel, cap=cap, n_tok=n_tok, n_pairs=n_pairs)
    grid_spec = pltpu.PrefetchScalarGridSpec(
        num_scalar_prefetch=1,
        grid=(N_EXPERTS, n_pairs),
        in_specs=[pl.BlockSpec((2, 1, cap, LANES), lambda e, i, idx: (i, e, 0, 0)),
                  pl.BlockSpec((batch, 1, d), lambda e, i, idx: (0, 0, 0)),
                  pl.BlockSpec(memory_space=pl.ANY), pl.BlockSpec(memory_space=pl.ANY),
                  pl.BlockSpec(memory_space=pl.ANY), pl.BlockSpec(memory_space=pl.ANY)],
        out_specs=pl.BlockSpec(memory_space=pl.ANY),
        scratch_shapes=[pltpu.VMEM((cap, 2 * d), F32), pltpu.VMEM((cap, 2 * d), F32),
                        pltpu.VMEM((2, d, EXPERT_FF), BF16), pltpu.VMEM((2, d, EXPERT_FF), BF16),
                        pltpu.VMEM((2, EXPERT_FF, d), BF16),
                        pltpu.VMEM((2, MOE_WROWS_A, EXPERT_FF), F32),
                        pltpu.VMEM((2, MOE_WROWS_D, d), F32),
                        pltpu.SemaphoreType.DMA((6,))],
    )
    return pl.pallas_call(
        kern,
        grid_spec=grid_spec,
        out_shape=jax.ShapeDtypeStruct((m, 2 * d), F32),
        input_output_aliases={6: 0},
        compiler_params=_cparams(("arbitrary", "arbitrary")),
        name="moe",
    )(idx_flat, gcol, g2, wg, wu, wd, fh)


def _final_kernel(h_ref, w_ref, o_ref):
    h = h_ref[...]
    ms = jnp.mean(h * h, axis=-1, keepdims=True)
    o_ref[...] = h * lax.rsqrt(ms + EPS) * w_ref[...]


def _final_call(fh, w, *, tm):
    m, d = fh.shape[0], fh.shape[1] // 2
    return pl.pallas_call(
        _final_kernel,
        grid=(m // tm,),
        in_specs=[pl.BlockSpec((tm, d), lambda i: (i, 1)), pl.BlockSpec((1, d), lambda i: (0, 0))],
        out_specs=pl.BlockSpec((tm, d), lambda i: (i, 0)),
        out_shape=jax.ShapeDtypeStruct((m, d), F32),
        compiler_params=_cparams(("arbitrary",)),
        name="final_norm",
    )(fh, w)


def _pick_tile(n, pref):
    t = pref
    while n % t:
        t //= 2
    return t


def _ssd_params(a_log, dt_bias, d_skip, norm_w):
    padw = LANES - N_DT
    alog = jnp.pad(a_log.astype(F32), (0, padw))
    bias = jnp.pad(dt_bias.astype(F32), (0, padw))
    lane = jnp.arange(LANES)[:, None]
    head = jnp.arange(D_SSM)[None, :] // SSM_HEAD_DIM
    return {
        "alog_row": alog[None, :], "alog_col": alog[:, None],
        "bias_row": bias[None, :], "bias_col": bias[:, None],
        "expand_fwd": (lane == head).astype(BF16),
        "expand_rev": (lane == head + SSM_HEADS).astype(BF16),
        "dskip": jnp.repeat(d_skip.astype(F32), SSM_HEAD_DIM)[None, :],
        "norm_w": norm_w.astype(F32)[None, :],
    }


def kernel(x, c, ctx, c_ctx, w_mod, b_mod, norm_mix_w, w_in, ssm_conv_w, ssm_conv_b, ssm_a_log,
           ssm_dt_bias, ssm_d, ssm_norm_w, sc_conv_w, sc_norm_w, w_out, norm_ffn_w, w_router,
           w_gate, w_up, w_down, final_norm_w):
    batch, n_lat, d = x.shape
    n_ctx = ctx.shape[1]
    depth = w_mod.shape[0]
    assert depth == 1 and d == D_MODEL and n_ctx == BLK and n_lat % BLK == 0
    layer = 0
    cap = EC_CAPACITY_FACTOR * n_lat // N_EXPERTS

    pad_rows = -(batch + 1) % 8
    cvec = jnp.concatenate([c, c_ctx[None, :], jnp.zeros((pad_rows, d), F32)], axis=0)
    mod = _mod_call(cvec, w_mod[layer], b_mod[layer][None, :])
    sh1, sc1, g1, sh2, sc2, g2 = [mod[:batch, k * d:(k + 1) * d][:, None, :] for k in range(N_MOD)]
    sh1_c = mod[batch:batch + 1, 0:d][:, None, :]
    sc1_c = mod[batch:batch + 1, d:2 * d][:, None, :]

    w_inT = w_in[layer].T.astype(BF16)
    tc = 512
    w_out_b = w_out[layer].astype(BF16)
    w_rT = w_router[layer].T.astype(BF16)
    wg, wu, wd = w_gate[layer], w_up[layer], w_down[layer]
    norm_mix = norm_mix_w[layer][None, :]
    conv_w, conv_b = ssm_conv_w[layer], ssm_conv_b[layer][None, :]
    prm = _ssd_params(ssm_a_log[layer], ssm_dt_bias[layer], ssm_d[layer], ssm_norm_w[layer])

    ctx2d = ctx.reshape(batch * n_ctx, d)
    xbc_c, dt_c, dtT_c = _inproj_zx_call(
        ctx2d, sh1_c, sc1_c, norm_mix, w_inT, conv_w, conv_b, prm["bias_row"], prm["bias_col"],
        with_z=False, period=n_ctx, tm=_pick_tile(batch * n_ctx, 1024), rows_per_mod=batch * n_ctx)
    zero_state = jnp.zeros((batch, SSM_GROUPS, SSM_STATE, GROUP_W), F32)
    st_f = _ssd_call(xbc_c, dt_c, dtT_c, prm, zero_state, batch=batch, nblk=1, reverse=False,
                     mode="state")
    st_r = _ssd_call(xbc_c, dt_c, dtT_c, prm, zero_state, batch=batch, nblk=1, reverse=True,
                     mode="state")

    x2d = x.reshape(batch * n_lat, d)
    tm = _pick_tile(n_lat, 1024)
    z, xbc, dt, dtT, a_lat = _inproj_zx_call(
        x2d, sh1, sc1, norm_mix, w_inT, conv_w, conv_b, prm["bias_row"], prm["bias_col"],
        with_z=True, period=GRID_W, tm=tm, rows_per_mod=n_lat)
    y_sc = _inproj_sc_call(a_lat, w_inT, sc_conv_w[layer], tm=tm, tc=tc)
    nblk = n_lat // BLK
    y_rev = _ssd_call(xbc, dt, dtT, prm, st_r, batch=batch, nblk=nblk, reverse=True, mode="partial")
    y_ssm = _ssd_call(xbc, dt, dtT, prm, st_f, batch=batch, nblk=nblk, reverse=False, mode="final",
                      extra=(z, y_rev))

    fh, aff = _outproj_call(
        y_ssm, y_sc, sc_norm_w[layer][None, :], w_out_b, x2d, g1, norm_ffn_w[layer][None, :],
        sh2, sc2, w_rT, batch=batch, tm=_pick_tile(n_lat, 512))
    idx, gcol = _route_call(aff, cap=cap)
    fh = _moe_call(idx.reshape(-1), gcol, g2, wg, wu, wd, fh, batch=batch, cap=cap)
    out = _final_call(fh, final_norm_w[None, :], tm=_pick_tile(batch * n_lat, 512))
    return out.reshape(batch, n_lat, d)
```

```python
import functools

import jax
import jax.numpy as jnp
from jax import lax
from jax.experimental import pallas as pl
from jax.experimental.pallas import tpu as pltpu

F32 = jnp.float32
BF16 = jnp.bfloat16
I32 = jnp.int32

D_MODEL = 2048
GRID_W = 64
D_SSM = 2048
SSM_HEAD_DIM = 64
SSM_HEADS = D_SSM // SSM_HEAD_DIM
SSM_GROUPS = 4
SSM_HPG = SSM_HEADS // SSM_GROUPS
SSM_STATE = 128
SSD_CHUNK = 128
GN = SSM_GROUPS * SSM_STATE
D_XBC = D_SSM + 2 * GN
D_ZX = D_SSM + D_XBC
N_DT = 2 * SSM_HEADS
D_SSM_IN = D_ZX + N_DT
D_SC = 2048
N_EXPERTS = 16
EC_CAPACITY_FACTOR = 2
EXPERT_FF = 1024
N_MOD = 6
EPS = 1e-6
LOG2E = 1.4426950408889634

LANES = 128
BLK = 2 * SSD_CHUNK
GROUP_W = SSM_HPG * SSM_HEAD_DIM
VMEM_LIMIT = 56 * 1024 * 1024


def _cparams(sem):
    return pltpu.CompilerParams(dimension_semantics=sem, vmem_limit_bytes=VMEM_LIMIT)


def _silu(v):
    return v * jax.nn.sigmoid(v)


def _softplus(v):
    return jnp.maximum(v, 0.0) + jnp.log1p(jnp.exp(-jnp.abs(v)))


def _mod_kernel(c_ref, w_ref, b_ref, o_ref):
    s = _silu(c_ref[...]).astype(BF16)
    o_ref[...] = jnp.dot(s, w_ref[...].astype(BF16), preferred_element_type=F32) + b_ref[...]


def _mod_call(cvec, w_mod, b_mod):
    rows, d = cvec.shape
    n = w_mod.shape[1]
    tn = 1024
    return pl.pallas_call(
        _mod_kernel,
        grid=(n // tn,),
        in_specs=[pl.BlockSpec((rows, d), lambda j: (0, 0)),
                  pl.BlockSpec((d, tn), lambda j: (0, j)),
                  pl.BlockSpec((1, tn), lambda j: (0, j))],
        out_specs=pl.BlockSpec((rows, tn), lambda j: (0, j)),
        out_shape=jax.ShapeDtypeStruct((rows, n), F32),
        compiler_params=_cparams(("arbitrary",)),
        name="mod",
    )(cvec, w_mod, b_mod)


ZX_SUB = 1024
SC_SUB = 256


def _dot_nt(a, bt):
    return lax.dot_general(a, bt, (((1,), (1,)), ((), ())), preferred_element_type=F32)


def _norm_modulate(h, nw, shift, scale):
    ms = jnp.mean(h * h, axis=-1, keepdims=True)
    hn = h * lax.rsqrt(ms + EPS) * nw
    return hn * (1.0 + scale) + shift


def _conv3(v, cw, period):
    rows = v.shape[0]
    r = lax.broadcasted_iota(I32, (rows, 1), 0) % period
    prev = jnp.where(r == 0, 0.0, pltpu.roll(v, 1, 0))
    nxt = jnp.where(r == period - 1, 0.0, pltpu.roll(v, rows - 1, 0))
    return prev * cw[0:1] + v * cw[1:2] + nxt * cw[2:3]


def _inproj_zx_kernel(h_ref, sh_ref, sc_ref, nw_ref, w_ref, wdt_ref, cw_ref, cb_ref, br_ref, bc_ref,
                      *rest, with_z, period, tm, n_z):
    if with_z:
        z_ref, xbc_ref, dt_ref, dtT_ref, ao_ref, a_ref = rest
    else:
        xbc_ref, dt_ref, dtT_ref, a_ref = rest
    n = pl.program_id(1)

    @pl.when(n == 0)
    def _():
        a = _norm_modulate(h_ref[...], nw_ref[...], sh_ref[0], sc_ref[0]).astype(BF16)
        a_ref[...] = a
        if with_z:
            ao_ref[...] = a
        wdt = wdt_ref[...]
        dt_ref[...] = _softplus(_dot_nt(a, wdt) + br_ref[...])
        dtT_ref[...] = _softplus(_dot_nt(wdt, a) + bc_ref[...])

    wb = w_ref[...]
    sub = min(tm, ZX_SUB)

    if with_z:
        @pl.when(n < n_z)
        def _():
            for k in range(tm // sub):
                rows = slice(k * sub, (k + 1) * sub)
                z_ref[rows, :] = _dot_nt(a_ref[rows, :], wb).astype(BF16)

    @pl.when(n >= n_z)
    def _():
        for k in range(tm // sub):
            rows = slice(k * sub, (k + 1) * sub)
            acc = _dot_nt(a_ref[rows, :], wb)
            y = _conv3(acc, cw_ref[...], period) + cb_ref[...]
            xbc_ref[rows, :] = _silu(y).astype(BF16)


def _inproj_zx_call(h2d, shift, scale, norm_w, w_inT, conv_w, conv_b, bias_row, bias_col, *,
                    with_z, period, tm, rows_per_mod):
    m, d = h2d.shape
    tn = 1024
    n_z = D_SSM // tn if with_z else 0
    n_x = D_XBC // tn
    n_off = 0 if with_z else D_SSM // tn
    per = rows_per_mod // tm
    kern = functools.partial(_inproj_zx_kernel, with_z=with_z, period=period, tm=tm, n_z=n_z)
    out_shape = [jax.ShapeDtypeStruct((m, D_XBC), BF16),
                 jax.ShapeDtypeStruct((m, LANES), F32),
                 jax.ShapeDtypeStruct((LANES, m), F32)]
    out_specs = [pl.BlockSpec((tm, tn), lambda i, n: (i, jnp.maximum(n - n_z, 0))),
                 pl.BlockSpec((tm, LANES), lambda i, n: (i, 0)),
                 pl.BlockSpec((LANES, tm), lambda i, n: (0, i))]
    if with_z:
        out_shape = ([jax.ShapeDtypeStruct((m, D_SSM), BF16)] + out_shape
                     + [jax.ShapeDtypeStruct((m, d), BF16)])
        out_specs = ([pl.BlockSpec((tm, tn), lambda i, n: (i, jnp.minimum(n, n_z - 1)))] + out_specs
                     + [pl.BlockSpec((tm, d), lambda i, n: (i, 0))])
    return pl.pallas_call(
        kern,
        grid=(m // tm, n_z + n_x),
        in_specs=[pl.BlockSpec((tm, d), lambda i, n: (i, 0)),
                  pl.BlockSpec((1, 1, d), lambda i, n: (i // per, 0, 0)),
                  pl.BlockSpec((1, 1, d), lambda i, n: (i // per, 0, 0)),
                  pl.BlockSpec((1, d), lambda i, n: (0, 0)),
                  pl.BlockSpec((tn, d), lambda i, n: (n + n_off, 0)),
                  pl.BlockSpec((LANES, d), lambda i, n: (D_ZX // LANES, 0)),
                  pl.BlockSpec((3, tn), lambda i, n: (0, jnp.maximum(n - n_z, 0))),
                  pl.BlockSpec((1, tn), lambda i, n: (0, jnp.maximum(n - n_z, 0))),
                  pl.BlockSpec((1, LANES), lambda i, n: (0, 0)),
                  pl.BlockSpec((LANES, 1), lambda i, n: (0, 0))],
        out_specs=out_specs,
        out_shape=out_shape,
        scratch_shapes=[pltpu.VMEM((tm, d), BF16)],
        compiler_params=_cparams(("arbitrary", "arbitrary")),
        name="inproj_zx" if with_z else "inproj_ctx",
    )(h2d, shift, scale, norm_w, w_inT, w_inT, conv_w, conv_b, bias_row, bias_col)


def _inproj_sc_kernel(a_ref, wb_ref, wc_ref, wv_ref, cw_ref, o_ref, *, tm):
    wb, wc, wv = wb_ref[...], wc_ref[...], wv_ref[...]
    for k in range(tm // SC_SUB):
        rows = slice(k * SC_SUB, (k + 1) * SC_SUB)
        a = a_ref[rows, :]
        cg = _dot_nt(a, wc)
        hv = _dot_nt(a, wv)
        v = _conv3(cg * hv, cw_ref[...], GRID_W)
        bg = _dot_nt(a, wb)
        o_ref[rows, :] = (bg * v).astype(BF16)


def _inproj_sc_call(a2d, w_inT, conv_w, *, tm, tc):
    m, d = a2d.shape
    kern = functools.partial(_inproj_sc_kernel, tm=tm)

    def wspec(k):
        return pl.BlockSpec((pl.Element(tc), pl.Element(d)),
                            lambda i, n: (pl.multiple_of(D_SSM_IN + k * D_SC + n * tc, 16), 0))

    return pl.pallas_call(
        kern,
        grid=(m // tm, D_SC // tc),
        in_specs=[pl.BlockSpec((tm, d), lambda i, n: (i, 0)),
                  wspec(0), wspec(1), wspec(2),
                  pl.BlockSpec((3, tc), lambda i, n: (0, n))],
        out_specs=pl.BlockSpec((tm, tc), lambda i, n: (i, n)),
        out_shape=jax.ShapeDtypeStruct((m, D_SC), BF16),
        compiler_params=_cparams(("arbitrary", "arbitrary")),
        name="inproj_sc",
    )(a2d, w_inT, w_inT, w_inT, conv_w)


def _ssd_kernel(*refs, reverse, mode, nblk):
    xbc_ref, dt_ref, dtT_ref, alr_ref, alc_ref, e_ref, init_ref = refs[:7]
    rest = refs[7:]
    if mode == "state":
        out_ref, s_ref = rest
    elif mode == "partial":
        out_ref, s_ref = rest
    else:
        z_ref, yo_ref, dsk_ref, nw_ref, out_ref, s_ref = rest
    j = pl.program_id(1)

    @pl.when(j == 0)
    def _():
        s_ref[...] = init_ref[0]

    ii = lax.broadcasted_iota(I32, (SSD_CHUNK, SSD_CHUNK), 0)
    jj = lax.broadcasted_iota(I32, (SSD_CHUNK, SSD_CHUNK), 1)
    tri = jnp.where(jj <= ii, 1.0, 0.0).astype(F32)
    triT = jnp.where(ii <= jj, 1.0, 0.0).astype(F32)
    mask = (jj >= ii) if reverse else (ii >= jj)
    lane4 = lax.broadcasted_iota(I32, (SSD_CHUNK, 4 * SSM_HEAD_DIM), 1) // SSM_HEAD_DIM
    aneg_r = -jnp.exp(alr_ref[...]) * LOG2E
    aneg_c = -jnp.exp(alc_ref[...]) * LOG2E
    lane0 = SSM_HEADS if reverse else 0
    hi = lax.Precision.HIGHEST

    def chunk(c):
        rows = slice(c * SSD_CHUNK, (c + 1) * SSD_CHUNK)
        dt = dt_ref[rows, :]
        dtT = dtT_ref[:, rows]
        la = dt * aneg_r
        laT = dtT * aneg_c
        cs = jnp.dot(tri, la, precision=hi, preferred_element_type=F32)
        csT = jnp.dot(laT, triT, precision=hi, preferred_element_type=F32)
        total = cs[SSD_CHUNK - 1:SSD_CHUNK, :]
        log2_dtT = jnp.log(dtT) * LOG2E
        if reverse:
            ecs = cs - la
            col, rowm = -ecs, csT - laT + log2_dtT
            yscale = jnp.exp2(total - ecs)
            w = dt * jnp.exp2(ecs)
        else:
            col, rowm = cs, log2_dtT - csT
            yscale = jnp.exp2(cs)
            w = dt * jnp.exp2(total - cs)
        t1 = total.astype(BF16).astype(F32)
        r1 = total - t1
        t2 = r1.astype(BF16).astype(F32)
        t3 = r1 - t2
        r16 = lax.broadcasted_iota(I32, (16, LANES), 0)
        tail = jnp.where(r16 == 0, t1, jnp.where(r16 == 1, t2, jnp.where(r16 == 2, t3, 0.0)))
        stack = jnp.concatenate([yscale, w, tail], axis=0).astype(BF16)
        ex = jnp.dot(stack, e_ref[...], preferred_element_type=F32)
        yscale_x = ex[0:SSD_CHUNK]
        w_x = ex[SSD_CHUNK:2 * SSD_CHUNK]
        tb = 2 * SSD_CHUNK
        sdec_x = jnp.exp2(ex[tb:tb + 1] + ex[tb + 1:tb + 2] + ex[tb + 2:tb + 3])

        for g in range(SSM_GROUPS):
            gsl = slice(g * GROUP_W, (g + 1) * GROUP_W)
            xg = xbc_ref[rows, gsl]
            bg = xbc_ref[rows, D_SSM + g * SSM_STATE:D_SSM + (g + 1) * SSM_STATE]
            s_old = s_ref[g]
            if mode != "state":
                cg = xbc_ref[rows, D_SSM + GN + g * SSM_STATE:D_SSM + GN + (g + 1) * SSM_STATE]
                cb = lax.dot_general(cg, bg, (((1,), (1,)), ((), ())), preferred_element_type=F32)
                y_off = jnp.dot(cg, s_old.astype(BF16), preferred_element_type=F32)
                halves = []
                for q in range(2):
                    xq = xg[:, q * 256:(q + 1) * 256]
                    mhs, xms = [], []
                    for r4 in range(4):
                        lane = lane0 + g * SSM_HPG + q * 4 + r4
                        seg = col[:, lane:lane + 1] + rowm[lane:lane + 1, :]
                        dec = jnp.exp2(jnp.where(mask, seg, -jnp.inf))
                        mhs.append((cb * dec).astype(BF16))
                        xms.append(jnp.where(lane4 == r4, xq, jnp.zeros_like(xq)))
                    halves.append(jnp.dot(jnp.concatenate(mhs, axis=1), jnp.concatenate(xms, axis=0),
                                          preferred_element_type=F32))
                y = jnp.concatenate(halves, axis=1) + yscale_x[:, gsl] * y_off
                if mode == "partial":
                    out_ref[rows, gsl] = y.astype(BF16)
                else:
                    y = y + yo_ref[rows, gsl].astype(F32) + dsk_ref[:, gsl] * xg.astype(F32)
                    y = y * _silu(z_ref[rows, gsl].astype(F32))
                    ms = jnp.mean(y * y, axis=-1, keepdims=True)
                    out_ref[rows, gsl] = (y * lax.rsqrt(ms + EPS) * nw_ref[:, gsl]).astype(BF16)
            xw = (xg.astype(F32) * w_x[:, gsl]).astype(BF16)
            upd = lax.dot_general(bg, xw, (((0,), (0,)), ((), ())), preferred_element_type=F32)
            s_ref[g] = sdec_x[:, gsl] * s_old + upd

    for c in ((1, 0) if reverse else (0, 1)):
        chunk(c)

    if mode == "state":
        @pl.when(j == nblk - 1)
        def _():
            out_ref[0] = s_ref[...]


def _ssd_call(xbc, dt, dtT, prm, init, *, batch, nblk, reverse, mode, extra=()):
    m = xbc.shape[0]

    def blk(b, j):
        return b * nblk + (nblk - 1 - j if reverse else j)

    state_spec = pl.BlockSpec((1, SSM_GROUPS, SSM_STATE, GROUP_W), lambda b, j: (b, 0, 0, 0))
    const2 = lambda b, j: (0, 0)
    in_specs = [pl.BlockSpec((BLK, D_XBC), lambda b, j: (blk(b, j), 0)),
                pl.BlockSpec((BLK, LANES), lambda b, j: (blk(b, j), 0)),
                pl.BlockSpec((LANES, BLK), lambda b, j: (0, blk(b, j))),
                pl.BlockSpec((1, LANES), const2), pl.BlockSpec((LANES, 1), const2),
                pl.BlockSpec((LANES, D_SSM), const2),
                state_spec]
    args = [xbc, dt, dtT, prm["alog_row"], prm["alog_col"],
            prm["expand_rev" if reverse else "expand_fwd"], init]
    if mode == "state":
        out_specs = state_spec
        out_shape = jax.ShapeDtypeStruct((batch, SSM_GROUPS, SSM_STATE, GROUP_W), F32)
    else:
        out_specs = pl.BlockSpec((BLK, D_SSM), lambda b, j: (blk(b, j), 0))
        out_shape = jax.ShapeDtypeStruct((m, D_SSM), BF16)
    if mode == "final":
        z, y_other = extra
        in_specs += [pl.BlockSpec((BLK, D_SSM), lambda b, j: (blk(b, j), 0)),
                     pl.BlockSpec((BLK, D_SSM), lambda b, j: (blk(b, j), 0)),
                     pl.BlockSpec((1, D_SSM), const2), pl.BlockSpec((1, D_SSM), const2)]
        args += [z, y_other, prm["dskip"], prm["norm_w"]]
    kern = functools.partial(_ssd_kernel, reverse=reverse, mode=mode, nblk=nblk)
    return pl.pallas_call(
        kern,
        grid=(batch, nblk),
        in_specs=in_specs,
        out_specs=out_specs,
        out_shape=out_shape,
        scratch_shapes=[pltpu.VMEM((SSM_GROUPS, SSM_STATE, GROUP_W), F32)],
        compiler_params=_cparams(("arbitrary", "arbitrary")),
        name=f"ssd_{mode}_{'rev' if reverse else 'fwd'}",
    )(*args)


OUT_NCHUNK = 1024


def _outproj_kernel(ys_ref, yc_ref, scw_ref, w_ref, x_ref, g1_ref, nw_ref, sh_ref, sc_ref, wr_ref,
                    fh_ref, aff_ref, *, tm, sub):
    nc = OUT_NCHUNK
    chunks = [slice(c * nc, (c + 1) * nc) for c in range(D_MODEL // nc)]
    g1, nw, sh, sc = g1_ref[0], nw_ref[...], sh_ref[0], sc_ref[0]

    def rows_of(k):
        return slice(k * sub, (k + 1) * sub)

    def prologue(k):
        yc = yc_ref[rows_of(k), :].astype(F32)
        ms = jnp.mean(yc * yc, axis=-1, keepdims=True)
        return ys_ref[rows_of(k), :], (yc * lax.rsqrt(ms + EPS) * scw_ref[...]).astype(BF16)

    def pass1(k, ys, ycn, cols):
        mix = jnp.dot(ys, w_ref[0:D_SSM, cols], preferred_element_type=F32)
        mix = mix + jnp.dot(ycn, w_ref[D_SSM:, cols], preferred_element_type=F32)
        h = x_ref[rows_of(k), cols] + g1[:, cols] * mix
        fh_ref[rows_of(k), slice(D_MODEL + cols.start, D_MODEL + cols.stop)] = h
        return jnp.sum(h * h, axis=-1, keepdims=True)

    def pass2(k, rstd, cols):
        h = fh_ref[rows_of(k), slice(D_MODEL + cols.start, D_MODEL + cols.stop)]
        f = h * rstd * nw[:, cols] * (1.0 + sc[:, cols]) + sh[:, cols]
        fh_ref[rows_of(k), cols] = f
        return _dot_nt(wr_ref[:, cols], f.astype(BF16))

    def finish(k, logits):
        mx = jnp.max(logits, axis=0, keepdims=True)
        ex = jnp.exp(logits - mx)
        aff_ref[0, :, rows_of(k)] = ex / jnp.sum(ex, axis=0, keepdims=True)

    prev = None
    for k in range(tm // sub):
        ys, ycn = prologue(k)
        ssq, logits = None, None
        for cols in chunks:
            part = pass1(k, ys, ycn, cols)
            ssq = part if ssq is None else ssq + part
            if prev is not None:
                lg = pass2(prev[0], prev[1], cols)
                logits = lg if logits is None else logits + lg
        if prev is not None:
            finish(prev[0], logits)
        prev = (k, lax.rsqrt(ssq * (1.0 / D_MODEL) + EPS))
    logits = None
    for cols in chunks:
        lg = pass2(prev[0], prev[1], cols)
        logits = lg if logits is None else logits + lg
    finish(prev[0], logits)


def _outproj_call(y_ssm, y_sc, sc_norm_w, w_out, x2d, g1, norm_w, shift, scale, w_rT, *, batch, tm):
    m, d = x2d.shape
    n_tok = m // batch
    per = n_tok // tm
    vec = lambda i: (i // per, 0, 0)
    const2 = lambda i: (0, 0)
    sub = min(tm, 256)
    return pl.pallas_call(
        functools.partial(_outproj_kernel, tm=tm, sub=sub),
        grid=(m // tm,),
        in_specs=[pl.BlockSpec((tm, D_SSM), lambda i: (i, 0)),
                  pl.BlockSpec((tm, D_SC), lambda i: (i, 0)),
                  pl.BlockSpec((1, D_SC), const2),
                  pl.BlockSpec((D_SSM + D_SC, d), const2, pipeline_mode=pl.Buffered(1)),
                  pl.BlockSpec((tm, d), lambda i: (i, 0)),
                  pl.BlockSpec((1, 1, d), vec),
                  pl.BlockSpec((1, d), const2),
                  pl.BlockSpec((1, 1, d), vec),
                  pl.BlockSpec((1, 1, d), vec),
                  pl.BlockSpec((N_EXPERTS, d), const2)],
        out_specs=[pl.BlockSpec((tm, 2 * d), lambda i: (i, 0)),
                   pl.BlockSpec((1, N_EXPERTS, tm), lambda i: (i // per, 0, i % per))],
        out_shape=[jax.ShapeDtypeStruct((m, 2 * d), F32),
                   jax.ShapeDtypeStruct((batch, N_EXPERTS, n_tok), F32)],
        compiler_params=_cparams(("arbitrary",)),
        name="outproj",
    )(y_ssm, y_sc, sc_norm_w, w_out, x2d, g1, norm_w, shift, scale, w_rT)


def _prefix_lanes(m01):
    rows, n = m01.shape
    ki = lax.broadcasted_iota(I32, (LANES, LANES), 0)
    ci = lax.broadcasted_iota(I32, (LANES, LANES), 1)
    triu = jnp.where(ki <= ci, 1.0, 0.0).astype(BF16)
    off = jnp.zeros((rows, 1), F32)
    parts = []
    for k in range(n // LANES):
        p = jnp.dot(m01[:, k * LANES:(k + 1) * LANES].astype(BF16), triu,
                    preferred_element_type=F32)
        parts.append(p + off)
        off = off + p[:, LANES - 1:LANES]
    return jnp.concatenate(parts, axis=1)


def _route_kernel(aff_ref, idx_ref, gcol_ref, *, cap):
    a = aff_ref[0]
    n_e, n = a.shape
    thr = jnp.zeros((n_e, 1), I32)
    for bit in range(30, -1, -1):
        cand = thr | (1 << bit)
        cand_f = lax.bitcast_convert_type(cand, F32)
        cnt = jnp.sum(jnp.where(a >= cand_f, 1.0, 0.0), axis=1, keepdims=True)
        thr = jnp.where(cnt >= cap, cand, thr)
    thr_f = lax.bitcast_convert_type(thr, F32)
    gt = a > thr_f
    eq = a == thr_f
    need = cap - jnp.sum(jnp.where(gt, 1.0, 0.0), axis=1, keepdims=True)
    eqf = jnp.where(eq, 1.0, 0.0)
    ties_before = _prefix_lanes(eqf) - eqf
    sel = jnp.logical_or(gt, jnp.logical_and(eq, ties_before < need))
    self_ = jnp.where(sel, 1.0, 0.0)
    slot = jnp.where(sel, _prefix_lanes(self_) - 1.0, -1.0).astype(I32)

    tok = lax.broadcasted_iota(I32, (1, n), 1)
    t_hi = (tok // 64).astype(F32)
    t_lo = (tok % 64).astype(F32)
    a1 = a.astype(BF16).astype(F32)
    ra = a - a1
    a2 = ra.astype(BF16).astype(F32)
    a3 = ra - a2
    srow = lax.broadcasted_iota(I32, (cap, n), 0)
    r8 = lax.broadcasted_iota(I32, (8, n), 0)
    zpad = jnp.zeros((LANES - 8, n), F32)
    for e in range(n_e):
        onehot = jnp.where(slot[e:e + 1, :] == srow, 1.0, 0.0).astype(BF16)
        v8 = jnp.where(r8 == 0, t_hi, jnp.where(r8 == 1, t_lo, jnp.where(
            r8 == 2, a1[e:e + 1], jnp.where(r8 == 3, a2[e:e + 1], jnp.where(
                r8 == 4, a3[e:e + 1], 0.0)))))
        vals = jnp.concatenate([v8, zpad], axis=0).astype(BF16)
        o_col = lax.dot_general(onehot, vals, (((1,), (1,)), ((), ())),
                                preferred_element_type=F32)
        o_row = o_col.T
        idx_ref[0, e:e + 1, :] = (o_row[0:1] * 64.0 + o_row[1:2]).astype(I32)
        gcol_ref[0, e] = o_col


def _route_call(aff, *, cap):
    batch, n_e, n = aff.shape
    return pl.pallas_call(
        functools.partial(_route_kernel, cap=cap),
        grid=(batch,),
        in_specs=[pl.BlockSpec((1, n_e, n), lambda b: (b, 0, 0))],
        out_specs=[pl.BlockSpec((1, n_e, cap), lambda b: (b, 0, 0)),
                   pl.BlockSpec((1, n_e, cap, LANES), lambda b: (b, 0, 0, 0))],
        out_shape=[jax.ShapeDtypeStruct((batch, n_e, cap), I32),
                   jax.ShapeDtypeStruct((batch, n_e, cap, LANES), F32)],
        compiler_params=_cparams(("arbitrary",)),
        name="route",
    )(aff)


MOE_PARTS = 4
MOE_NCHUNK = 512
MOE_WROWS_A = 512
MOE_WROWS_D = 128
MOE_WPIECES = 4
MOE_WPRIO = 1


def _moe_kernel(idx_ref, gcol_ref, g2_ref, wg_hbm, wu_hbm, wd_hbm, fh_in, fh_out,
                buf0, buf1, wg_s, wu_s, wd_s, stg_a, stg_d, sem, *, cap, n_tok, n_pairs):
    e = pl.program_id(0)
    i = pl.program_id(1)
    n_e = pl.num_programs(0)
    first = jnp.logical_and(e == 0, i == 0)
    last = jnp.logical_and(e == n_e - 1, i == n_pairs - 1)
    i_next = jnp.where(i == n_pairs - 1, 0, i + 1)
    e_next = jnp.where(i == n_pairs - 1, e + 1, e)
    bufs = (buf0, buf1)
    part = cap // MOE_PARTS
    hcols = pl.ds(D_MODEL, D_MODEL)
    cur = e % 2

    def row_of(ex, b, jrow):
        return b * n_tok + idx_ref[(b * N_EXPERTS + ex) * cap + jrow]

    def gather_start(ex, b, s, jrow):
        pltpu.make_async_copy(fh_in.at[pl.ds(row_of(ex, b, jrow), 1)], bufs[s].at[pl.ds(jrow, 1)],
                              sem.at[s]).start()

    def scatter_start(ex, b, s, jrow):
        pltpu.make_async_copy(bufs[s].at[pl.ds(jrow, 1), hcols],
                              fh_out.at[pl.ds(row_of(ex, b, jrow), 1), hcols],
                              sem.at[2 + s]).start()

    def wait_gather(s):
        pltpu.make_async_copy(fh_in.at[pl.ds(0, cap)], bufs[s], sem.at[s]).wait()

    def wait_scatter(s):
        pltpu.make_async_copy(bufs[s].at[:, hcols], fh_out.at[pl.ds(0, cap), hcols],
                              sem.at[2 + s]).wait()

    g_units = (D_MODEL // MOE_WROWS_A) // n_pairs
    d_units = (EXPERT_FF // MOE_WROWS_D) // n_pairs
    units = [("g", j) for j in range(g_units)] + [("u", j) for j in range(g_units)] \
        + [("d", j) for j in range(d_units)]
    n_parts = 2 * MOE_PARTS
    upp = len(units) // n_parts
    pieces = MOE_WPIECES // upp

    def unit_refs(kind, j, tgt, step, slot):
        if kind == "d":
            row0 = pl.multiple_of((step * d_units + j) * MOE_WROWS_D, MOE_WROWS_D)
            return (wd_hbm.at[tgt, pl.ds(row0, MOE_WROWS_D), :], stg_d.at[slot], wd_s, row0,
                    MOE_WROWS_D)
        src = wg_hbm if kind == "g" else wu_hbm
        row0 = pl.multiple_of((step * g_units + j) * MOE_WROWS_A, MOE_WROWS_A)
        return (src.at[tgt, pl.ds(row0, MOE_WROWS_A), :], stg_a.at[slot],
                wg_s if kind == "g" else wu_s, row0, MOE_WROWS_A)

    def unit_copy(t, tgt, step):
        kind, j = units[t]
        src, dst, _, _, _ = unit_refs(kind, j, tgt, step, t % 2)
        return pltpu.make_async_copy(src, dst, sem.at[4 + t % 2])

    def unit_cast(t, step, wslot, piece, n_pieces):
        kind, j = units[t]
        _, stg, dst, row0, rows = unit_refs(kind, j, 0, step, t % 2)
        pr = rows // n_pieces
        dst[wslot, pl.ds(row0 + piece * pr, pr), :] = stg[piece * pr:(piece + 1) * pr, :].astype(BF16)

    tgt_now = jnp.minimum(e + 1, n_e - 1)
    tgt_next = jnp.minimum(e_next + 1, n_e - 1)

    def unit_start_ahead(t):
        t2 = t + 2
        if t2 < len(units):
            unit_copy(t2, tgt_now, i).start(priority=MOE_WPRIO)
        else:
            unit_copy(t2 - len(units), tgt_next, i_next).start(priority=MOE_WPRIO)

    def weight_jobs(p):
        jobs = []
        for c in range(upp):
            t = p * upp + c
            for piece in range(pieces):
                def job(t=t, piece=piece):
                    if piece == 0:
                        unit_copy(t, tgt_now, i).wait()
                    unit_cast(t, i, 1 - cur, piece, pieces)
                jobs.append(job)
            jobs.append(functools.partial(unit_start_ahead, t))
        return jobs

    def compute_part(s, b_local, q, between, wjobs):
        rows = slice(q * part, (q + 1) * part)
        xq = bufs[s][rows, 0:D_MODEL].astype(BF16)
        gc = gcol_ref[b_local, 0, rows, :]
        gate = gc[:, 2:3] + gc[:, 3:4] + gc[:, 4:5]
        g2 = g2_ref[2 * i + b_local]
        nc = MOE_NCHUNK
        hid = []
        slot_no = 0

        def tick():
            nonlocal slot_no
            between[slot_no]()
            if slot_no < len(wjobs):
                wjobs[slot_no]()
            slot_no += 1

        for c in range(EXPERT_FF // nc):
            cols = slice(c * nc, (c + 1) * nc)
            tick()
            hg = jnp.dot(xq, wg_s[cur, :, cols], preferred_element_type=F32)
            tick()
            hu = jnp.dot(xq, wu_s[cur, :, cols], preferred_element_type=F32)
            hid.append((_silu(hg) * hu).astype(BF16))
        hid = jnp.concatenate(hid, axis=1)
        for c in range(D_MODEL // nc):
            cols = slice(c * nc, (c + 1) * nc)
            hc = slice(D_MODEL + c * nc, D_MODEL + (c + 1) * nc)
            tick()
            ye = jnp.dot(hid, wd_s[cur, :, cols], preferred_element_type=F32)
            bufs[s][rows, hc] = bufs[s][rows, hc] + g2[:, cols] * (ye * gate)

    n_between = 2 * (EXPERT_FF // MOE_NCHUNK) + D_MODEL // MOE_NCHUNK

    def spread(jobs):
        n = len(jobs)
        cuts = [n * k // n_between for k in range(n_between + 1)]

        def group(k):
            def run():
                for job in jobs[cuts[k]:cuts[k + 1]]:
                    job()
            return run
        return [group(k) for k in range(n_between)]

    b_a, b_b = 2 * i, 2 * i + 1
    b_n = 2 * i_next

    @pl.when(first)
    def _():
        @pl.loop(0, cap)
        def _(jrow):
            gather_start(e, b_a, 0, jrow)
        for step in range(n_pairs):
            for t in range(len(units)):
                kind, j = units[t]
                src, _, _, _, _ = unit_refs(kind, j, 0, step, 0)
                stg0 = stg_d.at[0] if kind == "d" else stg_a.at[0]
                cp = pltpu.make_async_copy(src, stg0, sem.at[4])
                cp.start()
                cp.wait()
                _, _, dst, row0, rows = unit_refs(kind, j, 0, step, 0)
                dst[0, pl.ds(row0, rows), :] = (stg_d[0] if kind == "d" else stg_a[0]).astype(BF16)
        unit_copy(0, tgt_now, i).start(priority=MOE_WPRIO)
        unit_copy(1, tgt_now, i).start(priority=MOE_WPRIO)

    wait_gather(0)
    fill = {1: range(0, cap // 2), 2: range(cap // 2, cap)}
    for q in range(MOE_PARTS):
        jobs = []
        if q > 0:
            jobs += [functools.partial(scatter_start, e, b_a, 0, jrow)
                     for jrow in range((q - 1) * part, q * part)]
        jobs += [functools.partial(gather_start, e, b_b, 1, jrow) for jrow in fill.get(q, ())]
        if q == 1:
            @pl.when(jnp.logical_not(first))
            def _():
                wait_scatter(1)
        compute_part(0, 0, q, spread(jobs), weight_jobs(q))

    wait_gather(1)
    for q in range(MOE_PARTS):
        if q == 0:
            jobs = [functools.partial(scatter_start, e, b_a, 0, jrow)
                    for jrow in range((MOE_PARTS - 1) * part, cap)]
        else:
            jobs = [functools.partial(scatter_start, e, b_b, 1, jrow)
                    for jrow in range((q - 1) * part, q * part)]
        jobs += [functools.partial(gather_start, e_next, b_n, 0, jrow) for jrow in fill.get(q, ())]
        if q == 1:
            wait_scatter(0)
        compute_part(1, 1, q, spread(jobs), weight_jobs(MOE_PARTS + q))

    for jrow in range((MOE_PARTS - 1) * part, cap):
        scatter_start(e, b_b, 1, jrow)

    @pl.when(last)
    def _():
        wait_scatter(1)
        wait_gather(0)
        unit_copy(0, tgt_next, i_next).wait()
        unit_copy(1, tgt_next, i_next).wait()


def _moe_call(idx_flat, gcol, g2, wg, wu, wd, fh, *, batch, cap):
    m, d = fh.shape[0], fh.shape[1] // 2
    n_tok = m // batch
    assert batch % 2 == 0 and cap % MOE_PARTS == 0
    n_pairs = batch // 2
    assert n_pairs in (1, 2)
    kern = functools.partial(_moe_kernel, cap=cap, n_tok=n_tok, n_pairs=n_pairs)
    grid_spec = pltpu.PrefetchScalarGridSpec(
        num_scalar_prefetch=1,
        grid=(N_EXPERTS, n_pairs),
        in_specs=[pl.BlockSpec((2, 1, cap, LANES), lambda e, i, idx: (i, e, 0, 0)),
                  pl.BlockSpec((batch, 1, d), lambda e, i, idx: (0, 0, 0)),
                  pl.BlockSpec(memory_space=pl.ANY), pl.BlockSpec(memory_space=pl.ANY),
                  pl.BlockSpec(memory_space=pl.ANY), pl.BlockSpec(memory_space=pl.ANY)],
        out_specs=pl.BlockSpec(memory_space=pl.ANY),
        scratch_shapes=[pltpu.VMEM((cap, 2 * d), F32), pltpu.VMEM((cap, 2 * d), F32),
                        pltpu.VMEM((2, d, EXPERT_FF), BF16), pltpu.VMEM((2, d, EXPERT_FF), BF16),
                        pltpu.VMEM((2, EXPERT_FF, d), BF16),
                        pltpu.VMEM((2, MOE_WROWS_A, EXPERT_FF), F32),
                        pltpu.VMEM((2, MOE_WROWS_D, d), F32),
                        pltpu.SemaphoreType.DMA((6,))],
    )
    return pl.pallas_call(
        kern,
        grid_spec=grid_spec,
        out_shape=jax.ShapeDtypeStruct((m, 2 * d), F32),
        input_output_aliases={6: 0},
        compiler_params=_cparams(("arbitrary", "arbitrary")),
        name="moe",
    )(idx_flat, gcol, g2, wg, wu, wd, fh)


def _final_kernel(h_ref, w_ref, o_ref):
    h = h_ref[...]
    ms = jnp.mean(h * h, axis=-1, keepdims=True)
    o_ref[...] = h * lax.rsqrt(ms + EPS) * w_ref[...]


def _final_call(fh, w, *, tm):
    m, d = fh.shape[0], fh.shape[1] // 2
    return pl.pallas_call(
        _final_kernel,
        grid=(m // tm,),
        in_specs=[pl.BlockSpec((tm, d), lambda i: (i, 1)), pl.BlockSpec((1, d), lambda i: (0, 0))],
        out_specs=pl.BlockSpec((tm, d), lambda i: (i, 0)),
        out_shape=jax.ShapeDtypeStruct((m, d), F32),
        compiler_params=_cparams(("arbitrary",)),
        name="final_norm",
    )(fh, w)


def _pick_tile(n, pref):
    t = pref
    while n % t:
        t //= 2
    return t


def _ssd_params(a_log, dt_bias, d_skip, norm_w):
    padw = LANES - N_DT
    alog = jnp.pad(a_log.astype(F32), (0, padw))
    bias = jnp.pad(dt_bias.astype(F32), (0, padw))
    lane = jnp.arange(LANES)[:, None]
    head = jnp.arange(D_SSM)[None, :] // SSM_HEAD_DIM
    return {
        "alog_row": alog[None, :], "alog_col": alog[:, None],
        "bias_row": bias[None, :], "bias_col": bias[:, None],
        "expand_fwd": (lane == head).astype(BF16),
        "expand_rev": (lane == head + SSM_HEADS).astype(BF16),
        "dskip": jnp.repeat(d_skip.astype(F32), SSM_HEAD_DIM)[None, :],
        "norm_w": norm_w.astype(F32)[None, :],
    }


def kernel(x, c, ctx, c_ctx, w_mod, b_mod, norm_mix_w, w_in, ssm_conv_w, ssm_conv_b, ssm_a_log,
           ssm_dt_bias, ssm_d, ssm_norm_w, sc_conv_w, sc_norm_w, w_out, norm_ffn_w, w_router,
           w_gate, w_up, w_down, final_norm_w):
    batch, n_lat, d = x.shape
    n_ctx = ctx.shape[1]
    depth = w_mod.shape[0]
    assert depth == 1 and d == D_MODEL and n_ctx == BLK and n_lat % BLK == 0
    layer = 0
    cap = EC_CAPACITY_FACTOR * n_lat // N_EXPERTS

    pad_rows = -(batch + 1) % 8
    cvec = jnp.concatenate([c, c_ctx[None, :], jnp.zeros((pad_rows, d), F32)], axis=0)
    mod = _mod_call(cvec, w_mod[layer], b_mod[layer][None, :])
    sh1, sc1, g1, sh2, sc2, g2 = [mod[:batch, k * d:(k + 1) * d][:, None, :] for k in range(N_MOD)]
    sh1_c = mod[batch:batch + 1, 0:d][:, None, :]
    sc1_c = mod[batch:batch + 1, d:2 * d][:, None, :]

    w_inT = w_in[layer].T.astype(BF16)
    tc = 512
    w_out_b = w_out[layer].astype(BF16)
    w_rT = w_router[layer].T.astype(BF16)
    wg, wu, wd = w_gate[layer], w_up[layer], w_down[layer]
    norm_mix = norm_mix_w[layer][None, :]
    conv_w, conv_b = ssm_conv_w[layer], ssm_conv_b[layer][None, :]
    prm = _ssd_params(ssm_a_log[layer], ssm_dt_bias[layer], ssm_d[layer], ssm_norm_w[layer])

    ctx2d = ctx.reshape(batch * n_ctx, d)
    xbc_c, dt_c, dtT_c = _inproj_zx_call(
        ctx2d, sh1_c, sc1_c, norm_mix, w_inT, conv_w, conv_b, prm["bias_row"], prm["bias_col"],
        with_z=False, period=n_ctx, tm=_pick_tile(batch * n_ctx, 1024), rows_per_mod=batch * n_ctx)
    zero_state = jnp.zeros((batch, SSM_GROUPS, SSM_STATE, GROUP_W), F32)
    st_f = _ssd_call(xbc_c, dt_c, dtT_c, prm, zero_state, batch=batch, nblk=1, reverse=False,
                     mode="state")
    st_r = _ssd_call(xbc_c, dt_c, dtT_c, prm, zero_state, batch=batch, nblk=1, reverse=True,
                     mode="state")

    x2d = x.reshape(batch * n_lat, d)
    tm = _pick_tile(n_lat, 1024)
    z, xbc, dt, dtT, a_lat = _inproj_zx_call(
        x2d, sh1, sc1, norm_mix, w_inT, conv_w, conv_b, prm["bias_row"], prm["bias_col"],
        with_z=True, period=GRID_W, tm=tm, rows_per_mod=n_lat)
    y_sc = _inproj_sc_call(a_lat, w_inT, sc_conv_w[layer], tm=tm, tc=tc)
    nblk = n_lat // BLK
    y_rev = _ssd_call(xbc, dt, dtT, prm, st_r, batch=batch, nblk=nblk, reverse=True, mode="partial")
    y_ssm = _ssd_call(xbc, dt, dtT, prm, st_f, batch=batch, nblk=nblk, reverse=False, mode="final",
                      extra=(z, y_rev))

    fh, aff = _outproj_call(
        y_ssm, y_sc, sc_norm_w[layer][None, :], w_out_b, x2d, g1, norm_ffn_w[layer][None, :],
        sh2, sc2, w_rT, batch=batch, tm=_pick_tile(n_lat, 512))
    idx, gcol = _route_call(aff, cap=cap)
    fh = _moe_call(idx.reshape(-1), gcol, g2, wg, wu, wd, fh, batch=batch, cap=cap)
    out = _final_call(fh, final_norm_w[None, :], tm=_pick_tile(batch * n_lat, 512))
    return out.reshape(batch, n_lat, d)
```

```python
import functools

import jax
import jax.numpy as jnp
from jax import lax
from jax.experimental import pallas as pl
from jax.experimental.pallas import tpu as pltpu

F32 = jnp.float32
BF16 = jnp.bfloat16
I32 = jnp.int32

D_MODEL = 2048
GRID_W = 64
D_SSM = 2048
SSM_HEAD_DIM = 64
SSM_HEADS = D_SSM // SSM_HEAD_DIM
SSM_GROUPS = 4
SSM_HPG = SSM_HEADS // SSM_GROUPS
SSM_STATE = 128
SSD_CHUNK = 128
GN = SSM_GROUPS * SSM_STATE
D_XBC = D_SSM + 2 * GN
D_ZX = D_SSM + D_XBC
N_DT = 2 * SSM_HEADS
D_SSM_IN = D_ZX + N_DT
D_SC = 2048
N_EXPERTS = 16
EC_CAPACITY_FACTOR = 2
EXPERT_FF = 1024
N_MOD = 6
EPS = 1e-6
LOG2E = 1.4426950408889634

LANES = 128
BLK = 2 * SSD_CHUNK
GROUP_W = SSM_HPG * SSM_HEAD_DIM
VMEM_LIMIT = 56 * 1024 * 1024


def _cparams(sem):
    return pltpu.CompilerParams(dimension_semantics=sem, vmem_limit_bytes=VMEM_LIMIT)


def _silu(v):
    return v * jax.nn.sigmoid(v)


def _softplus(v):
    return jnp.maximum(v, 0.0) + jnp.log1p(jnp.exp(-jnp.abs(v)))


def _mod_kernel(c_ref, w_ref, b_ref, o_ref):
    s = _silu(c_ref[...]).astype(BF16)
    o_ref[...] = jnp.dot(s, w_ref[...].astype(BF16), preferred_element_type=F32) + b_ref[...]


def _mod_call(cvec, w_mod, b_mod):
    rows, d = cvec.shape
    n = w_mod.shape[1]
    tn = 1024
    return pl.pallas_call(
        _mod_kernel,
        grid=(n // tn,),
        in_specs=[pl.BlockSpec((rows, d), lambda j: (0, 0)),
                  pl.BlockSpec((d, tn), lambda j: (0, j)),
                  pl.BlockSpec((1, tn), lambda j: (0, j))],
        out_specs=pl.BlockSpec((rows, tn), lambda j: (0, j)),
        out_shape=jax.ShapeDtypeStruct((rows, n), F32),
        compiler_params=_cparams(("arbitrary",)),
        name="mod",
    )(cvec, w_mod, b_mod)


ZX_SUB = 1024
SC_SUB = 256


def _dot_nt(a, bt):
    return lax.dot_general(a, bt, (((1,), (1,)), ((), ())), preferred_element_type=F32)


def _norm_modulate(h, nw, shift, scale):
    ms = jnp.mean(h * h, axis=-1, keepdims=True)
    hn = h * lax.rsqrt(ms + EPS) * nw
    return hn * (1.0 + scale) + shift


def _conv3(v, cw, period):
    rows = v.shape[0]
    r = lax.broadcasted_iota(I32, (rows, 1), 0) % period
    prev = jnp.where(r == 0, 0.0, pltpu.roll(v, 1, 0))
    nxt = jnp.where(r == period - 1, 0.0, pltpu.roll(v, rows - 1, 0))
    return prev * cw[0:1] + v * cw[1:2] + nxt * cw[2:3]


def _inproj_zx_kernel(h_ref, sh_ref, sc_ref, nw_ref, w_ref, wdt_ref, cw_ref, cb_ref, br_ref, bc_ref,
                      *rest, with_z, period, tm, n_z):
    if with_z:
        z_ref, xbc_ref, dt_ref, dtT_ref, ao_ref, a_ref = rest
    else:
        xbc_ref, dt_ref, dtT_ref, a_ref = rest
    n = pl.program_id(1)

    @pl.when(n == 0)
    def _():
        a = _norm_modulate(h_ref[...], nw_ref[...], sh_ref[0], sc_ref[0]).astype(BF16)
        a_ref[...] = a
        if with_z:
            ao_ref[...] = a
        wdt = wdt_ref[...]
        dt_ref[...] = _softplus(_dot_nt(a, wdt) + br_ref[...])
        dtT_ref[...] = _softplus(_dot_nt(wdt, a) + bc_ref[...])

    wb = w_ref[...]
    sub = min(tm, ZX_SUB)

    if with_z:
        @pl.when(n < n_z)
        def _():
            for k in range(tm // sub):
                rows = slice(k * sub, (k + 1) * sub)
                z_ref[rows, :] = _dot_nt(a_ref[rows, :], wb).astype(BF16)

    @pl.when(n >= n_z)
    def _():
        for k in range(tm // sub):
            rows = slice(k * sub, (k + 1) * sub)
            acc = _dot_nt(a_ref[rows, :], wb)
            y = _conv3(acc, cw_ref[...], period) + cb_ref[...]
            xbc_ref[rows, :] = _silu(y).astype(BF16)


def _inproj_zx_call(h2d, shift, scale, norm_w, w_inT, conv_w, conv_b, bias_row, bias_col, *,
                    with_z, period, tm, rows_per_mod):
    m, d = h2d.shape
    tn = 1024
    n_z = D_SSM // tn if with_z else 0
    n_x = D_XBC // tn
    n_off = 0 if with_z else D_SSM // tn
    per = rows_per_mod // tm
    kern = functools.partial(_inproj_zx_kernel, with_z=with_z, period=period, tm=tm, n_z=n_z)
    out_shape = [jax.ShapeDtypeStruct((m, D_XBC), BF16),
                 jax.ShapeDtypeStruct((m, LANES), F32),
                 jax.ShapeDtypeStruct((LANES, m), F32)]
    out_specs = [pl.BlockSpec((tm, tn), lambda i, n: (i, jnp.maximum(n - n_z, 0))),
                 pl.BlockSpec((tm, LANES), lambda i, n: (i, 0)),
                 pl.BlockSpec((LANES, tm), lambda i, n: (0, i))]
    if with_z:
        out_shape = ([jax.ShapeDtypeStruct((m, D_SSM), BF16)] + out_shape
                     + [jax.ShapeDtypeStruct((m, d), BF16)])
        out_specs = ([pl.BlockSpec((tm, tn), lambda i, n: (i, jnp.minimum(n, n_z - 1)))] + out_specs
                     + [pl.BlockSpec((tm, d), lambda i, n: (i, 0))])
    return pl.pallas_call(
        kern,
        grid=(m // tm, n_z + n_x),
        in_specs=[pl.BlockSpec((tm, d), lambda i, n: (i, 0)),
                  pl.BlockSpec((1, 1, d), lambda i, n: (i // per, 0, 0)),
                  pl.BlockSpec((1, 1, d), lambda i, n: (i // per, 0, 0)),
                  pl.BlockSpec((1, d), lambda i, n: (0, 0)),
                  pl.BlockSpec((tn, d), lambda i, n: (n + n_off, 0)),
                  pl.BlockSpec((LANES, d), lambda i, n: (D_ZX // LANES, 0)),
                  pl.BlockSpec((3, tn), lambda i, n: (0, jnp.maximum(n - n_z, 0))),
                  pl.BlockSpec((1, tn), lambda i, n: (0, jnp.maximum(n - n_z, 0))),
                  pl.BlockSpec((1, LANES), lambda i, n: (0, 0)),
                  pl.BlockSpec((LANES, 1), lambda i, n: (0, 0))],
        out_specs=out_specs,
        out_shape=out_shape,
        scratch_shapes=[pltpu.VMEM((tm, d), BF16)],
        compiler_params=_cparams(("arbitrary", "arbitrary")),
        name="inproj_zx" if with_z else "inproj_ctx",
    )(h2d, shift, scale, norm_w, w_inT, w_inT, conv_w, conv_b, bias_row, bias_col)


def _inproj_sc_kernel(a_ref, wb_ref, wc_ref, wv_ref, cw_ref, o_ref, *, tm):
    wb, wc, wv = wb_ref[...], wc_ref[...], wv_ref[...]
    for k in range(tm // SC_SUB):
        rows = slice(k * SC_SUB, (k + 1) * SC_SUB)
        a = a_ref[rows, :]
        cg = _dot_nt(a, wc)
        hv = _dot_nt(a, wv)
        v = _conv3(cg * hv, cw_ref[...], GRID_W)
        bg = _dot_nt(a, wb)
        o_ref[rows, :] = (bg * v).astype(BF16)


def _inproj_sc_call(a2d, w_inT, conv_w, *, tm, tc):
    m, d = a2d.shape
    kern = functools.partial(_inproj_sc_kernel, tm=tm)

    def wspec(k):
        return pl.BlockSpec((pl.Element(tc), pl.Element(d)),
                            lambda i, n: (pl.multiple_of(D_SSM_IN + k * D_SC + n * tc, 16), 0))

    return pl.pallas_call(
        kern,
        grid=(m // tm, D_SC // tc),
        in_specs=[pl.BlockSpec((tm, d), lambda i, n: (i, 0)),
                  wspec(0), wspec(1), wspec(2),
                  pl.BlockSpec((3, tc), lambda i, n: (0, n))],
        out_specs=pl.BlockSpec((tm, tc), lambda i, n: (i, n)),
        out_shape=jax.ShapeDtypeStruct((m, D_SC), BF16),
        compiler_params=_cparams(("arbitrary", "arbitrary")),
        name="inproj_sc",
    )(a2d, w_inT, w_inT, w_inT, conv_w)


def _ssd_kernel(*refs, reverse, mode, nblk):
    xbc_ref, dt_ref, dtT_ref, alr_ref, alc_ref, e_ref, init_ref = refs[:7]
    rest = refs[7:]
    if mode == "state":
        out_ref, s_ref = rest
    elif mode == "partial":
        out_ref, s_ref = rest
    else:
        z_ref, yo_ref, dsk_ref, nw_ref, out_ref, s_ref = rest
    j = pl.program_id(1)

    @pl.when(j == 0)
    def _():
        s_ref[...] = init_ref[0]

    ii = lax.broadcasted_iota(I32, (SSD_CHUNK, SSD_CHUNK), 0)
    jj = lax.broadcasted_iota(I32, (SSD_CHUNK, SSD_CHUNK), 1)
    tri = jnp.where(jj <= ii, 1.0, 0.0).astype(F32)
    triT = jnp.where(ii <= jj, 1.0, 0.0).astype(F32)
    mask = (jj >= ii) if reverse else (ii >= jj)
    lane4 = lax.broadcasted_iota(I32, (SSD_CHUNK, 4 * SSM_HEAD_DIM), 1) // SSM_HEAD_DIM
    aneg_r = -jnp.exp(alr_ref[...]) * LOG2E
    aneg_c = -jnp.exp(alc_ref[...]) * LOG2E
    lane0 = SSM_HEADS if reverse else 0
    hi = lax.Precision.HIGHEST

    def chunk(c):
        rows = slice(c * SSD_CHUNK, (c + 1) * SSD_CHUNK)
        dt = dt_ref[rows, :]
        dtT = dtT_ref[:, rows]
        la = dt * aneg_r
        laT = dtT * aneg_c
        cs = jnp.dot(tri, la, precision=hi, preferred_element_type=F32)
        csT = jnp.dot(laT, triT, precision=hi, preferred_element_type=F32)
        total = cs[SSD_CHUNK - 1:SSD_CHUNK, :]
        log2_dtT = jnp.log(dtT) * LOG2E
        if reverse:
            ecs = cs - la
            col, rowm = -ecs, csT - laT + log2_dtT
            yscale = jnp.exp2(total - ecs)
            w = dt * jnp.exp2(ecs)
        else:
            col, rowm = cs, log2_dtT - csT
            yscale = jnp.exp2(cs)
            w = dt * jnp.exp2(total - cs)
        t1 = total.astype(BF16).astype(F32)
        r1 = total - t1
        t2 = r1.astype(BF16).astype(F32)
        t3 = r1 - t2
        r16 = lax.broadcasted_iota(I32, (16, LANES), 0)
        tail = jnp.where(r16 == 0, t1, jnp.where(r16 == 1, t2, jnp.where(r16 == 2, t3, 0.0)))
        stack = jnp.concatenate([yscale, w, tail], axis=0).astype(BF16)
        ex = jnp.dot(stack, e_ref[...], preferred_element_type=F32)
        yscale_x = ex[0:SSD_CHUNK]
        w_x = ex[SSD_CHUNK:2 * SSD_CHUNK]
        tb = 2 * SSD_CHUNK
        sdec_x = jnp.exp2(ex[tb:tb + 1] + ex[tb + 1:tb + 2] + ex[tb + 2:tb + 3])

        for g in range(SSM_GROUPS):
            gsl = slice(g * GROUP_W, (g + 1) * GROUP_W)
            xg = xbc_ref[rows, gsl]
            bg = xbc_ref[rows, D_SSM + g * SSM_STATE:D_SSM + (g + 1) * SSM_STATE]
            s_old = s_ref[g]
            if mode != "state":
                cg = xbc_ref[rows, D_SSM + GN + g * SSM_STATE:D_SSM + GN + (g + 1) * SSM_STATE]
                cb = lax.dot_general(cg, bg, (((1,), (1,)), ((), ())), preferred_element_type=F32)
                y_off = jnp.dot(cg, s_old.astype(BF16), preferred_element_type=F32)
                halves = []
                for q in range(2):
                    xq = xg[:, q * 256:(q + 1) * 256]
                    mhs, xms = [], []
                    for r4 in range(4):
                        lane = lane0 + g * SSM_HPG + q * 4 + r4
                        seg = col[:, lane:lane + 1] + rowm[lane:lane + 1, :]
                        dec = jnp.exp2(jnp.where(mask, seg, -jnp.inf))
                        mhs.append((cb * dec).astype(BF16))
                        xms.append(jnp.where(lane4 == r4, xq, jnp.zeros_like(xq)))
                    halves.append(jnp.dot(jnp.concatenate(mhs, axis=1), jnp.concatenate(xms, axis=0),
                                          preferred_element_type=F32))
                y = jnp.concatenate(halves, axis=1) + yscale_x[:, gsl] * y_off
                if mode == "partial":
                    out_ref[rows, gsl] = y.astype(BF16)
                else:
                    y = y + yo_ref[rows, gsl].astype(F32) + dsk_ref[:, gsl] * xg.astype(F32)
                    y = y * _silu(z_ref[rows, gsl].astype(F32))
                    ms = jnp.mean(y * y, axis=-1, keepdims=True)
                    out_ref[rows, gsl] = (y * lax.rsqrt(ms + EPS) * nw_ref[:, gsl]).astype(BF16)
            xw = (xg.astype(F32) * w_x[:, gsl]).astype(BF16)
            upd = lax.dot_general(bg, xw, (((0,), (0,)), ((), ())), preferred_element_type=F32)
            s_ref[g] = sdec_x[:, gsl] * s_old + upd

    for c in ((1, 0) if reverse else (0, 1)):
        chunk(c)

    if mode == "state":
        @pl.when(j == nblk - 1)
        def _():
            out_ref[0] = s_ref[...]


def _ssd_call(xbc, dt, dtT, prm, init, *, batch, nblk, reverse, mode, extra=()):
    m = xbc.shape[0]

    def blk(b, j):
        return b * nblk + (nblk - 1 - j if reverse else j)

    state_spec = pl.BlockSpec((1, SSM_GROUPS, SSM_STATE, GROUP_W), lambda b, j: (b, 0, 0, 0))
    const2 = lambda b, j: (0, 0)
    in_specs = [pl.BlockSpec((BLK, D_XBC), lambda b, j: (blk(b, j), 0)),
                pl.BlockSpec((BLK, LANES), lambda b, j: (blk(b, j), 0)),
                pl.BlockSpec((LANES, BLK), lambda b, j: (0, blk(b, j))),
                pl.BlockSpec((1, LANES), const2), pl.BlockSpec((LANES, 1), const2),
                pl.BlockSpec((LANES, D_SSM), const2),
                state_spec]
    args = [xbc, dt, dtT, prm["alog_row"], prm["alog_col"],
            prm["expand_rev" if reverse else "expand_fwd"], init]
    if mode == "state":
        out_specs = state_spec
        out_shape = jax.ShapeDtypeStruct((batch, SSM_GROUPS, SSM_STATE, GROUP_W), F32)
    else:
        out_specs = pl.BlockSpec((BLK, D_SSM), lambda b, j: (blk(b, j), 0))
        out_shape = jax.ShapeDtypeStruct((m, D_SSM), BF16)
    if mode == "final":
        z, y_other = extra
        in_specs += [pl.BlockSpec((BLK, D_SSM), lambda b, j: (blk(b, j), 0)),
                     pl.BlockSpec((BLK, D_SSM), lambda b, j: (blk(b, j), 0)),
                     pl.BlockSpec((1, D_SSM), const2), pl.BlockSpec((1, D_SSM), const2)]
        args += [z, y_other, prm["dskip"], prm["norm_w"]]
    kern = functools.partial(_ssd_kernel, reverse=reverse, mode=mode, nblk=nblk)
    return pl.pallas_call(
        kern,
        grid=(batch, nblk),
        in_specs=in_specs,
        out_specs=out_specs,
        out_shape=out_shape,
        scratch_shapes=[pltpu.VMEM((SSM_GROUPS, SSM_STATE, GROUP_W), F32)],
        compiler_params=_cparams(("arbitrary", "arbitrary")),
        name=f"ssd_{mode}_{'rev' if reverse else 'fwd'}",
    )(*args)


OUT_NCHUNK = 1024


def _outproj_kernel(ys_ref, yc_ref, scw_ref, w_ref, x_ref, g1_ref, nw_ref, sh_ref, sc_ref, wr_ref,
                    fh_ref, aff_ref, *, tm, sub):
    nc = OUT_NCHUNK
    chunks = [slice(c * nc, (c + 1) * nc) for c in range(D_MODEL // nc)]
    g1, nw, sh, sc = g1_ref[0], nw_ref[...], sh_ref[0], sc_ref[0]

    def rows_of(k):
        return slice(k * sub, (k + 1) * sub)

    def prologue(k):
        yc = yc_ref[rows_of(k), :].astype(F32)
        ms = jnp.mean(yc * yc, axis=-1, keepdims=True)
        return ys_ref[rows_of(k), :], (yc * lax.rsqrt(ms + EPS) * scw_ref[...]).astype(BF16)

    def pass1(k, ys, ycn, cols):
        mix = jnp.dot(ys, w_ref[0:D_SSM, cols], preferred_element_type=F32)
        mix = mix + jnp.dot(ycn, w_ref[D_SSM:, cols], preferred_element_type=F32)
        h = x_ref[rows_of(k), cols] + g1[:, cols] * mix
        fh_ref[rows_of(k), slice(D_MODEL + cols.start, D_MODEL + cols.stop)] = h
        return jnp.sum(h * h, axis=-1, keepdims=True)

    def pass2(k, rstd, cols):
        h = fh_ref[rows_of(k), slice(D_MODEL + cols.start, D_MODEL + cols.stop)]
        f = h * rstd * nw[:, cols] * (1.0 + sc[:, cols]) + sh[:, cols]
        fh_ref[rows_of(k), cols] = f
        return _dot_nt(wr_ref[:, cols], f.astype(BF16))

    def finish(k, logits):
        mx = jnp.max(logits, axis=0, keepdims=True)
        ex = jnp.exp(logits - mx)
        aff_ref[0, :, rows_of(k)] = ex / jnp.sum(ex, axis=0, keepdims=True)

    prev = None
    for k in range(tm // sub):
        ys, ycn = prologue(k)
        ssq, logits = None, None
        for cols in chunks:
            part = pass1(k, ys, ycn, cols)
            ssq = part if ssq is None else ssq + part
            if prev is not None:
                lg = pass2(prev[0], prev[1], cols)
                logits = lg if logits is None else logits + lg
        if prev is not None:
            finish(prev[0], logits)
        prev = (k, lax.rsqrt(ssq * (1.0 / D_MODEL) + EPS))
    logits = None
    for cols in chunks:
        lg = pass2(prev[0], prev[1], cols)
        logits = lg if logits is None else logits + lg
    finish(prev[0], logits)


def _outproj_call(y_ssm, y_sc, sc_norm_w, w_out, x2d, g1, norm_w, shift, scale, w_rT, *, batch, tm):
    m, d = x2d.shape
    n_tok = m // batch
    per = n_tok // tm
    vec = lambda i: (i // per, 0, 0)
    const2 = lambda i: (0, 0)
    sub = min(tm, 256)
    return pl.pallas_call(
        functools.partial(_outproj_kernel, tm=tm, sub=sub),
        grid=(m // tm,),
        in_specs=[pl.BlockSpec((tm, D_SSM), lambda i: (i, 0)),
                  pl.BlockSpec((tm, D_SC), lambda i: (i, 0)),
                  pl.BlockSpec((1, D_SC), const2),
                  pl.BlockSpec((D_SSM + D_SC, d), const2, pipeline_mode=pl.Buffered(1)),
                  pl.BlockSpec((tm, d), lambda i: (i, 0)),
                  pl.BlockSpec((1, 1, d), vec),
                  pl.BlockSpec((1, d), const2),
                  pl.BlockSpec((1, 1, d), vec),
                  pl.BlockSpec((1, 1, d), vec),
                  pl.BlockSpec((N_EXPERTS, d), const2)],
        out_specs=[pl.BlockSpec((tm, 2 * d), lambda i: (i, 0)),
                   pl.BlockSpec((1, N_EXPERTS, tm), lambda i: (i // per, 0, i % per))],
        out_shape=[jax.ShapeDtypeStruct((m, 2 * d), F32),
                   jax.ShapeDtypeStruct((batch, N_EXPERTS, n_tok), F32)],
        compiler_params=_cparams(("arbitrary",)),
        name="outproj",
    )(y_ssm, y_sc, sc_norm_w, w_out, x2d, g1, norm_w, shift, scale, w_rT)


def _prefix_lanes(m01):
    rows, n = m01.shape
    ki = lax.broadcasted_iota(I32, (LANES, LANES), 0)
    ci = lax.broadcasted_iota(I32, (LANES, LANES), 1)
    triu = jnp.where(ki <= ci, 1.0, 0.0).astype(BF16)
    off = jnp.zeros((rows, 1), F32)
    parts = []
    for k in range(n // LANES):
        p = jnp.dot(m01[:, k * LANES:(k + 1) * LANES].astype(BF16), triu,
                    preferred_element_type=F32)
        parts.append(p + off)
        off = off + p[:, LANES - 1:LANES]
    return jnp.concatenate(parts, axis=1)


def _route_kernel(aff_ref, idx_ref, gcol_ref, *, cap):
    a = aff_ref[0]
    n_e, n = a.shape
    thr = jnp.zeros((n_e, 1), I32)
    for bit in range(30, -1, -1):
        cand = thr | (1 << bit)
        cand_f = lax.bitcast_convert_type(cand, F32)
        cnt = jnp.sum(jnp.where(a >= cand_f, 1.0, 0.0), axis=1, keepdims=True)
        thr = jnp.where(cnt >= cap, cand, thr)
    thr_f = lax.bitcast_convert_type(thr, F32)
    gt = a > thr_f
    eq = a == thr_f
    need = cap - jnp.sum(jnp.where(gt, 1.0, 0.0), axis=1, keepdims=True)
    eqf = jnp.where(eq, 1.0, 0.0)
    ties_before = _prefix_lanes(eqf) - eqf
    sel = jnp.logical_or(gt, jnp.logical_and(eq, ties_before < need))
    self_ = jnp.where(sel, 1.0, 0.0)
    slot = jnp.where(sel, _prefix_lanes(self_) - 1.0, -1.0).astype(I32)

    tok = lax.broadcasted_iota(I32, (1, n), 1)
    t_hi = (tok // 64).astype(F32)
    t_lo = (tok % 64).astype(F32)
    a1 = a.astype(BF16).astype(F32)
    ra = a - a1
    a2 = ra.astype(BF16).astype(F32)
    a3 = ra - a2
    srow = lax.broadcasted_iota(I32, (cap, n), 0)
    r8 = lax.broadcasted_iota(I32, (8, n), 0)
    zpad = jnp.zeros((LANES - 8, n), F32)
    for e in range(n_e):
        onehot = jnp.where(slot[e:e + 1, :] == srow, 1.0, 0.0).astype(BF16)
        v8 = jnp.where(r8 == 0, t_hi, jnp.where(r8 == 1, t_lo, jnp.where(
            r8 == 2, a1[e:e + 1], jnp.where(r8 == 3, a2[e:e + 1], jnp.where(
                r8 == 4, a3[e:e + 1], 0.0)))))
        vals = jnp.concatenate([v8, zpad], axis=0).astype(BF16)
        o_col = lax.dot_general(onehot, vals, (((1,), (1,)), ((), ())),
                                preferred_element_type=F32)
        o_row = o_col.T
        idx_ref[0, e:e + 1, :] = (o_row[0:1] * 64.0 + o_row[1:2]).astype(I32)
        gcol_ref[0, e] = o_col


def _route_call(aff, *, cap):
    batch, n_e, n = aff.shape
    return pl.pallas_call(
        functools.partial(_route_kernel, cap=cap),
        grid=(batch,),
        in_specs=[pl.BlockSpec((1, n_e, n), lambda b: (b, 0, 0))],
        out_specs=[pl.BlockSpec((1, n_e, cap), lambda b: (b, 0, 0)),
                   pl.BlockSpec((1, n_e, cap, LANES), lambda b: (b, 0, 0, 0))],
        out_shape=[jax.ShapeDtypeStruct((batch, n_e, cap), I32),
                   jax.ShapeDtypeStruct((batch, n_e, cap, LANES), F32)],
        compiler_params=_cparams(("arbitrary",)),
        name="route",
    )(aff)


MOE_PARTS = 4
MOE_NCHUNK = 512
MOE_WROWS_A = 512
MOE_WROWS_D = 128
MOE_WPIECES = 4
MOE_WPRIO = 1


def _moe_kernel(idx_ref, gcol_ref, g2_ref, wg_hbm, wu_hbm, wd_hbm, fh_in, fh_out,
                buf0, buf1, wg_s, wu_s, wd_s, stg_a, stg_d, sem, *, cap, n_tok, n_pairs):
    e = pl.program_id(0)
    i = pl.program_id(1)
    n_e = pl.num_programs(0)
    first = jnp.logical_and(e == 0, i == 0)
    last = jnp.logical_and(e == n_e - 1, i == n_pairs - 1)
    i_next = jnp.where(i == n_pairs - 1, 0, i + 1)
    e_next = jnp.where(i == n_pairs - 1, e + 1, e)
    bufs = (buf0, buf1)
    part = cap // MOE_PARTS
    hcols = pl.ds(D_MODEL, D_MODEL)
    cur = e % 2

    def row_of(ex, b, jrow):
        return b * n_tok + idx_ref[(b * N_EXPERTS + ex) * cap + jrow]

    def gather_start(ex, b, s, jrow):
        pltpu.make_async_copy(fh_in.at[pl.ds(row_of(ex, b, jrow), 1)], bufs[s].at[pl.ds(jrow, 1)],
                              sem.at[s]).start()

    def scatter_start(ex, b, s, jrow):
        pltpu.make_async_copy(bufs[s].at[pl.ds(jrow, 1), hcols],
                              fh_out.at[pl.ds(row_of(ex, b, jrow), 1), hcols],
                              sem.at[2 + s]).start()

    def wait_gather(s):
        pltpu.make_async_copy(fh_in.at[pl.ds(0, cap)], bufs[s], sem.at[s]).wait()

    def wait_scatter(s):
        pltpu.make_async_copy(bufs[s].at[:, hcols], fh_out.at[pl.ds(0, cap), hcols],
                              sem.at[2 + s]).wait()

    g_units = (D_MODEL // MOE_WROWS_A) // n_pairs
    d_units = (EXPERT_FF // MOE_WROWS_D) // n_pairs
    units = [("g", j) for j in range(g_units)] + [("u", j) for j in range(g_units)] \
        + [("d", j) for j in range(d_units)]
    n_parts = 2 * MOE_PARTS
    upp = len(units) // n_parts
    pieces = MOE_WPIECES // upp

    def unit_refs(kind, j, tgt, step, slot):
        if kind == "d":
            row0 = pl.multiple_of((step * d_units + j) * MOE_WROWS_D, MOE_WROWS_D)
            return (wd_hbm.at[tgt, pl.ds(row0, MOE_WROWS_D), :], stg_d.at[slot], wd_s, row0,
                    MOE_WROWS_D)
        src = wg_hbm if kind == "g" else wu_hbm
        row0 = pl.multiple_of((step * g_units + j) * MOE_WROWS_A, MOE_WROWS_A)
        return (src.at[tgt, pl.ds(row0, MOE_WROWS_A), :], stg_a.at[slot],
                wg_s if kind == "g" else wu_s, row0, MOE_WROWS_A)

    def unit_copy(t, tgt, step):
        kind, j = units[t]
        src, dst, _, _, _ = unit_refs(kind, j, tgt, step, t % 2)
        return pltpu.make_async_copy(src, dst, sem.at[4 + t % 2])

    def unit_cast(t, step, wslot, piece, n_pieces):
        kind, j = units[t]
        _, stg, dst, row0, rows = unit_refs(kind, j, 0, step, t % 2)
        pr = rows // n_pieces
        dst[wslot, pl.ds(row0 + piece * pr, pr), :] = stg[piece * pr:(piece + 1) * pr, :].astype(BF16)

    tgt_now = jnp.minimum(e + 1, n_e - 1)
    tgt_next = jnp.minimum(e_next + 1, n_e - 1)

    def unit_start_ahead(t):
        t2 = t + 2
        if t2 < len(units):
            unit_copy(t2, tgt_now, i).start(priority=MOE_WPRIO)
        else:
            unit_copy(t2 - len(units), tgt_next, i_next).start(priority=MOE_WPRIO)

    def weight_jobs(p):
        jobs = []
        for c in range(upp):
            t = p * upp + c
            for piece in range(pieces):
                def job(t=t, piece=piece):
                    if piece == 0:
                        unit_copy(t, tgt_now, i).wait()
                    unit_cast(t, i, 1 - cur, piece, pieces)
                jobs.append(job)
            jobs.append(functools.partial(unit_start_ahead, t))
        return jobs

    def compute_part(s, b_local, q, between, wjobs):
        rows = slice(q * part, (q + 1) * part)
        xq = bufs[s][rows, 0:D_MODEL].astype(BF16)
        gc = gcol_ref[b_local, 0, rows, :]
        gate = gc[:, 2:3] + gc[:, 3:4] + gc[:, 4:5]
        g2 = g2_ref[2 * i + b_local]
        nc = MOE_NCHUNK
        hid = []
        slot_no = 0

        def tick():
            nonlocal slot_no
            between[slot_no]()
            if slot_no < len(wjobs):
                wjobs[slot_no]()
            slot_no += 1

        for c in range(EXPERT_FF // nc):
            cols = slice(c * nc, (c + 1) * nc)
            tick()
            hg = jnp.dot(xq, wg_s[cur, :, cols], preferred_element_type=F32)
            tick()
            hu = jnp.dot(xq, wu_s[cur, :, cols], preferred_element_type=F32)
            hid.append((_silu(hg) * hu).astype(BF16))
        hid = jnp.concatenate(hid, axis=1)
        for c in range(D_MODEL // nc):
            cols = slice(c * nc, (c + 1) * nc)
            hc = slice(D_MODEL + c * nc, D_MODEL + (c + 1) * nc)
            tick()
            ye = jnp.dot(hid, wd_s[cur, :, cols], preferred_element_type=F32)
            bufs[s][rows, hc] = bufs[s][rows, hc] + g2[:, cols] * (ye * gate)

    n_between = 2 * (EXPERT_FF // MOE_NCHUNK) + D_MODEL // MOE_NCHUNK

    def spread(jobs):
        n = len(jobs)
        cuts = [n * k // n_between for k in range(n_between + 1)]

        def group(k):
            def run():
                for job in jobs[cuts[k]:cuts[k + 1]]:
                    job()
            return run
        return [group(k) for k in range(n_between)]

    b_a, b_b = 2 * i, 2 * i + 1
    b_n = 2 * i_next

    @pl.when(first)
    def _():
        @pl.loop(0, cap)
        def _(jrow):
            gather_start(e, b_a, 0, jrow)
        for step in range(n_pairs):
            for t in range(len(units)):
                kind, j = units[t]
                src, _, _, _, _ = unit_refs(kind, j, 0, step, 0)
                stg0 = stg_d.at[0] if kind == "d" else stg_a.at[0]
                cp = pltpu.make_async_copy(src, stg0, sem.at[4])
                cp.start()
                cp.wait()
                _, _, dst, row0, rows = unit_refs(kind, j, 0, step, 0)
                dst[0, pl.ds(row0, rows), :] = (stg_d[0] if kind == "d" else stg_a[0]).astype(BF16)
        unit_copy(0, tgt_now, i).start(priority=MOE_WPRIO)
        unit_copy(1, tgt_now, i).start(priority=MOE_WPRIO)

    wait_gather(0)
    fill = {1: range(0, cap // 2), 2: range(cap // 2, cap)}
    for q in range(MOE_PARTS):
        jobs = []
        if q > 0:
            jobs += [functools.partial(scatter_start, e, b_a, 0, jrow)
                     for jrow in range((q - 1) * part, q * part)]
        jobs += [functools.partial(gather_start, e, b_b, 1, jrow) for jrow in fill.get(q, ())]
        if q == 1:
            @pl.when(jnp.logical_not(first))
            def _():
                wait_scatter(1)
        compute_part(0, 0, q, spread(jobs), weight_jobs(q))

    wait_gather(1)
    for q in range(MOE_PARTS):
        if q == 0:
            jobs = [functools.partial(scatter_start, e, b_a, 0, jrow)
                    for jrow in range((MOE_PARTS - 1) * part, cap)]
        else:
            jobs = [functools.partial(scatter_start, e, b_b, 1, jrow)
                    for jrow in range((q - 1) * part, q * part)]
        jobs += [functools.partial(gather_start, e_next, b_n, 0, jrow) for jrow in fill.get(q, ())]
        if q == 1:
            wait_scatter(0)
        compute_part(1, 1, q, spread(jobs), weight_jobs(MOE_PARTS + q))

    for jrow in range((MOE_PARTS - 1) * part, cap):
        scatter_start(e, b_b, 1, jrow)

    @pl.when(last)
    def _():
        wait_scatter(1)
        wait_gather(0)
        unit_copy(0, tgt_next, i_next).wait()
        unit_copy(1, tgt_next, i_next).wait()


def _moe_call(idx_flat, gcol, g2, wg, wu, wd, fh, *, batch, cap):
    m, d = fh.shape[0], fh.shape[1] // 2
    n_tok = m // batch
    assert batch % 2 == 0 and cap % MOE_PARTS == 0
    n_pairs = batch // 2
    assert n_pairs in (1, 2)
    kern = functools.partial(_moe_kernel, cap=cap, n_tok=n_tok, n_pairs=n_pairs)
    grid_spec = pltpu.PrefetchScalarGridSpec(
        num_scalar_prefetch=1,
        grid=(N_EXPERTS, n_pairs),
        in_specs=[pl.BlockSpec((2, 1, cap, LANES), lambda e, i, idx: (i, e, 0, 0)),
                  pl.BlockSpec((batch, 1, d), lambda e, i, idx: (0, 0, 0)),
                  pl.BlockSpec(memory_space=pl.ANY), pl.BlockSpec(memory_space=pl.ANY),
                  pl.BlockSpec(memory_space=pl.ANY), pl.BlockSpec(memory_space=pl.ANY)],
        out_specs=pl.BlockSpec(memory_space=pl.ANY),
        scratch_shapes=[pltpu.VMEM((cap, 2 * d), F32), pltpu.VMEM((cap, 2 * d), F32),
                        pltpu.VMEM((2, d, EXPERT_FF), BF16), pltpu.VMEM((2, d, EXPERT_FF), BF16),
                        pltpu.VMEM((2, EXPERT_FF, d), BF16),
                        pltpu.VMEM((2, MOE_WROWS_A, EXPERT_FF), F32),
                        pltpu.VMEM((2, MOE_WROWS_D, d), F32),
                        pltpu.SemaphoreType.DMA((6,))],
    )
    return pl.pallas_call(
        kern,
        grid_spec=grid_spec,
        out_shape=jax.ShapeDtypeStruct((m, 2 * d), F32),
        input_output_aliases={6: 0},
        compiler_params=_cparams(("arbitrary", "arbitrary")),
        name="moe",
    )(idx_flat, gcol, g2, wg, wu, wd, fh)


def _final_kernel(h_ref, w_ref, o_ref):
    h = h_ref[...]
    ms = jnp.mean(h * h, axis=-1, keepdims=True)
    o_ref[...] = h * lax.rsqrt(ms + EPS) * w_ref[...]


def _final_call(fh, w, *, tm):
    m, d = fh.shape[0], fh.shape[1] // 2
    return pl.pallas_call(
        _final_kernel,
        grid=(m // tm,),
        in_specs=[pl.BlockSpec((tm, d), lambda i: (i, 1)), pl.BlockSpec((1, d), lambda i: (0, 0))],
        out_specs=pl.BlockSpec((tm, d), lambda i: (i, 0)),
        out_shape=jax.ShapeDtypeStruct((m, d), F32),
        compiler_params=_cparams(("arbitrary",)),
        name="final_norm",
    )(fh, w)


def _pick_tile(n, pref):
    t = pref
    while n % t:
        t //= 2
    return t


def _ssd_params(a_log, dt_bias, d_skip, norm_w):
    padw = LANES - N_DT
    alog = jnp.pad(a_log.astype(F32), (0, padw))
    bias = jnp.pad(dt_bias.astype(F32), (0, padw))
    lane = jnp.arange(LANES)[:, None]
    head = jnp.arange(D_SSM)[None, :] // SSM_HEAD_DIM
    return {
        "alog_row": alog[None, :], "alog_col": alog[:, None],
        "bias_row": bias[None, :], "bias_col": bias[:, None],
        "expand_fwd": (lane == head).astype(BF16),
        "expand_rev": (lane == head + SSM_HEADS).astype(BF16),
        "dskip": jnp.repeat(d_skip.astype(F32), SSM_HEAD_DIM)[None, :],
        "norm_w": norm_w.astype(F32)[None, :],
    }


def kernel(x, c, ctx, c_ctx, w_mod, b_mod, norm_mix_w, w_in, ssm_conv_w, ssm_conv_b, ssm_a_log,
           ssm_dt_bias, ssm_d, ssm_norm_w, sc_conv_w, sc_norm_w, w_out, norm_ffn_w, w_router,
           w_gate, w_up, w_down, final_norm_w):
    batch, n_lat, d = x.shape
    n_ctx = ctx.shape[1]
    depth = w_mod.shape[0]
    assert depth == 1 and d == D_MODEL and n_ctx == BLK and n_lat % BLK == 0
    layer = 0
    cap = EC_CAPACITY_FACTOR * n_lat // N_EXPERTS

    pad_rows = -(batch + 1) % 8
    cvec = jnp.concatenate([c, c_ctx[None, :], jnp.zeros((pad_rows, d), F32)], axis=0)
    mod = _mod_call(cvec, w_mod[layer], b_mod[layer][None, :])
    sh1, sc1, g1, sh2, sc2, g2 = [mod[:batch, k * d:(k + 1) * d][:, None, :] for k in range(N_MOD)]
    sh1_c = mod[batch:batch + 1, 0:d][:, None, :]
    sc1_c = mod[batch:batch + 1, d:2 * d][:, None, :]

    w_inT = w_in[layer].T.astype(BF16)
    tc = 1024
    w_out_b = w_out[layer].astype(BF16)
    w_rT = w_router[layer].T.astype(BF16)
    wg, wu, wd = w_gate[layer], w_up[layer], w_down[layer]
    norm_mix = norm_mix_w[layer][None, :]
    conv_w, conv_b = ssm_conv_w[layer], ssm_conv_b[layer][None, :]
    prm = _ssd_params(ssm_a_log[layer], ssm_dt_bias[layer], ssm_d[layer], ssm_norm_w[layer])

    ctx2d = ctx.reshape(batch * n_ctx, d)
    xbc_c, dt_c, dtT_c = _inproj_zx_call(
        ctx2d, sh1_c, sc1_c, norm_mix, w_inT, conv_w, conv_b, prm["bias_row"], prm["bias_col"],
        with_z=False, period=n_ctx, tm=_pick_tile(batch * n_ctx, 1024), rows_per_mod=batch * n_ctx)
    zero_state = jnp.zeros((batch, SSM_GROUPS, SSM_STATE, GROUP_W), F32)
    st_f = _ssd_call(xbc_c, dt_c, dtT_c, prm, zero_state, batch=batch, nblk=1, reverse=False,
                     mode="state")
    st_r = _ssd_call(xbc_c, dt_c, dtT_c, prm, zero_state, batch=batch, nblk=1, reverse=True,
                     mode="state")

    x2d = x.reshape(batch * n_lat, d)
    tm = _pick_tile(n_lat, 1024)
    z, xbc, dt, dtT, a_lat = _inproj_zx_call(
        x2d, sh1, sc1, norm_mix, w_inT, conv_w, conv_b, prm["bias_row"], prm["bias_col"],
        with_z=True, period=GRID_W, tm=tm, rows_per_mod=n_lat)
    y_sc = _inproj_sc_call(a_lat, w_inT, sc_conv_w[layer], tm=tm, tc=tc)
    nblk = n_lat // BLK
    y_rev = _ssd_call(xbc, dt, dtT, prm, st_r, batch=batch, nblk=nblk, reverse=True, mode="partial")
    y_ssm = _ssd_call(xbc, dt, dtT, prm, st_f, batch=batch, nblk=nblk, reverse=False, mode="final",
                      extra=(z, y_rev))

    fh, aff = _outproj_call(
        y_ssm, y_sc, sc_norm_w[layer][None, :], w_out_b, x2d, g1, norm_ffn_w[layer][None, :],
        sh2, sc2, w_rT, batch=batch, tm=_pick_tile(n_lat, 512))
    idx, gcol = _route_call(aff, cap=cap)
    fh = _moe_call(idx.reshape(-1), gcol, g2, wg, wu, wd, fh, batch=batch, cap=cap)
    out = _final_call(fh, final_norm_w[None, :], tm=_pick_tile(batch * n_lat, 512))
    return out.reshape(batch, n_lat, d)
```

```python
import functools

import jax
import jax.numpy as jnp
from jax import lax
from jax.experimental import pallas as pl
from jax.experimental.pallas import tpu as pltpu

F32 = jnp.float32
BF16 = jnp.bfloat16
I32 = jnp.int32

D_MODEL = 2048
GRID_W = 64
D_SSM = 2048
SSM_HEAD_DIM = 64
SSM_HEADS = D_SSM // SSM_HEAD_DIM
SSM_GROUPS = 4
SSM_HPG = SSM_HEADS // SSM_GROUPS
SSM_STATE = 128
SSD_CHUNK = 128
GN = SSM_GROUPS * SSM_STATE
D_XBC = D_SSM + 2 * GN
D_ZX = D_SSM + D_XBC
N_DT = 2 * SSM_HEADS
D_SSM_IN = D_ZX + N_DT
D_SC = 2048
N_EXPERTS = 16
EC_CAPACITY_FACTOR = 2
EXPERT_FF = 1024
N_MOD = 6
EPS = 1e-6
LOG2E = 1.4426950408889634

LANES = 128
BLK = 2 * SSD_CHUNK
GROUP_W = SSM_HPG * SSM_HEAD_DIM
VMEM_LIMIT = 56 * 1024 * 1024


def _cparams(sem):
    return pltpu.CompilerParams(dimension_semantics=sem, vmem_limit_bytes=VMEM_LIMIT)


def _silu(v):
    return v * jax.nn.sigmoid(v)


def _softplus(v):
    return jnp.maximum(v, 0.0) + jnp.log1p(jnp.exp(-jnp.abs(v)))


def _mod_kernel(c_ref, w_ref, b_ref, o_ref):
    s = _silu(c_ref[...]).astype(BF16)
    o_ref[...] = jnp.dot(s, w_ref[...].astype(BF16), preferred_element_type=F32) + b_ref[...]


def _mod_call(cvec, w_mod, b_mod):
    rows, d = cvec.shape
    n = w_mod.shape[1]
    tn = 1024
    return pl.pallas_call(
        _mod_kernel,
        grid=(n // tn,),
        in_specs=[pl.BlockSpec((rows, d), lambda j: (0, 0)),
                  pl.BlockSpec((d, tn), lambda j: (0, j)),
                  pl.BlockSpec((1, tn), lambda j: (0, j))],
        out_specs=pl.BlockSpec((rows, tn), lambda j: (0, j)),
        out_shape=jax.ShapeDtypeStruct((rows, n), F32),
        compiler_params=_cparams(("arbitrary",)),
        name="mod",
    )(cvec, w_mod, b_mod)


ZX_SUB = 1024
SC_SUB = 256


def _dot_nt(a, bt):
    return lax.dot_general(a, bt, (((1,), (1,)), ((), ())), preferred_element_type=F32)


def _norm_modulate(h, nw, shift, scale):
    ms = jnp.mean(h * h, axis=-1, keepdims=True)
    hn = h * lax.rsqrt(ms + EPS) * nw
    return hn * (1.0 + scale) + shift


def _conv3(v, cw, period):
    rows = v.shape[0]
    r = lax.broadcasted_iota(I32, (rows, 1), 0) % period
    prev = jnp.where(r == 0, 0.0, pltpu.roll(v, 1, 0))
    nxt = jnp.where(r == period - 1, 0.0, pltpu.roll(v, rows - 1, 0))
    return prev * cw[0:1] + v * cw[1:2] + nxt * cw[2:3]


def _inproj_zx_kernel(h_ref, sh_ref, sc_ref, nw_ref, w_ref, wdt_ref, cw_ref, cb_ref, br_ref, bc_ref,
                      *rest, with_z, period, tm, n_z):
    if with_z:
        z_ref, xbc_ref, dt_ref, dtT_ref, ao_ref, a_ref = rest
    else:
        xbc_ref, dt_ref, dtT_ref, a_ref = rest
    n = pl.program_id(1)

    @pl.when(n == 0)
    def _():
        a = _norm_modulate(h_ref[...], nw_ref[...], sh_ref[0], sc_ref[0]).astype(BF16)
        a_ref[...] = a
        if with_z:
            ao_ref[...] = a
        wdt = wdt_ref[...]
        dt_ref[...] = _softplus(_dot_nt(a, wdt) + br_ref[...])
        dtT_ref[...] = _softplus(_dot_nt(wdt, a) + bc_ref[...])

    wb = w_ref[...]
    sub = min(tm, ZX_SUB)

    if with_z:
        @pl.when(n < n_z)
        def _():
            for k in range(tm // sub):
                rows = slice(k * sub, (k + 1) * sub)
                z_ref[rows, :] = _dot_nt(a_ref[rows, :], wb).astype(BF16)

    @pl.when(n >= n_z)
    def _():
        for k in range(tm // sub):
            rows = slice(k * sub, (k + 1) * sub)
            acc = _dot_nt(a_ref[rows, :], wb)
            y = _conv3(acc, cw_ref[...], period) + cb_ref[...]
            xbc_ref[rows, :] = _silu(y).astype(BF16)


def _inproj_zx_call(h2d, shift, scale, norm_w, w_inT, conv_w, conv_b, bias_row, bias_col, *,
                    with_z, period, tm, rows_per_mod):
    m, d = h2d.shape
    tn = 1024
    n_z = D_SSM // tn if with_z else 0
    n_x = D_XBC // tn
    n_off = 0 if with_z else D_SSM // tn
    per = rows_per_mod // tm
    kern = functools.partial(_inproj_zx_kernel, with_z=with_z, period=period, tm=tm, n_z=n_z)
    out_shape = [jax.ShapeDtypeStruct((m, D_XBC), BF16),
                 jax.ShapeDtypeStruct((m, LANES), F32),
                 jax.ShapeDtypeStruct((LANES, m), F32)]
    out_specs = [pl.BlockSpec((tm, tn), lambda i, n: (i, jnp.maximum(n - n_z, 0))),
                 pl.BlockSpec((tm, LANES), lambda i, n: (i, 0)),
                 pl.BlockSpec((LANES, tm), lambda i, n: (0, i))]
    if with_z:
        out_shape = ([jax.ShapeDtypeStruct((m, D_SSM), BF16)] + out_shape
                     + [jax.ShapeDtypeStruct((m, d), BF16)])
        out_specs = ([pl.BlockSpec((tm, tn), lambda i, n: (i, jnp.minimum(n, n_z - 1)))] + out_specs
                     + [pl.BlockSpec((tm, d), lambda i, n: (i, 0))])
    return pl.pallas_call(
        kern,
        grid=(m // tm, n_z + n_x),
        in_specs=[pl.BlockSpec((tm, d), lambda i, n: (i, 0)),
                  pl.BlockSpec((1, 1, d), lambda i, n: (i // per, 0, 0)),
                  pl.BlockSpec((1, 1, d), lambda i, n: (i // per, 0, 0)),
                  pl.BlockSpec((1, d), lambda i, n: (0, 0)),
                  pl.BlockSpec((tn, d), lambda i, n: (n + n_off, 0)),
                  pl.BlockSpec((LANES, d), lambda i, n: (D_ZX // LANES, 0)),
                  pl.BlockSpec((3, tn), lambda i, n: (0, jnp.maximum(n - n_z, 0))),
                  pl.BlockSpec((1, tn), lambda i, n: (0, jnp.maximum(n - n_z, 0))),
                  pl.BlockSpec((1, LANES), lambda i, n: (0, 0)),
                  pl.BlockSpec((LANES, 1), lambda i, n: (0, 0))],
        out_specs=out_specs,
        out_shape=out_shape,
        scratch_shapes=[pltpu.VMEM((tm, d), BF16)],
        compiler_params=_cparams(("arbitrary", "arbitrary")),
        name="inproj_zx" if with_z else "inproj_ctx",
    )(h2d, shift, scale, norm_w, w_inT, w_inT, conv_w, conv_b, bias_row, bias_col)


def _inproj_sc_kernel(a_ref, wb_ref, wc_ref, wv_ref, cw_ref, o_ref, *, tm):
    wb, wc, wv = wb_ref[...], wc_ref[...], wv_ref[...]
    for k in range(tm // SC_SUB):
        rows = slice(k * SC_SUB, (k + 1) * SC_SUB)
        a = a_ref[rows, :]
        cg = _dot_nt(a, wc)
        hv = _dot_nt(a, wv)
        v = _conv3(cg * hv, cw_ref[...], GRID_W)
        bg = _dot_nt(a, wb)
        o_ref[rows, :] = (bg * v).astype(BF16)


def _inproj_sc_call(a2d, w_inT, conv_w, *, tm, tc):
    m, d = a2d.shape
    kern = functools.partial(_inproj_sc_kernel, tm=tm)

    def wspec(k):
        return pl.BlockSpec((pl.Element(tc), pl.Element(d)),
                            lambda i, n: (pl.multiple_of(D_SSM_IN + k * D_SC + n * tc, 16), 0))

    return pl.pallas_call(
        kern,
        grid=(m // tm, D_SC // tc),
        in_specs=[pl.BlockSpec((tm, d), lambda i, n: (i, 0)),
                  wspec(0), wspec(1), wspec(2),
                  pl.BlockSpec((3, tc), lambda i, n: (0, n))],
        out_specs=pl.BlockSpec((tm, tc), lambda i, n: (i, n)),
        out_shape=jax.ShapeDtypeStruct((m, D_SC), BF16),
        compiler_params=_cparams(("arbitrary", "arbitrary")),
        name="inproj_sc",
    )(a2d, w_inT, w_inT, w_inT, conv_w)


def _ssd_kernel(*refs, reverse, mode, nblk):
    xbc_ref, dt_ref, dtT_ref, alr_ref, alc_ref, e_ref, init_ref = refs[:7]
    rest = refs[7:]
    if mode == "state":
        out_ref, s_ref = rest
    elif mode == "partial":
        out_ref, s_ref = rest
    else:
        z_ref, yo_ref, dsk_ref, nw_ref, out_ref, s_ref = rest
    j = pl.program_id(1)

    @pl.when(j == 0)
    def _():
        s_ref[...] = init_ref[0]

    ii = lax.broadcasted_iota(I32, (SSD_CHUNK, SSD_CHUNK), 0)
    jj = lax.broadcasted_iota(I32, (SSD_CHUNK, SSD_CHUNK), 1)
    tri = jnp.where(jj <= ii, 1.0, 0.0).astype(F32)
    triT = jnp.where(ii <= jj, 1.0, 0.0).astype(F32)
    mask = (jj >= ii) if reverse else (ii >= jj)
    lane4 = lax.broadcasted_iota(I32, (SSD_CHUNK, 4 * SSM_HEAD_DIM), 1) // SSM_HEAD_DIM
    aneg_r = -jnp.exp(alr_ref[...]) * LOG2E
    aneg_c = -jnp.exp(alc_ref[...]) * LOG2E
    lane0 = SSM_HEADS if reverse else 0
    hi = lax.Precision.HIGHEST

    def chunk(c):
        rows = slice(c * SSD_CHUNK, (c + 1) * SSD_CHUNK)
        dt = dt_ref[rows, :]
        dtT = dtT_ref[:, rows]
        la = dt * aneg_r
        laT = dtT * aneg_c
        cs = jnp.dot(tri, la, precision=hi, preferred_element_type=F32)
        csT = jnp.dot(laT, triT, precision=hi, preferred_element_type=F32)
        total = cs[SSD_CHUNK - 1:SSD_CHUNK, :]
        log2_dtT = jnp.log(dtT) * LOG2E
        if reverse:
            ecs = cs - la
            col, rowm = -ecs, csT - laT + log2_dtT
            yscale = jnp.exp2(total - ecs)
            w = dt * jnp.exp2(ecs)
        else:
            col, rowm = cs, log2_dtT - csT
            yscale = jnp.exp2(cs)
            w = dt * jnp.exp2(total - cs)
        t1 = total.astype(BF16).astype(F32)
        r1 = total - t1
        t2 = r1.astype(BF16).astype(F32)
        t3 = r1 - t2
        r16 = lax.broadcasted_iota(I32, (16, LANES), 0)
        tail = jnp.where(r16 == 0, t1, jnp.where(r16 == 1, t2, jnp.where(r16 == 2, t3, 0.0)))
        stack = jnp.concatenate([yscale, w, tail], axis=0).astype(BF16)
        ex = jnp.dot(stack, e_ref[...], preferred_element_type=F32)
        yscale_x = ex[0:SSD_CHUNK]
        w_x = ex[SSD_CHUNK:2 * SSD_CHUNK]
        tb = 2 * SSD_CHUNK
        sdec_x = jnp.exp2(ex[tb:tb + 1] + ex[tb + 1:tb + 2] + ex[tb + 2:tb + 3])

        for g in range(SSM_GROUPS):
            gsl = slice(g * GROUP_W, (g + 1) * GROUP_W)
            xg = xbc_ref[rows, gsl]
            bg = xbc_ref[rows, D_SSM + g * SSM_STATE:D_SSM + (g + 1) * SSM_STATE]
            s_old = s_ref[g]
            if mode != "state":
                cg = xbc_ref[rows, D_SSM + GN + g * SSM_STATE:D_SSM + GN + (g + 1) * SSM_STATE]
                cb = lax.dot_general(cg, bg, (((1,), (1,)), ((), ())), preferred_element_type=F32)
                y_off = jnp.dot(cg, s_old.astype(BF16), preferred_element_type=F32)
                halves = []
                lane2 = lax.broadcasted_iota(I32, (SSD_CHUNK, 2 * SSM_HEAD_DIM), 1) // SSM_HEAD_DIM
                for q in range(4):
                    xq = xg[:, q * 128:(q + 1) * 128]
                    mhs, xms = [], []
                    for r4 in range(2):
                        lane = lane0 + g * SSM_HPG + q * 2 + r4
                        seg = col[:, lane:lane + 1] + rowm[lane:lane + 1, :]
                        dec = jnp.exp2(jnp.where(mask, seg, -jnp.inf))
                        mhs.append((cb * dec).astype(BF16))
                        xms.append(jnp.where(lane2 == r4, xq, jnp.zeros_like(xq)))
                    halves.append(jnp.dot(jnp.concatenate(mhs, axis=1), jnp.concatenate(xms, axis=0),
                                          preferred_element_type=F32))
                y = jnp.concatenate(halves, axis=1) + yscale_x[:, gsl] * y_off
                if mode == "partial":
                    out_ref[rows, gsl] = y.astype(BF16)
                else:
                    y = y + yo_ref[rows, gsl].astype(F32) + dsk_ref[:, gsl] * xg.astype(F32)
                    y = y * _silu(z_ref[rows, gsl].astype(F32))
                    ms = jnp.mean(y * y, axis=-1, keepdims=True)
                    out_ref[rows, gsl] = (y * lax.rsqrt(ms + EPS) * nw_ref[:, gsl]).astype(BF16)
            xw = (xg.astype(F32) * w_x[:, gsl]).astype(BF16)
            upd = lax.dot_general(bg, xw, (((0,), (0,)), ((), ())), preferred_element_type=F32)
            s_ref[g] = sdec_x[:, gsl] * s_old + upd

    for c in ((1, 0) if reverse else (0, 1)):
        chunk(c)

    if mode == "state":
        @pl.when(j == nblk - 1)
        def _():
            out_ref[0] = s_ref[...]


def _ssd_call(xbc, dt, dtT, prm, init, *, batch, nblk, reverse, mode, extra=()):
    m = xbc.shape[0]

    def blk(b, j):
        return b * nblk + (nblk - 1 - j if reverse else j)

    state_spec = pl.BlockSpec((1, SSM_GROUPS, SSM_STATE, GROUP_W), lambda b, j: (b, 0, 0, 0))
    const2 = lambda b, j: (0, 0)
    in_specs = [pl.BlockSpec((BLK, D_XBC), lambda b, j: (blk(b, j), 0)),
                pl.BlockSpec((BLK, LANES), lambda b, j: (blk(b, j), 0)),
                pl.BlockSpec((LANES, BLK), lambda b, j: (0, blk(b, j))),
                pl.BlockSpec((1, LANES), const2), pl.BlockSpec((LANES, 1), const2),
                pl.BlockSpec((LANES, D_SSM), const2),
                state_spec]
    args = [xbc, dt, dtT, prm["alog_row"], prm["alog_col"],
            prm["expand_rev" if reverse else "expand_fwd"], init]
    if mode == "state":
        out_specs = state_spec
        out_shape = jax.ShapeDtypeStruct((batch, SSM_GROUPS, SSM_STATE, GROUP_W), F32)
    else:
        out_specs = pl.BlockSpec((BLK, D_SSM), lambda b, j: (blk(b, j), 0))
        out_shape = jax.ShapeDtypeStruct((m, D_SSM), BF16)
    if mode == "final":
        z, y_other = extra
        in_specs += [pl.BlockSpec((BLK, D_SSM), lambda b, j: (blk(b, j), 0)),
                     pl.BlockSpec((BLK, D_SSM), lambda b, j: (blk(b, j), 0)),
                     pl.BlockSpec((1, D_SSM), const2), pl.BlockSpec((1, D_SSM), const2)]
        args += [z, y_other, prm["dskip"], prm["norm_w"]]
    kern = functools.partial(_ssd_kernel, reverse=reverse, mode=mode, nblk=nblk)
    return pl.pallas_call(
        kern,
        grid=(batch, nblk),
        in_specs=in_specs,
        out_specs=out_specs,
        out_shape=out_shape,
        scratch_shapes=[pltpu.VMEM((SSM_GROUPS, SSM_STATE, GROUP_W), F32)],
        compiler_params=_cparams(("arbitrary", "arbitrary")),
        name=f"ssd_{mode}_{'rev' if reverse else 'fwd'}",
    )(*args)


OUT_NCHUNK = 1024


def _outproj_kernel(ys_ref, yc_ref, scw_ref, w_ref, x_ref, g1_ref, nw_ref, sh_ref, sc_ref, wr_ref,
                    fh_ref, aff_ref, *, tm, sub):
    nc = OUT_NCHUNK
    chunks = [slice(c * nc, (c + 1) * nc) for c in range(D_MODEL // nc)]
    g1, nw, sh, sc = g1_ref[0], nw_ref[...], sh_ref[0], sc_ref[0]

    def rows_of(k):
        return slice(k * sub, (k + 1) * sub)

    def prologue(k):
        yc = yc_ref[rows_of(k), :].astype(F32)
        ms = jnp.mean(yc * yc, axis=-1, keepdims=True)
        return ys_ref[rows_of(k), :], (yc * lax.rsqrt(ms + EPS) * scw_ref[...]).astype(BF16)

    def pass1(k, ys, ycn, cols):
        mix = jnp.dot(ys, w_ref[0:D_SSM, cols], preferred_element_type=F32)
        mix = mix + jnp.dot(ycn, w_ref[D_SSM:, cols], preferred_element_type=F32)
        h = x_ref[rows_of(k), cols] + g1[:, cols] * mix
        fh_ref[rows_of(k), slice(D_MODEL + cols.start, D_MODEL + cols.stop)] = h
        return jnp.sum(h * h, axis=-1, keepdims=True)

    def pass2(k, rstd, cols):
        h = fh_ref[rows_of(k), slice(D_MODEL + cols.start, D_MODEL + cols.stop)]
        f = h * rstd * nw[:, cols] * (1.0 + sc[:, cols]) + sh[:, cols]
        fh_ref[rows_of(k), cols] = f
        return _dot_nt(wr_ref[:, cols], f.astype(BF16))

    def finish(k, logits):
        mx = jnp.max(logits, axis=0, keepdims=True)
        ex = jnp.exp(logits - mx)
        aff_ref[0, :, rows_of(k)] = ex / jnp.sum(ex, axis=0, keepdims=True)

    prev = None
    for k in range(tm // sub):
        ys, ycn = prologue(k)
        ssq, logits = None, None
        for cols in chunks:
            part = pass1(k, ys, ycn, cols)
            ssq = part if ssq is None else ssq + part
            if prev is not None:
                lg = pass2(prev[0], prev[1], cols)
                logits = lg if logits is None else logits + lg
        if prev is not None:
            finish(prev[0], logits)
        prev = (k, lax.rsqrt(ssq * (1.0 / D_MODEL) + EPS))
    logits = None
    for cols in chunks:
        lg = pass2(prev[0], prev[1], cols)
        logits = lg if logits is None else logits + lg
    finish(prev[0], logits)


def _outproj_call(y_ssm, y_sc, sc_norm_w, w_out, x2d, g1, norm_w, shift, scale, w_rT, *, batch, tm):
    m, d = x2d.shape
    n_tok = m // batch
    per = n_tok // tm
    vec = lambda i: (i // per, 0, 0)
    const2 = lambda i: (0, 0)
    sub = min(tm, 256)
    return pl.pallas_call(
        functools.partial(_outproj_kernel, tm=tm, sub=sub),
        grid=(m // tm,),
        in_specs=[pl.BlockSpec((tm, D_SSM), lambda i: (i, 0)),
                  pl.BlockSpec((tm, D_SC), lambda i: (i, 0)),
                  pl.BlockSpec((1, D_SC), const2),
                  pl.BlockSpec((D_SSM + D_SC, d), const2, pipeline_mode=pl.Buffered(1)),
                  pl.BlockSpec((tm, d), lambda i: (i, 0)),
                  pl.BlockSpec((1, 1, d), vec),
                  pl.BlockSpec((1, d), const2),
                  pl.BlockSpec((1, 1, d), vec),
                  pl.BlockSpec((1, 1, d), vec),
                  pl.BlockSpec((N_EXPERTS, d), const2)],
        out_specs=[pl.BlockSpec((tm, 2 * d), lambda i: (i, 0)),
                   pl.BlockSpec((1, N_EXPERTS, tm), lambda i: (i // per, 0, i % per))],
        out_shape=[jax.ShapeDtypeStruct((m, 2 * d), F32),
                   jax.ShapeDtypeStruct((batch, N_EXPERTS, n_tok), F32)],
        compiler_params=_cparams(("arbitrary",)),
        name="outproj",
    )(y_ssm, y_sc, sc_norm_w, w_out, x2d, g1, norm_w, shift, scale, w_rT)


def _prefix_lanes(m01):
    rows, n = m01.shape
    ki = lax.broadcasted_iota(I32, (LANES, LANES), 0)
    ci = lax.broadcasted_iota(I32, (LANES, LANES), 1)
    triu = jnp.where(ki <= ci, 1.0, 0.0).astype(BF16)
    off = jnp.zeros((rows, 1), F32)
    parts = []
    for k in range(n // LANES):
        p = jnp.dot(m01[:, k * LANES:(k + 1) * LANES].astype(BF16), triu,
                    preferred_element_type=F32)
        parts.append(p + off)
        off = off + p[:, LANES - 1:LANES]
    return jnp.concatenate(parts, axis=1)


def _route_kernel(aff_ref, idx_ref, gcol_ref, *, cap):
    a = aff_ref[0]
    n_e, n = a.shape
    thr = jnp.zeros((n_e, 1), I32)
    for bit in range(30, -1, -1):
        cand = thr | (1 << bit)
        cand_f = lax.bitcast_convert_type(cand, F32)
        cnt = jnp.sum(jnp.where(a >= cand_f, 1.0, 0.0), axis=1, keepdims=True)
        thr = jnp.where(cnt >= cap, cand, thr)
    thr_f = lax.bitcast_convert_type(thr, F32)
    gt = a > thr_f
    eq = a == thr_f
    need = cap - jnp.sum(jnp.where(gt, 1.0, 0.0), axis=1, keepdims=True)
    eqf = jnp.where(eq, 1.0, 0.0)
    ties_before = _prefix_lanes(eqf) - eqf
    sel = jnp.logical_or(gt, jnp.logical_and(eq, ties_before < need))
    self_ = jnp.where(sel, 1.0, 0.0)
    slot = jnp.where(sel, _prefix_lanes(self_) - 1.0, -1.0).astype(I32)

    tok = lax.broadcasted_iota(I32, (1, n), 1)
    t_hi = (tok // 64).astype(F32)
    t_lo = (tok % 64).astype(F32)
    a1 = a.astype(BF16).astype(F32)
    ra = a - a1
    a2 = ra.astype(BF16).astype(F32)
    a3 = ra - a2
    srow = lax.broadcasted_iota(I32, (cap, n), 0)
    r8 = lax.broadcasted_iota(I32, (8, n), 0)
    zpad = jnp.zeros((LANES - 8, n), F32)
    for e in range(n_e):
        onehot = jnp.where(slot[e:e + 1, :] == srow, 1.0, 0.0).astype(BF16)
        v8 = jnp.where(r8 == 0, t_hi, jnp.where(r8 == 1, t_lo, jnp.where(
            r8 == 2, a1[e:e + 1], jnp.where(r8 == 3, a2[e:e + 1], jnp.where(
                r8 == 4, a3[e:e + 1], 0.0)))))
        vals = jnp.concatenate([v8, zpad], axis=0).astype(BF16)
        o_col = lax.dot_general(onehot, vals, (((1,), (1,)), ((), ())),
                                preferred_element_type=F32)
        o_row = o_col.T
        idx_ref[0, e:e + 1, :] = (o_row[0:1] * 64.0 + o_row[1:2]).astype(I32)
        gcol_ref[0, e] = o_col


def _route_call(aff, *, cap):
    batch, n_e, n = aff.shape
    return pl.pallas_call(
        functools.partial(_route_kernel, cap=cap),
        grid=(batch,),
        in_specs=[pl.BlockSpec((1, n_e, n), lambda b: (b, 0, 0))],
        out_specs=[pl.BlockSpec((1, n_e, cap), lambda b: (b, 0, 0)),
                   pl.BlockSpec((1, n_e, cap, LANES), lambda b: (b, 0, 0, 0))],
        out_shape=[jax.ShapeDtypeStruct((batch, n_e, cap), I32),
                   jax.ShapeDtypeStruct((batch, n_e, cap, LANES), F32)],
        compiler_params=_cparams(("arbitrary",)),
        name="route",
    )(aff)


MOE_PARTS = 4
MOE_NCHUNK = 512
MOE_WROWS_A = 512
MOE_WROWS_D = 128
MOE_WPIECES = 4
MOE_WPRIO = 1


def _moe_kernel(idx_ref, gcol_ref, g2_ref, wg_hbm, wu_hbm, wd_hbm, fh_in, fh_out,
                buf0, buf1, wg_s, wu_s, wd_s, stg_a, stg_d, sem, *, cap, n_tok, n_pairs):
    e = pl.program_id(0)
    i = pl.program_id(1)
    n_e = pl.num_programs(0)
    first = jnp.logical_and(e == 0, i == 0)
    last = jnp.logical_and(e == n_e - 1, i == n_pairs - 1)
    i_next = jnp.where(i == n_pairs - 1, 0, i + 1)
    e_next = jnp.where(i == n_pairs - 1, e + 1, e)
    bufs = (buf0, buf1)
    part = cap // MOE_PARTS
    hcols = pl.ds(D_MODEL, D_MODEL)
    cur = e % 2

    def row_of(ex, b, jrow):
        return b * n_tok + idx_ref[(b * N_EXPERTS + ex) * cap + jrow]

    def gather_start(ex, b, s, jrow):
        pltpu.make_async_copy(fh_in.at[pl.ds(row_of(ex, b, jrow), 1)], bufs[s].at[pl.ds(jrow, 1)],
                              sem.at[s]).start()

    def scatter_start(ex, b, s, jrow):
        pltpu.make_async_copy(bufs[s].at[pl.ds(jrow, 1), hcols],
                              fh_out.at[pl.ds(row_of(ex, b, jrow), 1), hcols],
                              sem.at[2 + s]).start()

    def wait_gather(s):
        pltpu.make_async_copy(fh_in.at[pl.ds(0, cap)], bufs[s], sem.at[s]).wait()

    def wait_scatter(s):
        pltpu.make_async_copy(bufs[s].at[:, hcols], fh_out.at[pl.ds(0, cap), hcols],
                              sem.at[2 + s]).wait()

    g_units = (D_MODEL // MOE_WROWS_A) // n_pairs
    d_units = (EXPERT_FF // MOE_WROWS_D) // n_pairs
    units = [("g", j) for j in range(g_units)] + [("u", j) for j in range(g_units)] \
        + [("d", j) for j in range(d_units)]
    n_parts = 2 * MOE_PARTS
    upp = len(units) // n_parts
    pieces = MOE_WPIECES // upp

    def unit_refs(kind, j, tgt, step, slot):
        if kind == "d":
            row0 = pl.multiple_of((step * d_units + j) * MOE_WROWS_D, MOE_WROWS_D)
            return (wd_hbm.at[tgt, pl.ds(row0, MOE_WROWS_D), :], stg_d.at[slot], wd_s, row0,
                    MOE_WROWS_D)
        src = wg_hbm if kind == "g" else wu_hbm
        row0 = pl.multiple_of((step * g_units + j) * MOE_WROWS_A, MOE_WROWS_A)
        return (src.at[tgt, pl.ds(row0, MOE_WROWS_A), :], stg_a.at[slot],
                wg_s if kind == "g" else wu_s, row0, MOE_WROWS_A)

    def unit_copy(t, tgt, step):
        kind, j = units[t]
        src, dst, _, _, _ = unit_refs(kind, j, tgt, step, t % 2)
        return pltpu.make_async_copy(src, dst, sem.at[4 + t % 2])

    def unit_cast(t, step, wslot, piece, n_pieces):
        kind, j = units[t]
        _, stg, dst, row0, rows = unit_refs(kind, j, 0, step, t % 2)
        pr = rows // n_pieces
        dst[wslot, pl.ds(row0 + piece * pr, pr), :] = stg[piece * pr:(piece + 1) * pr, :].astype(BF16)

    tgt_now = jnp.minimum(e + 1, n_e - 1)
    tgt_next = jnp.minimum(e_next + 1, n_e - 1)

    def unit_start_ahead(t):
        t2 = t + 2
        if t2 < len(units):
            unit_copy(t2, tgt_now, i).start(priority=MOE_WPRIO)
        else:
            unit_copy(t2 - len(units), tgt_next, i_next).start(priority=MOE_WPRIO)

    def weight_jobs(p):
        jobs = []
        for c in range(upp):
            t = p * upp + c
            for piece in range(pieces):
                def job(t=t, piece=piece):
                    if piece == 0:
                        unit_copy(t, tgt_now, i).wait()
                    unit_cast(t, i, 1 - cur, piece, pieces)
                jobs.append(job)
            jobs.append(functools.partial(unit_start_ahead, t))
        return jobs

    def compute_part(s, b_local, q, between, wjobs):
        rows = slice(q * part, (q + 1) * part)
        xq = bufs[s][rows, 0:D_MODEL].astype(BF16)
        gc = gcol_ref[b_local, 0, rows, :]
        gate = gc[:, 2:3] + gc[:, 3:4] + gc[:, 4:5]
        g2 = g2_ref[2 * i + b_local]
        nc = MOE_NCHUNK
        hid = []
        slot_no = 0

        def tick():
            nonlocal slot_no
            between[slot_no]()
            if slot_no < len(wjobs):
                wjobs[slot_no]()
            slot_no += 1

        for c in range(EXPERT_FF // nc):
            cols = slice(c * nc, (c + 1) * nc)
            tick()
            hg = jnp.dot(xq, wg_s[cur, :, cols], preferred_element_type=F32)
            tick()
            hu = jnp.dot(xq, wu_s[cur, :, cols], preferred_element_type=F32)
            hid.append((_silu(hg) * hu).astype(BF16))
        hid = jnp.concatenate(hid, axis=1)
        for c in range(D_MODEL // nc):
            cols = slice(c * nc, (c + 1) * nc)
            hc = slice(D_MODEL + c * nc, D_MODEL + (c + 1) * nc)
            tick()
            ye = jnp.dot(hid, wd_s[cur, :, cols], preferred_element_type=F32)
            bufs[s][rows, hc] = bufs[s][rows, hc] + g2[:, cols] * (ye * gate)

    n_between = 2 * (EXPERT_FF // MOE_NCHUNK) + D_MODEL // MOE_NCHUNK

    def spread(jobs):
        n = len(jobs)
        cuts = [n * k // n_between for k in range(n_between + 1)]

        def group(k):
            def run():
                for job in jobs[cuts[k]:cuts[k + 1]]:
                    job()
            return run
        return [group(k) for k in range(n_between)]

    b_a, b_b = 2 * i, 2 * i + 1
    b_n = 2 * i_next

    @pl.when(first)
    def _():
        @pl.loop(0, cap)
        def _(jrow):
            gather_start(e, b_a, 0, jrow)
        for step in range(n_pairs):
            for t in range(len(units)):
                kind, j = units[t]
                src, _, _, _, _ = unit_refs(kind, j, 0, step, 0)
                stg0 = stg_d.at[0] if kind == "d" else stg_a.at[0]
                cp = pltpu.make_async_copy(src, stg0, sem.at[4])
                cp.start()
                cp.wait()
                _, _, dst, row0, rows = unit_refs(kind, j, 0, step, 0)
                dst[0, pl.ds(row0, rows), :] = (stg_d[0] if kind == "d" else stg_a[0]).astype(BF16)
        unit_copy(0, tgt_now, i).start(priority=MOE_WPRIO)
        unit_copy(1, tgt_now, i).start(priority=MOE_WPRIO)

    wait_gather(0)
    fill = {1: range(0, cap // 2), 2: range(cap // 2, cap)}
    for q in range(MOE_PARTS):
        jobs = []
        if q > 0:
            jobs += [functools.partial(scatter_start, e, b_a, 0, jrow)
                     for jrow in range((q - 1) * part, q * part)]
        jobs += [functools.partial(gather_start, e, b_b, 1, jrow) for jrow in fill.get(q, ())]
        if q == 1:
            @pl.when(jnp.logical_not(first))
            def _():
                wait_scatter(1)
        compute_part(0, 0, q, spread(jobs), weight_jobs(q))

    wait_gather(1)
    for q in range(MOE_PARTS):
        if q == 0:
            jobs = [functools.partial(scatter_start, e, b_a, 0, jrow)
                    for jrow in range((MOE_PARTS - 1) * part, cap)]
        else:
            jobs = [functools.partial(scatter_start, e, b_b, 1, jrow)
                    for jrow in range((q - 1) * part, q * part)]
        jobs += [functools.partial(gather_start, e_next, b_n, 0, jrow) for jrow in fill.get(q, ())]
        if q == 1:
            wait_scatter(0)
        compute_part(1, 1, q, spread(jobs), weight_jobs(MOE_PARTS + q))

    for jrow in range((MOE_PARTS - 1) * part, cap):
        scatter_start(e, b_b, 1, jrow)

    @pl.when(last)
    def _():
        wait_scatter(1)
        wait_gather(0)
        unit_copy(0, tgt_next, i_next).wait()
        unit_copy(1, tgt_next, i_next).wait()


def _moe_call(idx_flat, gcol, g2, wg, wu, wd, fh, *, batch, cap):
    m, d = fh.shape[0], fh.shape[1] // 2
    n_tok = m // batch
    assert batch % 2 == 0 and cap % MOE_PARTS == 0
    n_pairs = batch // 2
    assert n_pairs in (1, 2)
    kern = functools.partial(_moe_kernel, cap=cap, n_tok=n_tok, n_pairs=n_pairs)
    grid_spec = pltpu.PrefetchScalarGridSpec(
        num_scalar_prefetch=1,
        grid=(N_EXPERTS, n_pairs),
        in_specs=[pl.BlockSpec((2, 1, cap, LANES), lambda e, i, idx: (i, e, 0, 0)),
                  pl.BlockSpec((batch, 1, d), lambda e, i, idx: (0, 0, 0)),
                  pl.BlockSpec(memory_space=pl.ANY), pl.BlockSpec(memory_space=pl.ANY),
                  pl.BlockSpec(memory_space=pl.ANY), pl.BlockSpec(memory_space=pl.ANY)],
        out_specs=pl.BlockSpec(memory_space=pl.ANY),
        scratch_shapes=[pltpu.VMEM((cap, 2 * d), F32), pltpu.VMEM((cap, 2 * d), F32),
                        pltpu.VMEM((2, d, EXPERT_FF), BF16), pltpu.VMEM((2, d, EXPERT_FF), BF16),
                        pltpu.VMEM((2, EXPERT_FF, d), BF16),
                        pltpu.VMEM((2, MOE_WROWS_A, EXPERT_FF), F32),
                        pltpu.VMEM((2, MOE_WROWS_D, d), F32),
                        pltpu.SemaphoreType.DMA((6,))],
    )
    return pl.pallas_call(
        kern,
        grid_spec=grid_spec,
        out_shape=jax.ShapeDtypeStruct((m, 2 * d), F32),
        input_output_aliases={6: 0},
        compiler_params=_cparams(("arbitrary", "arbitrary")),
        name="moe",
    )(idx_flat, gcol, g2, wg, wu, wd, fh)


def _final_kernel(h_ref, w_ref, o_ref):
    h = h_ref[...]
    ms = jnp.mean(h * h, axis=-1, keepdims=True)
    o_ref[...] = h * lax.rsqrt(ms + EPS) * w_ref[...]


def _final_call(fh, w, *, tm):
    m, d = fh.shape[0], fh.shape[1] // 2
    return pl.pallas_call(
        _final_kernel,
        grid=(m // tm,),
        in_specs=[pl.BlockSpec((tm, d), lambda i: (i, 1)), pl.BlockSpec((1, d), lambda i: (0, 0))],
        out_specs=pl.BlockSpec((tm, d), lambda i: (i, 0)),
        out_shape=jax.ShapeDtypeStruct((m, d), F32),
        compiler_params=_cparams(("arbitrary",)),
        name="final_norm",
    )(fh, w)


def _pick_tile(n, pref):
    t = pref
    while n % t:
        t //= 2
    return t


def _ssd_params(a_log, dt_bias, d_skip, norm_w):
    padw = LANES - N_DT
    alog = jnp.pad(a_log.astype(F32), (0, padw))
    bias = jnp.pad(dt_bias.astype(F32), (0, padw))
    lane = jnp.arange(LANES)[:, None]
    head = jnp.arange(D_SSM)[None, :] // SSM_HEAD_DIM
    return {
        "alog_row": alog[None, :], "alog_col": alog[:, None],
        "bias_row": bias[None, :], "bias_col": bias[:, None],
        "expand_fwd": (lane == head).astype(BF16),
        "expand_rev": (lane == head + SSM_HEADS).astype(BF16),
        "dskip": jnp.repeat(d_skip.astype(F32), SSM_HEAD_DIM)[None, :],
        "norm_w": norm_w.astype(F32)[None, :],
    }


def kernel(x, c, ctx, c_ctx, w_mod, b_mod, norm_mix_w, w_in, ssm_conv_w, ssm_conv_b, ssm_a_log,
           ssm_dt_bias, ssm_d, ssm_norm_w, sc_conv_w, sc_norm_w, w_out, norm_ffn_w, w_router,
           w_gate, w_up, w_down, final_norm_w):
    batch, n_lat, d = x.shape
    n_ctx = ctx.shape[1]
    depth = w_mod.shape[0]
    assert depth == 1 and d == D_MODEL and n_ctx == BLK and n_lat % BLK == 0
    layer = 0
    cap = EC_CAPACITY_FACTOR * n_lat // N_EXPERTS

    pad_rows = -(batch + 1) % 8
    cvec = jnp.concatenate([c, c_ctx[None, :], jnp.zeros((pad_rows, d), F32)], axis=0)
    mod = _mod_call(cvec, w_mod[layer], b_mod[layer][None, :])
    sh1, sc1, g1, sh2, sc2, g2 = [mod[:batch, k * d:(k + 1) * d][:, None, :] for k in range(N_MOD)]
    sh1_c = mod[batch:batch + 1, 0:d][:, None, :]
    sc1_c = mod[batch:batch + 1, d:2 * d][:, None, :]

    w_inT = w_in[layer].T.astype(BF16)
    tc = 512
    w_out_b = w_out[layer].astype(BF16)
    w_rT = w_router[layer].T.astype(BF16)
    wg, wu, wd = w_gate[layer], w_up[layer], w_down[layer]
    norm_mix = norm_mix_w[layer][None, :]
    conv_w, conv_b = ssm_conv_w[layer], ssm_conv_b[layer][None, :]
    prm = _ssd_params(ssm_a_log[layer], ssm_dt_bias[layer], ssm_d[layer], ssm_norm_w[layer])

    ctx2d = ctx.reshape(batch * n_ctx, d)
    xbc_c, dt_c, dtT_c = _inproj_zx_call(
        ctx2d, sh1_c, sc1_c, norm_mix, w_inT, conv_w, conv_b, prm["bias_row"], prm["bias_col"],
        with_z=False, period=n_ctx, tm=_pick_tile(batch * n_ctx, 1024), rows_per_mod=batch * n_ctx)
    zero_state = jnp.zeros((batch, SSM_GROUPS, SSM_STATE, GROUP_W), F32)
    st_f = _ssd_call(xbc_c, dt_c, dtT_c, prm, zero_state, batch=batch, nblk=1, reverse=False,
                     mode="state")
    st_r = _ssd_call(xbc_c, dt_c, dtT_c, prm, zero_state, batch=batch, nblk=1, reverse=True,
                     mode="state")

    x2d = x.reshape(batch * n_lat, d)
    tm = _pick_tile(n_lat, 1024)
    z, xbc, dt, dtT, a_lat = _inproj_zx_call(
        x2d, sh1, sc1, norm_mix, w_inT, conv_w, conv_b, prm["bias_row"], prm["bias_col"],
        with_z=True, period=GRID_W, tm=tm, rows_per_mod=n_lat)
    y_sc = _inproj_sc_call(a_lat, w_inT, sc_conv_w[layer], tm=tm, tc=tc)
    nblk = n_lat // BLK
    y_rev = _ssd_call(xbc, dt, dtT, prm, st_r, batch=batch, nblk=nblk, reverse=True, mode="partial")
    y_ssm = _ssd_call(xbc, dt, dtT, prm, st_f, batch=batch, nblk=nblk, reverse=False, mode="final",
                      extra=(z, y_rev))

    fh, aff = _outproj_call(
        y_ssm, y_sc, sc_norm_w[layer][None, :], w_out_b, x2d, g1, norm_ffn_w[layer][None, :],
        sh2, sc2, w_rT, batch=batch, tm=_pick_tile(n_lat, 512))
    idx, gcol = _route_call(aff, cap=cap)
    fh = _moe_call(idx.reshape(-1), gcol, g2, wg, wu, wd, fh, batch=batch, cap=cap)
    out = _final_call(fh, final_norm_w[None, :], tm=_pick_tile(batch * n_lat, 512))
    return out.reshape(batch, n_lat, d)
```
